```python
import jax
import jax.numpy as jnp
from jax import lax
import numpy as np

D_MODEL = 1024
BATCH = 2
SEQ = 8192
DEPTH = 2

N_MIXERS = 2
N_CONV_LAYERS = (DEPTH + N_MIXERS - 1) // N_MIXERS
N_NSA_LAYERS = DEPTH // N_MIXERS

D_FF = 2816
FFN_RESIDUAL_WEIGHT = 0.5
CONV_WIDTH = 31
NSA_HEADS = 16
NSA_KV_GROUPS = 4
NSA_REP = NSA_HEADS // NSA_KV_GROUPS
HEAD_DIM = D_MODEL // NSA_HEADS
ROT_DIM = HEAD_DIM // 4
ROPE_THETA = 500000.0
CMP_LEN = 32
CMP_STRIDE = 16
CMP_HIDDEN = 4 * HEAD_DIM
SEL_BLOCK = 64
N_SEL = 16
WINDOW = 512
Q_BLOCK = 128
N_GATES = 3
Q_WIDTH = NSA_HEADS * HEAD_DIM
KV_WIDTH = NSA_KV_GROUPS * HEAD_DIM
NSA_IN_WIDTH = Q_WIDTH + 6 * KV_WIDTH + NSA_HEADS * N_GATES
RMS_EPS = 1e-6
LN_EPS = 1e-5
NEG_INF = -1e30

kernel_name = 'hybrid_conformer_conv_nsa_macaron'


def rms_norm(x, gain):
    xf = x.astype(jnp.float32)
    y = xf * lax.rsqrt(jnp.mean(xf * xf, axis=-1, keepdims=True) + RMS_EPS)
    return (y * gain.astype(jnp.float32)).astype(x.dtype)


def layer_norm(x, gain, bias):
    xf = x.astype(jnp.float32)
    mu = jnp.mean(xf, axis=-1, keepdims=True)
    var = jnp.mean(jnp.square(xf - mu), axis=-1, keepdims=True)
    y = (xf - mu) * lax.rsqrt(var + LN_EPS)
    return (y * gain.astype(jnp.float32) + bias.astype(jnp.float32)).astype(x.dtype)


def masked_softmax(s, mask):
    s = jnp.where(mask, s.astype(jnp.float32), NEG_INF)
    p = jnp.exp(s - jnp.max(s, axis=-1, keepdims=True)) * mask
    return p / jnp.maximum(jnp.sum(p, axis=-1, keepdims=True), 1e-30)


def half_step_ffn(x, g_pre, g_post, w_gate, w_up, w_down):
    h = rms_norm(x, g_pre)
    h = (jax.nn.silu(h @ w_gate) * (h @ w_up)) @ w_down
    return x + FFN_RESIDUAL_WEIGHT * rms_norm(h, g_post)


def conformer_conv(x, w_pw1, b_pw1, w_dw, b_dw, ln_g, ln_b, w_pw2, b_pw2):
    h = x @ w_pw1 + b_pw1
    a, g = jnp.split(h, 2, axis=-1)
    h = a * jax.nn.sigmoid(g)
    h = lax.conv_general_dilated(
        h, w_dw[:, None, :], window_strides=(1,), padding=[(CONV_WIDTH - 1, 0)],
        dimension_numbers=('NWC', 'WIO', 'NWC'), feature_group_count=h.shape[-1]) + b_dw
    h = jax.nn.silu(layer_norm(h, ln_g, ln_b))
    return h @ w_pw2 + b_pw2


def rope_tables(seq_len, dtype):
    pos = jnp.arange(seq_len, dtype=jnp.float32)
    inv_freq = ROPE_THETA ** (-jnp.arange(0, ROT_DIM, 2, dtype=jnp.float32) / ROT_DIM)
    ang = pos[:, None] * inv_freq[None, :]
    return jnp.cos(ang).astype(dtype), jnp.sin(ang).astype(dtype)


def partial_rope(x, cos, sin):
    xr, xp = x[..., :ROT_DIM], x[..., ROT_DIM:]
    x1, x2 = xr[..., :ROT_DIM // 2], xr[..., ROT_DIM // 2:]
    return jnp.concatenate([x1 * cos - x2 * sin, x2 * cos + x1 * sin, xp], axis=-1)


def compress_tokens(kv, pos_emb, w1, w2):
    b, g, s, dh = kv.shape
    nc = (s - CMP_LEN) // CMP_STRIDE + 1
    idx = jnp.arange(nc)[:, None] * CMP_STRIDE + jnp.arange(CMP_LEN)[None, :]
    blocks = kv[:, :, idx] + pos_emb
    flat = blocks.reshape(b, g, nc, CMP_LEN * dh)
    return jax.nn.silu(flat @ w1) @ w2


def native_sparse_attention(x, w_in, b_gate, cmp_pos, cmp_w1, cmp_w2, w_out):
    b, s, _ = x.shape
    G, R, dh = NSA_KV_GROUPS, NSA_REP, HEAD_DIM
    proj = x @ w_in
    q = proj[..., :Q_WIDTH].reshape(b, s, G, R, dh).transpose(0, 2, 3, 1, 4)
    kv = proj[..., Q_WIDTH:Q_WIDTH + 6 * KV_WIDTH].reshape(b, s, 6, G, dh).transpose(2, 0, 3, 1, 4)
    gates = jax.nn.sigmoid(proj[..., Q_WIDTH + 6 * KV_WIDTH:] + b_gate)
    gates = gates.reshape(b, s, G, R, N_GATES).transpose(0, 2, 3, 1, 4)

    cos, sin = rope_tables(s, x.dtype)
    q_rot = partial_rope(q, cos, sin)
    k_sel = partial_rope(kv[2], cos, sin)
    v_sel = kv[3]
    k_win = partial_rope(kv[4], cos, sin)
    v_win = kv[5]
    k_cmp = compress_tokens(kv[0], cmp_pos[0], cmp_w1[0], cmp_w2[0])
    v_cmp = compress_tokens(kv[1], cmp_pos[1], cmp_w1[1], cmp_w2[1])

    nc = k_cmp.shape[2]
    nb = s // SEL_BLOCK
    nsel = min(N_SEL, nb)
    cmp_start = jnp.arange(nc) * CMP_STRIDE
    cmp_end = cmp_start + CMP_LEN - 1
    blk_ids = jnp.arange(nb)
    sel_start = blk_ids * SEL_BLOCK
    overlap = ((cmp_start[:, None] < sel_start[None, :] + SEL_BLOCK)
               & (cmp_start[:, None] + CMP_LEN > sel_start[None, :])).astype(jnp.float32)
    ks_blocks = k_sel.reshape(b, G, nb, SEL_BLOCK, dh)
    vs_blocks = v_sel.reshape(b, G, nb, SEL_BLOCK, dh)
    pad = ((0, 0), (0, 0), (WINDOW, 0), (0, 0))
    k_win_pad = jnp.pad(k_win, pad)
    v_win_pad = jnp.pad(v_win, pad)
    scale = HEAD_DIM ** -0.5
    b_idx = jnp.arange(b)[:, None, None, None]
    g_idx = jnp.arange(G)[None, :, None, None]

    def query_block(i):
        t0 = i * Q_BLOCK
        pos = t0 + jnp.arange(Q_BLOCK)
        qb = lax.dynamic_slice_in_dim(q, t0, Q_BLOCK, axis=3)
        qrb = lax.dynamic_slice_in_dim(q_rot, t0, Q_BLOCK, axis=3)
        gb = lax.dynamic_slice_in_dim(gates, t0, Q_BLOCK, axis=3)

        s_c = jnp.einsum('bgrqd,bgcd->bgrqc', qb, k_cmp) * scale
        p_c = masked_softmax(s_c, cmp_end[None, :] <= pos[:, None])
        o_cmp = jnp.einsum('bgrqc,bgcd->bgrqd', p_c.astype(v_cmp.dtype), v_cmp)

        imp = jnp.einsum('bgrqc,cn->bgqn', p_c, overlap)
        cur = pos // SEL_BLOCK
        forced = (blk_ids[None, :] == 0) | (blk_ids[None, :] == cur[:, None]) | (blk_ids[None, :] == cur[:, None] - 1)
        causal_blk = sel_start[None, :] <= pos[:, None]
        imp = jnp.where(forced, 1e9, jnp.where(causal_blk, imp, -1e9))
        _, sel = lax.top_k(imp, nsel)

        k_g = ks_blocks[b_idx, g_idx, sel].reshape(b, G, Q_BLOCK, nsel * SEL_BLOCK, dh)
        v_g = vs_blocks[b_idx, g_idx, sel].reshape(b, G, Q_BLOCK, nsel * SEL_BLOCK, dh)
        kpos = (sel[..., None] * SEL_BLOCK + jnp.arange(SEL_BLOCK)).reshape(b, G, Q_BLOCK, nsel * SEL_BLOCK)
        s_s = jnp.einsum('bgrqd,bgqkd->bgrqk', qrb, k_g) * scale
        p_s = masked_softmax(s_s, (kpos <= pos[None, None, :, None])[:, :, None])
        o_sel = jnp.einsum('bgrqk,bgqkd->bgrqd', p_s.astype(v_g.dtype), v_g)

        kw = lax.dynamic_slice_in_dim(k_win_pad, t0, Q_BLOCK + WINDOW, axis=2)
        vw = lax.dynamic_slice_in_dim(v_win_pad, t0, Q_BLOCK + WINDOW, axis=2)
        wpos = t0 - WINDOW + jnp.arange(Q_BLOCK + WINDOW)
        wmask = (wpos[None, :] <= pos[:, None]) & (wpos[None, :] > pos[:, None] - WINDOW) & (wpos[None, :] >= 0)
        s_w = jnp.einsum('bgrqd,bgkd->bgrqk', qrb, kw) * scale
        p_w = masked_softmax(s_w, wmask)
        o_win = jnp.einsum('bgrqk,bgkd->bgrqd', p_w.astype(vw.dtype), vw)

        return gb[..., 0:1] * o_cmp + gb[..., 1:2] * o_sel + gb[..., 2:3] * o_win

    out = lax.map(query_block, jnp.arange(s // Q_BLOCK))
    out = out.transpose(1, 0, 4, 2, 3, 5).reshape(b, s, Q_WIDTH)
    return out @ w_out


def setup_inputs(seed: int = 0) -> dict:
    key = jax.random.key(seed)
    ks = jax.random.split(key, 22)
    nrm = jax.random.normal
    D, F, NC, NN = D_MODEL, D_FF, N_CONV_LAYERS, N_NSA_LAYERS
    return {
        'x': nrm(ks[0], (BATCH, SEQ, D), jnp.float32),
        'mix_norm_pre': 1.0 + 0.05 * nrm(ks[1], (DEPTH, D), jnp.float32),
        'mix_norm_post': 1.0 + 0.05 * nrm(ks[2], (DEPTH, D), jnp.float32),
        'ffn_norm_pre': 1.0 + 0.05 * nrm(ks[3], (DEPTH, 2, D), jnp.float32),
        'ffn_norm_post': 1.0 + 0.05 * nrm(ks[4], (DEPTH, 2, D), jnp.float32),
        'ffn_w_gate': nrm(ks[5], (DEPTH, 2, D, F), jnp.float32) * D ** -0.5,
        'ffn_w_up': nrm(ks[6], (DEPTH, 2, D, F), jnp.float32) * D ** -0.5,
        'ffn_w_down': nrm(ks[7], (DEPTH, 2, F, D), jnp.float32) * F ** -0.5,
        'conv_w_pw1': nrm(ks[8], (NC, D, 2 * D), jnp.float32) * D ** -0.5,
        'conv_b_pw1': 0.02 * nrm(ks[9], (NC, 2 * D), jnp.float32),
        'conv_w_dw': nrm(ks[10], (NC, CONV_WIDTH, D), jnp.float32) * CONV_WIDTH ** -0.5,
        'conv_b_dw': 0.02 * nrm(ks[11], (NC, D), jnp.float32),
        'conv_ln_g': 1.0 + 0.05 * nrm(ks[12], (NC, D), jnp.float32),
        'conv_ln_b': 0.02 * nrm(ks[13], (NC, D), jnp.float32),
        'conv_w_pw2': nrm(ks[14], (NC, D, D), jnp.float32) * D ** -0.5,
        'conv_b_pw2': 0.02 * nrm(ks[15], (NC, D), jnp.float32),
        'nsa_w_in': nrm(ks[16], (NN, D, NSA_IN_WIDTH), jnp.float32) * D ** -0.5,
        'nsa_b_gate': 0.02 * nrm(ks[17], (NN, NSA_HEADS * N_GATES), jnp.float32),
        'nsa_cmp_pos': 0.1 * nrm(ks[18], (NN, 2, CMP_LEN, HEAD_DIM), jnp.float32),
        'nsa_cmp_w1': nrm(ks[19], (NN, 2, CMP_LEN * HEAD_DIM, CMP_HIDDEN), jnp.float32) * (CMP_LEN * HEAD_DIM) ** -0.5,
        'nsa_cmp_w2': nrm(ks[20], (NN, 2, CMP_HIDDEN, HEAD_DIM), jnp.float32) * CMP_HIDDEN ** -0.5,
        'nsa_w_out': nrm(ks[21], (NN, Q_WIDTH, D), jnp.float32) * Q_WIDTH ** -0.5,
    }


def reference(x, mix_norm_pre, mix_norm_post, ffn_norm_pre, ffn_norm_post, ffn_w_gate, ffn_w_up, ffn_w_down,
              conv_w_pw1, conv_b_pw1, conv_w_dw, conv_b_dw, conv_ln_g, conv_ln_b, conv_w_pw2, conv_b_pw2,
              nsa_w_in, nsa_b_gate, nsa_cmp_pos, nsa_cmp_w1, nsa_cmp_w2, nsa_w_out):
    for i in range(DEPTH):
        x = half_step_ffn(x, ffn_norm_pre[i, 0], ffn_norm_post[i, 0], ffn_w_gate[i, 0], ffn_w_up[i, 0], ffn_w_down[i, 0])
        h = rms_norm(x, mix_norm_pre[i])
        j = i // N_MIXERS
        if i % N_MIXERS == 0:
            h = conformer_conv(h, conv_w_pw1[j], conv_b_pw1[j], conv_w_dw[j], conv_b_dw[j],
                               conv_ln_g[j], conv_ln_b[j], conv_w_pw2[j], conv_b_pw2[j])
        else:
            h = native_sparse_attention(h, nsa_w_in[j], nsa_b_gate[j], nsa_cmp_pos[j],
                                        nsa_cmp_w1[j], nsa_cmp_w2[j], nsa_w_out[j])
        x = x + rms_norm(h, mix_norm_post[i])
        x = half_step_ffn(x, ffn_norm_pre[i, 1], ffn_norm_post[i, 1], ffn_w_gate[i, 1], ffn_w_up[i, 1], ffn_w_down[i, 1])
    return x
```

```python
import functools
import math

import jax
import jax.numpy as jnp
import numpy as np
from jax import lax
from jax.experimental import pallas as pl
from jax.experimental.pallas import tpu as pltpu

RMS_EPS = 1e-6
LN_EPS = 1e-5
FFN_RESIDUAL_WEIGHT = 0.5
CONV_WIDTH = 31
NSA_HEADS = 16
NSA_KV_GROUPS = 4
NSA_REP = NSA_HEADS // NSA_KV_GROUPS
HEAD_DIM = 64
ROT_HALF = HEAD_DIM // 8
ROPE_THETA = 500000.0
CMP_LEN = 32
CMP_STRIDE = 16
SEL_BLOCK = 64
N_SEL = 16
WINDOW = 512
Q_BLOCK = 128
N_GATES = 3
NEG_INF = -1e30

LANES = 128
V7X_VMEM_LIMIT = 56 * 1024 * 1024

BF16 = jnp.bfloat16
F32 = jnp.float32


def _params(sem):
    return pltpu.CompilerParams(dimension_semantics=sem, vmem_limit_bytes=V7X_VMEM_LIMIT)


def _const_spec(shape):
    nd = len(shape)
    return pl.BlockSpec(shape, lambda *_: (0,) * nd, pipeline_mode=pl.Buffered(1))


def _rms(x, gain):
    return x * lax.rsqrt(jnp.mean(x * x, axis=-1, keepdims=True) + RMS_EPS) * gain


def _sigmoid(x):
    return 1.0 / (1.0 + jnp.exp(-x))


def _dot(a, b):
    return jnp.dot(a, b, preferred_element_type=F32)


def _dot_nt(a, b):
    return lax.dot_general(a, b, (((1,), (1,)), ((), ())), preferred_element_type=F32)


def _dot_tn(a, b):
    return lax.dot_general(a, b, (((0,), (0,)), ((), ())), preferred_element_type=F32)


FFN_ROWS = 512
FFN_CHUNK = 512


def _ffn_body(x_ref, gpre_ref, gpost_ref, wg_ref, wu_ref, wd_ref, o_ref):
    x = x_ref[...]
    h = _rms(x, gpre_ref[...]).astype(BF16)
    d_ff = wd_ref.shape[0]
    acc = jnp.zeros(x.shape, F32)
    for c0 in range(0, d_ff, FFN_CHUNK):
        c1 = min(c0 + FFN_CHUNK, d_ff)
        g = _dot(h, wg_ref[:, c0:c1])
        u = _dot(h, wu_ref[:, c0:c1])
        a = (g * _sigmoid(g) * u).astype(BF16)
        acc = acc + _dot(a, wd_ref[c0:c1, :])
    o_ref[...] = x + FFN_RESIDUAL_WEIGHT * _rms(acc, gpost_ref[...])


def _ffn_half(x2, g_pre, g_post, w_gate, w_up, w_down):
    t, d = x2.shape
    f = w_gate.shape[1]
    tm = min(FFN_ROWS, t)
    row = pl.BlockSpec((tm, d), lambda i: (i, 0))
    return pl.pallas_call(
        _ffn_body,
        grid=(t // tm,),
        in_specs=[row, _const_spec((1, d)), _const_spec((1, d)),
                  _const_spec((d, f)), _const_spec((d, f)), _const_spec((f, d))],
        out_specs=row,
        out_shape=jax.ShapeDtypeStruct((t, d), F32),
        compiler_params=_params(("arbitrary",)),
        name="ffn_half",
    )(x2, g_pre.reshape(1, d), g_post.reshape(1, d),
      w_gate.astype(BF16), w_up.astype(BF16), w_down.astype(BF16))


CONV_ROWS = 256
CONV_HALO = 32
CONV_STRIP = 32


def _conv_body(x_ref, gpre_ref, w1_ref, b1_ref, wdw_ref, bdw_ref, lng_ref, lnb_ref,
               w2_ref, b2_ref, gpost_ref, o_ref, buf_ref, dw_ref):
    ts, d = x_ref.shape[1], x_ref.shape[2]

    @pl.when(pl.program_id(1) == 0)
    def _():
        buf_ref[0:CONV_HALO, :] = jnp.zeros((CONV_HALO, d), F32)

    x = x_ref[0]
    h = _rms(x, gpre_ref[...]).astype(BF16)
    p = _dot(h, w1_ref[...]) + b1_ref[...]
    buf_ref[CONV_HALO:CONV_HALO + ts, :] = p[:, :d] * _sigmoid(p[:, d:])

    base = CONV_HALO - (CONV_WIDTH - 1)
    for r0 in range(0, ts, CONV_STRIP):
        acc = jnp.zeros((CONV_STRIP, d), F32)
        for k in range(CONV_WIDTH):
            acc = acc + wdw_ref[k:k + 1, :] * buf_ref[r0 + base + k:r0 + base + k + CONV_STRIP, :]
        dw_ref[r0:r0 + CONV_STRIP, :] = acc
    buf_ref[0:CONV_HALO, :] = buf_ref[ts:ts + CONV_HALO, :]

    c = dw_ref[...] + bdw_ref[...]
    mu = jnp.mean(c, axis=-1, keepdims=True)
    cc = c - mu
    var = jnp.mean(cc * cc, axis=-1, keepdims=True)
    y = cc * lax.rsqrt(var + LN_EPS) * lng_ref[...] + lnb_ref[...]
    y = (y * _sigmoid(y)).astype(BF16)
    out = _dot(y, w2_ref[...]) + b2_ref[...]
    o_ref[0] = x + _rms(out, gpost_ref[...])


def _conv_mixer(x, g_pre, g_post, w_pw1, b_pw1, w_dw, b_dw, ln_g, ln_b, w_pw2, b_pw2):
    b, s, d = x.shape
    ts = min(CONV_ROWS, s)
    tile = pl.BlockSpec((1, ts, d), lambda bi, j: (bi, j, 0))
    vec = lambda n: _const_spec((1, n))
    return pl.pallas_call(
        _conv_body,
        grid=(b, s // ts),
        in_specs=[tile, vec(d), _const_spec((d, 2 * d)), vec(2 * d), _const_spec((CONV_WIDTH, d)),
                  vec(d), vec(d), vec(d), _const_spec((d, d)), vec(d), vec(d)],
        out_specs=tile,
        out_shape=jax.ShapeDtypeStruct((b, s, d), F32),
        scratch_shapes=[pltpu.VMEM((ts + CONV_HALO, d), F32), pltpu.VMEM((ts, d), F32)],
        compiler_params=_params(("arbitrary", "arbitrary")),
        name="conv_mixer",
    )(x, g_pre.reshape(1, d), w_pw1.astype(BF16), b_pw1.reshape(1, 2 * d), w_dw,
      b_dw.reshape(1, d), ln_g.reshape(1, d), ln_b.reshape(1, d), w_pw2.astype(BF16),
      b_pw2.reshape(1, d), g_post.reshape(1, d))


PROJ_ROWS = 512
Q_WIDTH = NSA_HEADS * HEAD_DIM
KV_WIDTH = NSA_KV_GROUPS * HEAD_DIM
KPAD_WIDTH = NSA_KV_GROUPS * LANES
GATE_PAD = LANES
Q_SCALE = HEAD_DIM ** -0.5 * math.log2(math.e)


def _rope_tables(s):
    pos = jnp.arange(s, dtype=F32)
    inv_freq = ROPE_THETA ** (-jnp.arange(0, 2 * ROT_HALF, 2, dtype=F32) / (2 * ROT_HALF))
    ang = pos[:, None] * inv_freq[None, :]
    return jnp.cos(ang), jnp.sin(ang)


def _proj_body(x_ref, gpre_ref, wtok_ref, wtr_ref, bg_ref, cosT_ref, sinT_ref, ck_ref, s1_ref, s2_ref,
               qT_ref, qrT_ref, ksel_ref, kwin_ref, vselT_ref, vwinT_ref, kc_ref, vc_ref, gate_ref):
    tm = x_ref.shape[1]
    h = _rms(x_ref[0], gpre_ref[...]).astype(BF16)
    tok = _dot(h, wtok_ref[...])
    tr = _dot_nt(wtr_ref[...], h)

    q = tr[0:Q_WIDTH] * Q_SCALE
    qT_ref[0] = q.astype(BF16)
    q3 = q.reshape(NSA_HEADS, HEAD_DIM, tm)
    cos, sin = cosT_ref[...], sinT_ref[...]
    x1, x2 = q3[:, 0:ROT_HALF], q3[:, ROT_HALF:2 * ROT_HALF]
    qr = jnp.concatenate([x1 * cos - x2 * sin, x2 * cos + x1 * sin, q3[:, 2 * ROT_HALF:]], axis=1)
    qrT_ref[0] = qr.reshape(Q_WIDTH, tm).astype(BF16)
    vselT_ref[0] = tr[Q_WIDTH:Q_WIDTH + KV_WIDTH].astype(BF16)
    vwinT_ref[0] = tr[Q_WIDTH + KV_WIDTH:Q_WIDTH + 2 * KV_WIDTH].astype(BF16)

    ck, s1, s2 = ck_ref[...], s1_ref[...], s2_ref[...]
    for out_ref, base in ((ksel_ref, 0), (kwin_ref, KPAD_WIDTH)):
        for g in range(NSA_KV_GROUPS):
            k = tok[:, base + g * LANES:base + (g + 1) * LANES]
            kr = k * ck + pltpu.roll(k, ROT_HALF, 1) * s1 + pltpu.roll(k, LANES - ROT_HALF, 1) * s2
            out_ref[0, :, g * LANES:(g + 1) * LANES] = kr.astype(BF16)
    c0 = 2 * KPAD_WIDTH
    kc_ref[0] = tok[:, c0:c0 + KV_WIDTH]
    vc_ref[0] = tok[:, c0 + KV_WIDTH:c0 + 2 * KV_WIDTH]
    gate_ref[0] = _sigmoid(tok[:, c0 + 2 * KV_WIDTH:] + bg_ref[...])


def _nsa_project(x, g_pre, w_in, b_gate):
    b, s, d = x.shape
    tm = min(PROJ_ROWS, s)
    n_gate = NSA_HEADS * N_GATES
    w_q = w_in[:, :Q_WIDTH]
    w_kv = w_in[:, Q_WIDTH:Q_WIDTH + 6 * KV_WIDTH].reshape(d, 6, NSA_KV_GROUPS, HEAD_DIM)
    w_gate = w_in[:, Q_WIDTH + 6 * KV_WIDTH:]

    def padded(w):
        return jnp.pad(w, ((0, 0), (0, 0), (0, LANES - HEAD_DIM))).reshape(d, KPAD_WIDTH)

    w_tok = jnp.concatenate(
        [padded(w_kv[:, 2]), padded(w_kv[:, 4]), w_kv[:, 0].reshape(d, KV_WIDTH),
         w_kv[:, 1].reshape(d, KV_WIDTH), jnp.pad(w_gate, ((0, 0), (0, GATE_PAD - n_gate)))],
        axis=1).astype(BF16)
    w_tr = jnp.concatenate(
        [w_q, w_kv[:, 3].reshape(d, KV_WIDTH), w_kv[:, 5].reshape(d, KV_WIDTH)], axis=1).T.astype(BF16)
    bg = jnp.pad(b_gate, (0, GATE_PAD - n_gate)).reshape(1, GATE_PAD)

    cos, sin = _rope_tables(s)
    zeros = jnp.zeros_like(sin)
    pad_to = lambda parts, fill: jnp.concatenate(
        parts + [jnp.full((s, LANES - 2 * ROT_HALF), fill, F32)], axis=1)
    ck = pad_to([cos, cos], 1.0)
    s1 = pad_to([zeros, sin], 0.0)
    s2 = pad_to([-sin, zeros], 0.0)

    ntok, ntr = w_tok.shape[1], w_tr.shape[0]
    tok_major = lambda w: pl.BlockSpec((1, tm, w), lambda bi, j: (bi, j, 0))
    tr_major = lambda r: pl.BlockSpec((1, r, tm), lambda bi, j: (bi, 0, j))
    return pl.pallas_call(
        _proj_body,
        grid=(b, s // tm),
        in_specs=[tok_major(d), _const_spec((1, d)), _const_spec((d, ntok)), _const_spec((ntr, d)),
                  _const_spec((1, GATE_PAD)),
                  pl.BlockSpec((ROT_HALF, tm), lambda bi, j: (0, j)),
                  pl.BlockSpec((ROT_HALF, tm), lambda bi, j: (0, j)),
                  pl.BlockSpec((tm, LANES), lambda bi, j: (j, 0)),
                  pl.BlockSpec((tm, LANES), lambda bi, j: (j, 0)),
                  pl.BlockSpec((tm, LANES), lambda bi, j: (j, 0))],
        out_specs=[tr_major(Q_WIDTH), tr_major(Q_WIDTH), tok_major(KPAD_WIDTH), tok_major(KPAD_WIDTH),
                   tr_major(KV_WIDTH), tr_major(KV_WIDTH), tok_major(KV_WIDTH), tok_major(KV_WIDTH),
                   tok_major(GATE_PAD)],
        out_shape=[jax.ShapeDtypeStruct((b, Q_WIDTH, s), BF16), jax.ShapeDtypeStruct((b, Q_WIDTH, s), BF16),
                   jax.ShapeDtypeStruct((b, s, KPAD_WIDTH), BF16), jax.ShapeDtypeStruct((b, s, KPAD_WIDTH), BF16),
                   jax.ShapeDtypeStruct((b, KV_WIDTH, s), BF16), jax.ShapeDtypeStruct((b, KV_WIDTH, s), BF16),
                   jax.ShapeDtypeStruct((b, s, KV_WIDTH), F32), jax.ShapeDtypeStruct((b, s, KV_WIDTH), F32),
                   jax.ShapeDtypeStruct((b, s, GATE_PAD), F32)],
        compiler_params=_params(("arbitrary", "arbitrary")),
        name="nsa_project",
    )(x, g_pre.reshape(1, d), w_tok, w_tr, bg, cos.T, sin.T, ck, s1, s2)


def _compress_body(rk_ref, rv_ref, pos_ref, w1_ref, w2k_ref, w2vT_ref, kc_ref, vcT_ref):
    nrow = rk_ref.shape[2]
    half = rk_ref.shape[3]

    def hidden(r, t):
        top = _dot((r + pos_ref[t, 0:1, :]).astype(BF16), w1_ref[t, 0:half, :])
        bot = _dot((r + pos_ref[t, 1:2, :]).astype(BF16), w1_ref[t, half:2 * half, :])
        hid = top + pltpu.roll(bot, nrow - 1, 0)
        return (hid * _sigmoid(hid)).astype(BF16)

    kc_ref[0, 0] = _dot(hidden(rk_ref[0, 0], 0), w2k_ref[...]).astype(BF16)
    vcT_ref[0, 0] = _dot_nt(w2vT_ref[...], hidden(rv_ref[0, 0], 1)).astype(BF16)


def _nsa_compress(kc_raw, vc_raw, cmp_pos, cmp_w1, cmp_w2):
    b, s, _ = kc_raw.shape
    nrow = s // CMP_STRIDE
    half = CMP_STRIDE * HEAD_DIM
    hid = cmp_w1.shape[-1]

    def rows(raw):
        r = raw.reshape(b, nrow, CMP_STRIDE, NSA_KV_GROUPS, HEAD_DIM)
        return r.transpose(0, 3, 1, 2, 4).reshape(b, NSA_KV_GROUPS, nrow, half)

    pos = cmp_pos.reshape(2, 2, half)
    w2k = jnp.pad(cmp_w2[0], ((0, 0), (0, LANES - HEAD_DIM))).astype(BF16)
    w2vT = cmp_w2[1].T.astype(BF16)
    blk = pl.BlockSpec((1, 1, nrow, half), lambda bi, g: (bi, g, 0, 0))
    return pl.pallas_call(
        _compress_body,
        grid=(b, NSA_KV_GROUPS),
        in_specs=[blk, blk, _const_spec((2, 2, half)), _const_spec((2, 2 * half, hid)),
                  _const_spec((hid, LANES)), _const_spec((HEAD_DIM, hid))],
        out_specs=[pl.BlockSpec((1, 1, nrow, LANES), lambda bi, g: (bi, g, 0, 0)),
                   pl.BlockSpec((1, 1, HEAD_DIM, nrow), lambda bi, g: (bi, g, 0, 0))],
        out_shape=[jax.ShapeDtypeStruct((b, NSA_KV_GROUPS, nrow, LANES), BF16),
                   jax.ShapeDtypeStruct((b, NSA_KV_GROUPS, HEAD_DIM, nrow), BF16)],
        compiler_params=_params(("arbitrary", "arbitrary")),
        name="nsa_compress",
    )(rows(kc_raw), rows(vc_raw), pos, cmp_w1.astype(BF16), w2k, w2vT)


SEL_CHUNK = 512
SEL_CHUNK_BLOCKS = SEL_CHUNK // SEL_BLOCK
WIN_CHUNK = Q_BLOCK
QLANES = NSA_REP * Q_BLOCK


def _flash_step(s, vT, m, l, acc):
    m_new = jnp.maximum(m, jnp.max(s, axis=0, keepdims=True))
    alpha = jnp.exp2(m - m_new)
    p = jnp.exp2(s - m_new)
    l = alpha * l + jnp.sum(p, axis=0, keepdims=True)
    acc = alpha * acc + _dot(vT, p.astype(BF16))
    return m_new, l, acc


def _attn_body(qT_ref, qrT_ref, kc_ref, vcT_ref, ksel_ref, vselT_ref, kwin_ref, vwinT_ref, gate_ref,
               ovT_ref, o_ref, bias_ref):
    i = pl.program_id(2)
    t0 = i * Q_BLOCK
    nblk = ovT_ref.shape[0]
    ncmp = kc_ref.shape[2]

    def lanes_of_heads(ref):
        return jnp.concatenate([ref[0, r * HEAD_DIM:(r + 1) * HEAD_DIM, :] for r in range(NSA_REP)], axis=1)

    qT = lanes_of_heads(qT_ref)
    qrT = lanes_of_heads(qrT_ref)
    pos1 = t0 + lax.broadcasted_iota(jnp.int32, (1, Q_BLOCK), 1)
    pos = jnp.concatenate([pos1] * NSA_REP, axis=1)

    s = _dot(kc_ref[0, 0][:, :HEAD_DIM], qT)
    cend = lax.broadcasted_iota(jnp.int32, (ncmp, QLANES), 0) * CMP_STRIDE + (CMP_LEN - 1)
    valid = cend <= pos
    s = jnp.where(valid, s, NEG_INF)
    p = jnp.exp2(s - jnp.max(s, axis=0, keepdims=True)) * valid.astype(F32)
    p = p / jnp.maximum(jnp.sum(p, axis=0, keepdims=True), 1e-30)
    o_cmp = _dot(vcT_ref[0, 0], p.astype(BF16))

    ph = p[:, 0:Q_BLOCK]
    for r in range(1, NSA_REP):
        ph = ph + p[:, r * Q_BLOCK:(r + 1) * Q_BLOCK]
    hi = ph.astype(BF16)
    rem = ph - hi.astype(F32)
    mid = rem.astype(BF16)
    lo = (rem - mid.astype(F32)).astype(BF16)
    ovT = ovT_ref[...]
    imp = _dot(ovT, hi) + _dot(ovT, mid) + _dot(ovT, lo)

    blk = lax.broadcasted_iota(jnp.int32, (nblk, Q_BLOCK), 0)
    cur = pos1 // SEL_BLOCK
    forced = (blk == 0) | (blk == cur) | (blk == cur - 1)
    causal = blk * SEL_BLOCK <= pos1
    val = jnp.where(forced, 1e9, jnp.where(causal, imp, -1e9))
    chosen = jnp.zeros((nblk, Q_BLOCK), jnp.bool_)
    for _ in range(min(N_SEL, nblk)):
        top = jnp.max(val, axis=0, keepdims=True)
        first = jnp.min(jnp.where(val == top, blk, nblk), axis=0, keepdims=True)
        hit = blk == first
        chosen = chosen | hit
        val = jnp.where(hit, -jnp.inf, val)
    bias1 = jnp.where(chosen, 0.0, NEG_INF)
    bias_ref[...] = jnp.concatenate([bias1] * NSA_REP, axis=1)

    m0 = jnp.full((1, QLANES), NEG_INF, F32)
    l0 = jnp.zeros((1, QLANES), F32)
    a0 = jnp.zeros((HEAD_DIM, QLANES), F32)

    def sel_step(j, carry):
        k0 = pl.multiple_of(j * SEL_CHUNK, SEL_CHUNK)
        s = _dot(ksel_ref[0, pl.ds(k0, SEL_CHUNK), :][:, :HEAD_DIM], qrT)
        b8 = bias_ref[pl.ds(pl.multiple_of(j * SEL_CHUNK_BLOCKS, SEL_CHUNK_BLOCKS), SEL_CHUNK_BLOCKS), :]
        bias = jnp.concatenate(
            [jnp.broadcast_to(b8[n:n + 1, :], (SEL_BLOCK, QLANES)) for n in range(SEL_CHUNK_BLOCKS)], axis=0)
        kpos = k0 + lax.broadcasted_iota(jnp.int32, (SEL_CHUNK, QLANES), 0)
        s = jnp.where(kpos <= pos, s + bias, NEG_INF)
        return _flash_step(s, vselT_ref[0, :, pl.ds(k0, SEL_CHUNK)], *carry)

    n_sel_chunks = (t0 + Q_BLOCK + SEL_CHUNK - 1) // SEL_CHUNK
    _, l_s, a_s = lax.fori_loop(0, n_sel_chunks, sel_step, (m0, l0, a0))

    def win_step(w, carry):
        k0 = pl.multiple_of(t0 - WINDOW + w * WIN_CHUNK, WIN_CHUNK)
        s = _dot(kwin_ref[0, pl.ds(k0, WIN_CHUNK), :][:, :HEAD_DIM], qrT)
        kpos = k0 + lax.broadcasted_iota(jnp.int32, (WIN_CHUNK, QLANES), 0)
        s = jnp.where((kpos <= pos) & (kpos > pos - WINDOW), s, NEG_INF)
        return _flash_step(s, vwinT_ref[0, :, pl.ds(k0, WIN_CHUNK)], *carry)

    n_win_chunks = WINDOW // WIN_CHUNK + 1
    first_win = jnp.maximum(0, WINDOW // WIN_CHUNK - i)
    _, l_w, a_w = lax.fori_loop(first_win, n_win_chunks, win_step, (m0, l0, a0))

    gate = gate_ref[0, 0, 0]
    out = gate[0:1] * o_cmp + gate[1:2] * (a_s / l_s) + gate[2:3] * (a_w / l_w)
    for r in range(NSA_REP):
        o_ref[0, r * HEAD_DIM:(r + 1) * HEAD_DIM, :] = out[:, r * Q_BLOCK:(r + 1) * Q_BLOCK].astype(BF16)


def _nsa_attention(qT, qrT, kcmp, vcmpT, ksel, vselT, kwin, vwinT, gates):
    b, _, s = qT.shape
    nq = s // Q_BLOCK
    nblk = s // SEL_BLOCK
    ncmp = kcmp.shape[2]
    rows = NSA_REP * HEAD_DIM

    gt = gates[:, :, :NSA_HEADS * N_GATES].reshape(b, nq, Q_BLOCK, NSA_KV_GROUPS, NSA_REP, N_GATES)
    gt = gt.transpose(0, 3, 1, 5, 4, 2).reshape(b, NSA_KV_GROUPS, nq, N_GATES, QLANES)

    cstart = np.arange(ncmp) * CMP_STRIDE
    sstart = np.arange(nblk) * SEL_BLOCK
    ovT = ((cstart[None, :] < sstart[:, None] + SEL_BLOCK) & (cstart[None, :] + CMP_LEN > sstart[:, None]))
    ovT = jnp.asarray(ovT, BF16)

    q_spec = pl.BlockSpec((1, rows, Q_BLOCK), lambda bi, g, i: (bi, g, i))
    k_spec = pl.BlockSpec((1, s, LANES), lambda bi, g, i: (bi, 0, g))
    vT_spec = pl.BlockSpec((1, HEAD_DIM, s), lambda bi, g, i: (bi, g, 0))
    return pl.pallas_call(
        _attn_body,
        grid=(b, NSA_KV_GROUPS, nq),
        in_specs=[q_spec, q_spec,
                  pl.BlockSpec((1, 1, ncmp, LANES), lambda bi, g, i: (bi, g, 0, 0)),
                  pl.BlockSpec((1, 1, HEAD_DIM, ncmp), lambda bi, g, i: (bi, g, 0, 0)),
                  k_spec, vT_spec, k_spec, vT_spec,
                  pl.BlockSpec((1, 1, 1, N_GATES, QLANES), lambda bi, g, i: (bi, g, i, 0, 0)),
                  _const_spec((nblk, ncmp))],
        out_specs=q_spec,
        out_shape=jax.ShapeDtypeStruct((b, NSA_HEADS * HEAD_DIM, s), BF16),
        scratch_shapes=[pltpu.VMEM((nblk, QLANES), F32)],
        compiler_params=_params(("arbitrary", "arbitrary", "arbitrary")),
        name="nsa_attention",
    )(qT, qrT, kcmp, vcmpT, ksel, vselT, kwin, vwinT, gt, ovT)


OUT_ROWS = 512


def _out_body(x_ref, aT_ref, w_ref, gpost_ref, o_ref):
    h = _dot_tn(aT_ref[0], w_ref[...])
    o_ref[0] = x_ref[0] + _rms(h, gpost_ref[...])


def _nsa_output(x, attnT, w_out, g_post):
    b, s, d = x.shape
    tm = min(OUT_ROWS, s)
    tile = pl.BlockSpec((1, tm, d), lambda bi, j: (bi, j, 0))
    return pl.pallas_call(
        _out_body,
        grid=(b, s // tm),
        in_specs=[tile, pl.BlockSpec((1, attnT.shape[1], tm), lambda bi, j: (bi, 0, j)),
                  _const_spec(w_out.shape), _const_spec((1, d))],
        out_specs=tile,
        out_shape=jax.ShapeDtypeStruct((b, s, d), F32),
        compiler_params=_params(("arbitrary", "arbitrary")),
        name="nsa_output",
    )(x, attnT, w_out.astype(BF16), g_post.reshape(1, d))


def _nsa_mixer(x, g_pre, g_post, w_in, b_gate, cmp_pos, cmp_w1, cmp_w2, w_out):
    qT, qrT, ksel, kwin, vselT, vwinT, kc_raw, vc_raw, gates = _nsa_project(x, g_pre, w_in, b_gate)
    kcmp, vcmpT = _nsa_compress(kc_raw, vc_raw, cmp_pos, cmp_w1, cmp_w2)
    attnT = _nsa_attention(qT, qrT, kcmp, vcmpT, ksel, vselT, kwin, vwinT, gates)
    return _nsa_output(x, attnT, w_out, g_post)


def kernel(x, mix_norm_pre, mix_norm_post, ffn_norm_pre, ffn_norm_post, ffn_w_gate, ffn_w_up, ffn_w_down,
           conv_w_pw1, conv_b_pw1, conv_w_dw, conv_b_dw, conv_ln_g, conv_ln_b, conv_w_pw2, conv_b_pw2,
           nsa_w_in, nsa_b_gate, nsa_cmp_pos, nsa_cmp_w1, nsa_cmp_w2, nsa_w_out):
    b, s, d = x.shape
    depth = mix_norm_pre.shape[0]
    n_mixers = 2

    def ffn(x, i, half):
        y = _ffn_half(x.reshape(b * s, d), ffn_norm_pre[i, half], ffn_norm_post[i, half],
                      ffn_w_gate[i, half], ffn_w_up[i, half], ffn_w_down[i, half])
        return y.reshape(b, s, d)

    for i in range(depth):
        x = ffn(x, i, 0)
        j = i // n_mixers
        if i % n_mixers == 0:
            x = _conv_mixer(x, mix_norm_pre[i], mix_norm_post[i], conv_w_pw1[j], conv_b_pw1[j], conv_w_dw[j],
                            conv_b_dw[j], conv_ln_g[j], conv_ln_b[j], conv_w_pw2[j], conv_b_pw2[j])
        else:
            x = _nsa_mixer(x, mix_norm_pre[i], mix_norm_post[i], nsa_w_in[j], nsa_b_gate[j], nsa_cmp_pos[j],
                           nsa_cmp_w1[j], nsa_cmp_w2[j], nsa_w_out[j])
        x = ffn(x, i, 1)
    return x
```

```python
import functools
import math

import jax
import jax.numpy as jnp
import numpy as np
from jax import lax
from jax.experimental import pallas as pl
from jax.experimental.pallas import tpu as pltpu

RMS_EPS = 1e-6
LN_EPS = 1e-5
FFN_RESIDUAL_WEIGHT = 0.5
CONV_WIDTH = 31
NSA_HEADS = 16
NSA_KV_GROUPS = 4
NSA_REP = NSA_HEADS // NSA_KV_GROUPS
HEAD_DIM = 64
ROT_HALF = HEAD_DIM // 8
ROPE_THETA = 500000.0
CMP_LEN = 32
CMP_STRIDE = 16
SEL_BLOCK = 64
N_SEL = 16
WINDOW = 512
Q_BLOCK = 128
N_GATES = 3
NEG_INF = -1e30

LANES = 128
V7X_VMEM_LIMIT = 56 * 1024 * 1024

BF16 = jnp.bfloat16
F32 = jnp.float32


def _params(sem):
    return pltpu.CompilerParams(dimension_semantics=sem, vmem_limit_bytes=V7X_VMEM_LIMIT)


def _const_spec(shape):
    nd = len(shape)
    return pl.BlockSpec(shape, lambda *_: (0,) * nd, pipeline_mode=pl.Buffered(1))


def _rms(x, gain):
    return x * lax.rsqrt(jnp.mean(x * x, axis=-1, keepdims=True) + RMS_EPS) * gain


def _sigmoid(x):
    return 1.0 / (1.0 + jnp.exp(-x))


def _dot(a, b):
    return jnp.dot(a, b, preferred_element_type=F32)


def _dot_nt(a, b):
    return lax.dot_general(a, b, (((1,), (1,)), ((), ())), preferred_element_type=F32)


def _dot_tn(a, b):
    return lax.dot_general(a, b, (((0,), (0,)), ((), ())), preferred_element_type=F32)


FFN_ROWS = 512
FFN_CHUNK = 512


def _ffn_body(x_ref, gpre_ref, gpost_ref, wg_ref, wu_ref, wd_ref, o_ref):
    x = x_ref[...]
    h = _rms(x, gpre_ref[...]).astype(BF16)
    d_ff = wd_ref.shape[0]
    acc = jnp.zeros(x.shape, F32)
    for c0 in range(0, d_ff, FFN_CHUNK):
        c1 = min(c0 + FFN_CHUNK, d_ff)
        g = _dot(h, wg_ref[:, c0:c1])
        u = _dot(h, wu_ref[:, c0:c1])
        a = (g * _sigmoid(g) * u).astype(BF16)
        acc = acc + _dot(a, wd_ref[c0:c1, :])
    o_ref[...] = x + FFN_RESIDUAL_WEIGHT * _rms(acc, gpost_ref[...])


def _ffn_half(x2, g_pre, g_post, w_gate, w_up, w_down):
    t, d = x2.shape
    f = w_gate.shape[1]
    tm = min(FFN_ROWS, t)
    row = pl.BlockSpec((tm, d), lambda i: (i, 0))
    return pl.pallas_call(
        _ffn_body,
        grid=(t // tm,),
        in_specs=[row, _const_spec((1, d)), _const_spec((1, d)),
                  _const_spec((d, f)), _const_spec((d, f)), _const_spec((f, d))],
        out_specs=row,
        out_shape=jax.ShapeDtypeStruct((t, d), F32),
        compiler_params=_params(("arbitrary",)),
        name="ffn_half",
    )(x2, g_pre.reshape(1, d), g_post.reshape(1, d),
      w_gate.astype(BF16), w_up.astype(BF16), w_down.astype(BF16))


CONV_ROWS = 256
CONV_HALO = 32
CONV_STRIP = 32


def _conv_body(x_ref, gpre_ref, w1_ref, b1_ref, wdw_ref, bdw_ref, lng_ref, lnb_ref,
               w2_ref, b2_ref, gpost_ref, o_ref, buf_ref, dw_ref):
    ts, d = x_ref.shape[1], x_ref.shape[2]

    @pl.when(pl.program_id(1) == 0)
    def _():
        buf_ref[0:CONV_HALO, :] = jnp.zeros((CONV_HALO, d), F32)

    x = x_ref[0]
    h = _rms(x, gpre_ref[...]).astype(BF16)
    p = _dot(h, w1_ref[...]) + b1_ref[...]
    buf_ref[CONV_HALO:CONV_HALO + ts, :] = p[:, :d] * _sigmoid(p[:, d:])

    base = CONV_HALO - (CONV_WIDTH - 1)
    for r0 in range(0, ts, CONV_STRIP):
        acc = jnp.zeros((CONV_STRIP, d), F32)
        for k in range(CONV_WIDTH):
            acc = acc + wdw_ref[k:k + 1, :] * buf_ref[r0 + base + k:r0 + base + k + CONV_STRIP, :]
        dw_ref[r0:r0 + CONV_STRIP, :] = acc
    buf_ref[0:CONV_HALO, :] = buf_ref[ts:ts + CONV_HALO, :]

    c = dw_ref[...] + bdw_ref[...]
    mu = jnp.mean(c, axis=-1, keepdims=True)
    cc = c - mu
    var = jnp.mean(cc * cc, axis=-1, keepdims=True)
    y = cc * lax.rsqrt(var + LN_EPS) * lng_ref[...] + lnb_ref[...]
    y = (y * _sigmoid(y)).astype(BF16)
    out = _dot(y, w2_ref[...]) + b2_ref[...]
    o_ref[0] = x + _rms(out, gpost_ref[...])


def _conv_mixer(x, g_pre, g_post, w_pw1, b_pw1, w_dw, b_dw, ln_g, ln_b, w_pw2, b_pw2):
    b, s, d = x.shape
    ts = min(CONV_ROWS, s)
    tile = pl.BlockSpec((1, ts, d), lambda bi, j: (bi, j, 0))
    vec = lambda n: _const_spec((1, n))
    return pl.pallas_call(
        _conv_body,
        grid=(b, s // ts),
        in_specs=[tile, vec(d), _const_spec((d, 2 * d)), vec(2 * d), _const_spec((CONV_WIDTH, d)),
                  vec(d), vec(d), vec(d), _const_spec((d, d)), vec(d), vec(d)],
        out_specs=tile,
        out_shape=jax.ShapeDtypeStruct((b, s, d), F32),
        scratch_shapes=[pltpu.VMEM((ts + CONV_HALO, d), F32), pltpu.VMEM((ts, d), F32)],
        compiler_params=_params(("arbitrary", "arbitrary")),
        name="conv_mixer",
    )(x, g_pre.reshape(1, d), w_pw1.astype(BF16), b_pw1.reshape(1, 2 * d), w_dw,
      b_dw.reshape(1, d), ln_g.reshape(1, d), ln_b.reshape(1, d), w_pw2.astype(BF16),
      b_pw2.reshape(1, d), g_post.reshape(1, d))


PROJ_ROWS = 512
Q_WIDTH = NSA_HEADS * HEAD_DIM
KV_WIDTH = NSA_KV_GROUPS * HEAD_DIM
KPAD_WIDTH = NSA_KV_GROUPS * LANES
GATE_PAD = LANES
Q_SCALE = HEAD_DIM ** -0.5 * math.log2(math.e)


def _rope_tables(s):
    pos = jnp.arange(s, dtype=F32)
    inv_freq = ROPE_THETA ** (-jnp.arange(0, 2 * ROT_HALF, 2, dtype=F32) / (2 * ROT_HALF))
    ang = pos[:, None] * inv_freq[None, :]
    return jnp.cos(ang), jnp.sin(ang)


def _proj_body(x_ref, gpre_ref, wtok_ref, wtr_ref, bg_ref, cosT_ref, sinT_ref, ck_ref, s1_ref, s2_ref,
               qT_ref, qrT_ref, ksel_ref, kwin_ref, vselT_ref, vwinT_ref, kc_ref, vc_ref, gate_ref):
    tm = x_ref.shape[1]
    h = _rms(x_ref[0], gpre_ref[...]).astype(BF16)
    tok = _dot(h, wtok_ref[...])
    tr = _dot_nt(wtr_ref[...], h)

    q = tr[0:Q_WIDTH] * Q_SCALE
    qT_ref[0] = q.astype(BF16)
    q3 = q.reshape(NSA_HEADS, HEAD_DIM, tm)
    cos, sin = cosT_ref[...], sinT_ref[...]
    x1, x2 = q3[:, 0:ROT_HALF], q3[:, ROT_HALF:2 * ROT_HALF]
    qr = jnp.concatenate([x1 * cos - x2 * sin, x2 * cos + x1 * sin, q3[:, 2 * ROT_HALF:]], axis=1)
    qrT_ref[0] = qr.reshape(Q_WIDTH, tm).astype(BF16)
    vselT_ref[0] = tr[Q_WIDTH:Q_WIDTH + KV_WIDTH].astype(BF16)
    vwinT_ref[0] = tr[Q_WIDTH + KV_WIDTH:Q_WIDTH + 2 * KV_WIDTH].astype(BF16)

    ck, s1, s2 = ck_ref[...], s1_ref[...], s2_ref[...]
    for out_ref, base in ((ksel_ref, 0), (kwin_ref, KPAD_WIDTH)):
        for g in range(NSA_KV_GROUPS):
            k = tok[:, base + g * LANES:base + (g + 1) * LANES]
            kr = k * ck + pltpu.roll(k, ROT_HALF, 1) * s1 + pltpu.roll(k, LANES - ROT_HALF, 1) * s2
            out_ref[0, :, g * LANES:(g + 1) * LANES] = kr.astype(BF16)
    c0 = 2 * KPAD_WIDTH
    kc_ref[0] = tok[:, c0:c0 + KV_WIDTH]
    vc_ref[0] = tok[:, c0 + KV_WIDTH:c0 + 2 * KV_WIDTH]
    gate_ref[0] = _sigmoid(tok[:, c0 + 2 * KV_WIDTH:] + bg_ref[...])


def _nsa_project(x, g_pre, w_in, b_gate):
    b, s, d = x.shape
    tm = min(PROJ_ROWS, s)
    n_gate = NSA_HEADS * N_GATES
    w_q = w_in[:, :Q_WIDTH]
    w_kv = w_in[:, Q_WIDTH:Q_WIDTH + 6 * KV_WIDTH].reshape(d, 6, NSA_KV_GROUPS, HEAD_DIM)
    w_gate = w_in[:, Q_WIDTH + 6 * KV_WIDTH:]

    def padded(w):
        return jnp.pad(w, ((0, 0), (0, 0), (0, LANES - HEAD_DIM))).reshape(d, KPAD_WIDTH)

    w_tok = jnp.concatenate(
        [padded(w_kv[:, 2]), padded(w_kv[:, 4]), w_kv[:, 0].reshape(d, KV_WIDTH),
         w_kv[:, 1].reshape(d, KV_WIDTH), jnp.pad(w_gate, ((0, 0), (0, GATE_PAD - n_gate)))],
        axis=1).astype(BF16)
    w_tr = jnp.concatenate(
        [w_q, w_kv[:, 3].reshape(d, KV_WIDTH), w_kv[:, 5].reshape(d, KV_WIDTH)], axis=1).T.astype(BF16)
    bg = jnp.pad(b_gate, (0, GATE_PAD - n_gate)).reshape(1, GATE_PAD)

    cos, sin = _rope_tables(s)
    zeros = jnp.zeros_like(sin)
    pad_to = lambda parts, fill: jnp.concatenate(
        parts + [jnp.full((s, LANES - 2 * ROT_HALF), fill, F32)], axis=1)
    ck = pad_to([cos, cos], 1.0)
    s1 = pad_to([zeros, sin], 0.0)
    s2 = pad_to([-sin, zeros], 0.0)

    ntok, ntr = w_tok.shape[1], w_tr.shape[0]
    tok_major = lambda w: pl.BlockSpec((1, tm, w), lambda bi, j: (bi, j, 0))
    tr_major = lambda r: pl.BlockSpec((1, r, tm), lambda bi, j: (bi, 0, j))
    return pl.pallas_call(
        _proj_body,
        grid=(b, s // tm),
        in_specs=[tok_major(d), _const_spec((1, d)), _const_spec((d, ntok)), _const_spec((ntr, d)),
                  _const_spec((1, GATE_PAD)),
                  pl.BlockSpec((ROT_HALF, tm), lambda bi, j: (0, j)),
                  pl.BlockSpec((ROT_HALF, tm), lambda bi, j: (0, j)),
                  pl.BlockSpec((tm, LANES), lambda bi, j: (j, 0)),
                  pl.BlockSpec((tm, LANES), lambda bi, j: (j, 0)),
                  pl.BlockSpec((tm, LANES), lambda bi, j: (j, 0))],
        out_specs=[tr_major(Q_WIDTH), tr_major(Q_WIDTH), tok_major(KPAD_WIDTH), tok_major(KPAD_WIDTH),
                   tr_major(KV_WIDTH), tr_major(KV_WIDTH), tok_major(KV_WIDTH), tok_major(KV_WIDTH),
                   tok_major(GATE_PAD)],
        out_shape=[jax.ShapeDtypeStruct((b, Q_WIDTH, s), BF16), jax.ShapeDtypeStruct((b, Q_WIDTH, s), BF16),
                   jax.ShapeDtypeStruct((b, s, KPAD_WIDTH), BF16), jax.ShapeDtypeStruct((b, s, KPAD_WIDTH), BF16),
                   jax.ShapeDtypeStruct((b, KV_WIDTH, s), BF16), jax.ShapeDtypeStruct((b, KV_WIDTH, s), BF16),
                   jax.ShapeDtypeStruct((b, s, KV_WIDTH), F32), jax.ShapeDtypeStruct((b, s, KV_WIDTH), F32),
                   jax.ShapeDtypeStruct((b, s, GATE_PAD), F32)],
        compiler_params=_params(("arbitrary", "arbitrary")),
        name="nsa_project",
    )(x, g_pre.reshape(1, d), w_tok, w_tr, bg, cos.T, sin.T, ck, s1, s2)


def _compress_body(rk_ref, rv_ref, pos_ref, w1_ref, w2k_ref, w2vT_ref, kc_ref, vcT_ref):
    nrow = rk_ref.shape[2]
    half = rk_ref.shape[3]

    def hidden(r, t):
        top = _dot((r + pos_ref[t, 0:1, :]).astype(BF16), w1_ref[t, 0:half, :])
        bot = _dot((r + pos_ref[t, 1:2, :]).astype(BF16), w1_ref[t, half:2 * half, :])
        hid = top + pltpu.roll(bot, nrow - 1, 0)
        return (hid * _sigmoid(hid)).astype(BF16)

    kc_ref[0, 0] = _dot(hidden(rk_ref[0, 0], 0), w2k_ref[...]).astype(BF16)
    vcT_ref[0, 0] = _dot_nt(w2vT_ref[...], hidden(rv_ref[0, 0], 1)).astype(BF16)


def _nsa_compress(kc_raw, vc_raw, cmp_pos, cmp_w1, cmp_w2):
    b, s, _ = kc_raw.shape
    nrow = s // CMP_STRIDE
    half = CMP_STRIDE * HEAD_DIM
    hid = cmp_w1.shape[-1]

    def rows(raw):
        r = raw.reshape(b, nrow, CMP_STRIDE, NSA_KV_GROUPS, HEAD_DIM)
        return r.transpose(0, 3, 1, 2, 4).reshape(b, NSA_KV_GROUPS, nrow, half)

    pos = cmp_pos.reshape(2, 2, half)
    w2k = jnp.pad(cmp_w2[0], ((0, 0), (0, LANES - HEAD_DIM))).astype(BF16)
    w2vT = cmp_w2[1].T.astype(BF16)
    blk = pl.BlockSpec((1, 1, nrow, half), lambda bi, g: (bi, g, 0, 0))
    return pl.pallas_call(
        _compress_body,
        grid=(b, NSA_KV_GROUPS),
        in_specs=[blk, blk, _const_spec((2, 2, half)), _const_spec((2, 2 * half, hid)),
                  _const_spec((hid, LANES)), _const_spec((HEAD_DIM, hid))],
        out_specs=[pl.BlockSpec((1, 1, nrow, LANES), lambda bi, g: (bi, g, 0, 0)),
                   pl.BlockSpec((1, 1, HEAD_DIM, nrow), lambda bi, g: (bi, g, 0, 0))],
        out_shape=[jax.ShapeDtypeStruct((b, NSA_KV_GROUPS, nrow, LANES), BF16),
                   jax.ShapeDtypeStruct((b, NSA_KV_GROUPS, HEAD_DIM, nrow), BF16)],
        compiler_params=_params(("arbitrary", "arbitrary")),
        name="nsa_compress",
    )(rows(kc_raw), rows(vc_raw), pos, cmp_w1.astype(BF16), w2k, w2vT)


SEL_CHUNK = 512
WIN_KEYS = WINDOW + Q_BLOCK
QLANES = NSA_REP * Q_BLOCK
N_FORCED = 3
ONES_ROWS = 16
MASKED_MAX_FLOOR = -1e29


def _with_ones(vT):
    return jnp.concatenate([vT, jnp.ones((ONES_ROWS, vT.shape[1]), BF16)], axis=0)


def _flash_step(s, vT, m, acc):
    m_new = jnp.maximum(m, jnp.max(s, axis=0, keepdims=True))
    p = jnp.exp2(s - m_new).astype(BF16)
    acc = jnp.exp2(m - m_new) * acc + _dot(_with_ones(vT), p)
    return m_new, acc


def _attn_body(qT_ref, qrT_ref, kc_ref, vcT_ref, ksel_ref, vselT_ref, kwin_ref, vwinT_ref, gate_ref,
               ovT_ref, oh_ref, o_ref, qaug_ref, s0_ref, s1_ref, m_ref, acc_ref, part_ref):
    i = pl.program_id(2)
    t0 = i * Q_BLOCK
    nblk = ovT_ref.shape[0]
    ncmp = kc_ref.shape[2]
    n_top = min(N_SEL, nblk) - N_FORCED

    def lanes_of_heads(ref):
        parts = [ref[0, r * HEAD_DIM:(r + 1) * HEAD_DIM, :] for r in range(NSA_REP)]
        return jnp.concatenate(
            [jnp.concatenate(parts, axis=1), jnp.zeros((LANES - HEAD_DIM, QLANES), BF16)], axis=0)

    qT = lanes_of_heads(qT_ref)
    qrT = lanes_of_heads(qrT_ref)
    pos1 = t0 + lax.broadcasted_iota(jnp.int32, (1, Q_BLOCK), 1)
    pos = jnp.concatenate([pos1] * NSA_REP, axis=1)

    s = _dot(kc_ref[0, 0], qT)
    cend = lax.broadcasted_iota(jnp.int32, (ncmp, QLANES), 0) * CMP_STRIDE + (CMP_LEN - 1)
    s = jnp.where(cend <= pos, s, NEG_INF)
    m = jnp.maximum(jnp.max(s, axis=0, keepdims=True), MASKED_MAX_FLOOR)
    p = jnp.exp2(s - m)
    inv_l = 1.0 / jnp.maximum(jnp.sum(p, axis=0, keepdims=True), 1e-30)
    o_cmp = _dot(vcT_ref[0, 0], p.astype(BF16)) * inv_l

    pn = p * inv_l
    ph = pn[:, 0:Q_BLOCK]
    for r in range(1, NSA_REP):
        ph = ph + pn[:, r * Q_BLOCK:(r + 1) * Q_BLOCK]
    hi = ph.astype(BF16)
    rem = ph - hi.astype(F32)
    mid = rem.astype(BF16)
    lo = (rem - mid.astype(F32)).astype(BF16)
    ovT = ovT_ref[...]
    imp = _dot(ovT, hi) + _dot(ovT, mid) + _dot(ovT, lo)

    blk = lax.broadcasted_iota(jnp.int32, (nblk, Q_BLOCK), 0)
    cur = pos1 // SEL_BLOCK
    forced = (blk == 0) | (blk == cur) | (blk == cur - 1)
    v0 = jnp.where(forced, -1.0, jnp.where(blk * SEL_BLOCK <= pos1, imp, -1.0))
    left = v0
    for _ in range(n_top):
        left = jnp.where(left == jnp.max(left, axis=0, keepdims=True), -2.0, left)
    taken = left != v0
    n_taken = jnp.sum(jnp.where(taken, jnp.where(v0 >= 0.0, 1.0, 0.0), 0.0), axis=0, keepdims=True)

    def store_bias(left):
        bias = jnp.where(forced, 0.0, jnp.where(left != v0, 0.0, NEG_INF)).astype(BF16)
        bias = jnp.concatenate([bias] * NSA_REP, axis=1)
        if nblk < LANES:
            bias = jnp.concatenate([bias, jnp.zeros((LANES - nblk, QLANES), BF16)], axis=0)
        qaug_ref[LANES:2 * LANES, :] = bias

    qaug_ref[0:LANES, :] = qrT
    store_bias(left)

    w0 = pl.multiple_of(jnp.maximum(t0 - WINDOW, 0), Q_BLOCK)
    s = _dot(kwin_ref[0, pl.ds(w0, WIN_KEYS), :], qrT)
    kpos = w0 + lax.broadcasted_iota(jnp.int32, (WIN_KEYS, QLANES), 0)
    s = jnp.where(kpos <= pos, jnp.where(kpos > pos - WINDOW, s, NEG_INF), NEG_INF)
    p = jnp.exp2(s - jnp.max(s, axis=0, keepdims=True)).astype(BF16)
    a_w = _dot(_with_ones(vwinT_ref[0, :, pl.ds(w0, WIN_KEYS)]), p)
    d = HEAD_DIM
    gate = gate_ref[0, 0, 0]
    part_ref[...] = gate[0:1] * o_cmp + gate[2:3] * (a_w[0:d] / a_w[d:d + 1])

    @pl.when(jnp.max(n_taken) > n_top)
    def _():
        val = v0
        for _ in range(n_top):
            top = jnp.max(val, axis=0, keepdims=True)
            first = jnp.min(jnp.where(val == top, blk, nblk), axis=0, keepdims=True)
            val = jnp.where(blk == first, -2.0, val)
        store_bias(val)

    def chunk_start(j):
        return pl.multiple_of(j * SEL_CHUNK, SEL_CHUNK)

    def scores_into(s_ref, j):
        k0 = chunk_start(j)
        ka = jnp.concatenate([ksel_ref[0, pl.ds(k0, SEL_CHUNK), :], oh_ref[pl.ds(k0, SEL_CHUNK), :]], axis=1)
        s_ref[...] = _dot(ka, qaug_ref[...])

    def absorb(s_ref, j, diagonal):
        k0 = chunk_start(j)
        s = s_ref[...]
        if diagonal:
            kpos = k0 + lax.broadcasted_iota(jnp.int32, (SEL_CHUNK, QLANES), 0)
            s = jnp.where(kpos <= pos, s, NEG_INF)
        m_new, acc = _flash_step(s, vselT_ref[0, :, pl.ds(k0, SEL_CHUNK)], m_ref[...], acc_ref[...])
        m_ref[...] = m_new
        acc_ref[...] = acc

    last = t0 // SEL_CHUNK
    m_ref[...] = jnp.full(m_ref.shape, NEG_INF, F32)
    acc_ref[...] = jnp.zeros(acc_ref.shape, F32)
    scores_into(s0_ref, 0)

    def pair(t, carry):
        scores_into(s1_ref, 2 * t + 1)
        absorb(s0_ref, 2 * t, False)
        scores_into(s0_ref, 2 * t + 2)
        absorb(s1_ref, 2 * t + 1, False)
        return carry

    lax.fori_loop(0, last // 2, pair, 0)

    @pl.when(last % 2 == 1)
    def _():
        scores_into(s1_ref, last)
        absorb(s0_ref, last - 1, False)
        absorb(s1_ref, last, True)

    @pl.when(last % 2 == 0)
    def _():
        absorb(s0_ref, last, True)

    a_s = acc_ref[...]
    out = part_ref[...] + gate_ref[0, 0, 0][1:2] * (a_s[0:d] / a_s[d:d + 1])
    for r in range(NSA_REP):
        o_ref[0, r * HEAD_DIM:(r + 1) * HEAD_DIM, :] = out[:, r * Q_BLOCK:(r + 1) * Q_BLOCK].astype(BF16)


def _nsa_attention(qT, qrT, kcmp, vcmpT, ksel, vselT, kwin, vwinT, gates):
    b, _, s = qT.shape
    nq = s // Q_BLOCK
    nblk = s // SEL_BLOCK
    ncmp = kcmp.shape[2]
    rows = NSA_REP * HEAD_DIM
    assert nblk <= LANES and s % SEL_CHUNK == 0 and s >= WIN_KEYS

    gt = gates[:, :, :NSA_HEADS * N_GATES].reshape(b, nq, Q_BLOCK, NSA_KV_GROUPS, NSA_REP, N_GATES)
    gt = gt.transpose(0, 3, 1, 5, 4, 2).reshape(b, NSA_KV_GROUPS, nq, N_GATES, QLANES)

    cstart = np.arange(ncmp) * CMP_STRIDE
    sstart = np.arange(nblk) * SEL_BLOCK
    ovT = ((cstart[None, :] < sstart[:, None] + SEL_BLOCK) & (cstart[None, :] + CMP_LEN > sstart[:, None]))
    ovT = jnp.asarray(ovT, BF16)
    onehot = jnp.asarray(np.arange(s)[:, None] // SEL_BLOCK == np.arange(LANES)[None, :], BF16)

    q_spec = pl.BlockSpec((1, rows, Q_BLOCK), lambda bi, g, i: (bi, g, i))
    k_spec = pl.BlockSpec((1, s, LANES), lambda bi, g, i: (bi, 0, g))
    vT_spec = pl.BlockSpec((1, HEAD_DIM, s), lambda bi, g, i: (bi, g, 0))
    return pl.pallas_call(
        _attn_body,
        grid=(b, NSA_KV_GROUPS, nq),
        in_specs=[q_spec, q_spec,
                  pl.BlockSpec((1, 1, ncmp, LANES), lambda bi, g, i: (bi, g, 0, 0)),
                  pl.BlockSpec((1, 1, HEAD_DIM, ncmp), lambda bi, g, i: (bi, g, 0, 0)),
                  k_spec, vT_spec, k_spec, vT_spec,
                  pl.BlockSpec((1, 1, 1, N_GATES, QLANES), lambda bi, g, i: (bi, g, i, 0, 0)),
                  _const_spec((nblk, ncmp)), _const_spec((s, LANES))],
        out_specs=q_spec,
        out_shape=jax.ShapeDtypeStruct((b, NSA_HEADS * HEAD_DIM, s), BF16),
        scratch_shapes=[pltpu.VMEM((2 * LANES, QLANES), BF16),
                        pltpu.VMEM((SEL_CHUNK, QLANES), F32), pltpu.VMEM((SEL_CHUNK, QLANES), F32),
                        pltpu.VMEM((1, QLANES), F32), pltpu.VMEM((HEAD_DIM + ONES_ROWS, QLANES), F32),
                        pltpu.VMEM((HEAD_DIM, QLANES), F32)],
        compiler_params=_params(("arbitrary", "arbitrary", "arbitrary")),
        name="nsa_attention",
    )(qT, qrT, kcmp, vcmpT, ksel, vselT, kwin, vwinT, gt, ovT, onehot)


OUT_ROWS = 512


def _out_body(x_ref, aT_ref, w_ref, gpost_ref, o_ref):
    h = _dot_tn(aT_ref[0], w_ref[...])
    o_ref[0] = x_ref[0] + _rms(h, gpost_ref[...])


def _nsa_output(x, attnT, w_out, g_post):
    b, s, d = x.shape
    tm = min(OUT_ROWS, s)
    tile = pl.BlockSpec((1, tm, d), lambda bi, j: (bi, j, 0))
    return pl.pallas_call(
        _out_body,
        grid=(b, s // tm),
        in_specs=[tile, pl.BlockSpec((1, attnT.shape[1], tm), lambda bi, j: (bi, 0, j)),
                  _const_spec(w_out.shape), _const_spec((1, d))],
        out_specs=tile,
        out_shape=jax.ShapeDtypeStruct((b, s, d), F32),
        compiler_params=_params(("arbitrary", "arbitrary")),
        name="nsa_output",
    )(x, attnT, w_out.astype(BF16), g_post.reshape(1, d))


def _nsa_mixer(x, g_pre, g_post, w_in, b_gate, cmp_pos, cmp_w1, cmp_w2, w_out):
    qT, qrT, ksel, kwin, vselT, vwinT, kc_raw, vc_raw, gates = _nsa_project(x, g_pre, w_in, b_gate)
    kcmp, vcmpT = _nsa_compress(kc_raw, vc_raw, cmp_pos, cmp_w1, cmp_w2)
    attnT = _nsa_attention(qT, qrT, kcmp, vcmpT, ksel, vselT, kwin, vwinT, gates)
    return _nsa_output(x, attnT, w_out, g_post)


def kernel(x, mix_norm_pre, mix_norm_post, ffn_norm_pre, ffn_norm_post, ffn_w_gate, ffn_w_up, ffn_w_down,
           conv_w_pw1, conv_b_pw1, conv_w_dw, conv_b_dw, conv_ln_g, conv_ln_b, conv_w_pw2, conv_b_pw2,
           nsa_w_in, nsa_b_gate, nsa_cmp_pos, nsa_cmp_w1, nsa_cmp_w2, nsa_w_out):
    b, s, d = x.shape
    depth = mix_norm_pre.shape[0]
    n_mixers = 2

    def ffn(x, i, half):
        y = _ffn_half(x.reshape(b * s, d), ffn_norm_pre[i, half], ffn_norm_post[i, half],
                      ffn_w_gate[i, half], ffn_w_up[i, half], ffn_w_down[i, half])
        return y.reshape(b, s, d)

    for i in range(depth):
        x = ffn(x, i, 0)
        j = i // n_mixers
        if i % n_mixers == 0:
            x = _conv_mixer(x, mix_norm_pre[i], mix_norm_post[i], conv_w_pw1[j], conv_b_pw1[j], conv_w_dw[j],
                            conv_b_dw[j], conv_ln_g[j], conv_ln_b[j], conv_w_pw2[j], conv_b_pw2[j])
        else:
            x = _nsa_mixer(x, mix_norm_pre[i], mix_norm_post[i], nsa_w_in[j], nsa_b_gate[j], nsa_cmp_pos[j],
                           nsa_cmp_w1[j], nsa_cmp_w2[j], nsa_w_out[j])
        x = ffn(x, i, 1)
    return x
```

```python
import functools
import math

import jax
import jax.numpy as jnp
import numpy as np
from jax import lax
from jax.experimental import pallas as pl
from jax.experimental.pallas import tpu as pltpu

RMS_EPS = 1e-6
LN_EPS = 1e-5
FFN_RESIDUAL_WEIGHT = 0.5
CONV_WIDTH = 31
NSA_HEADS = 16
NSA_KV_GROUPS = 4
NSA_REP = NSA_HEADS // NSA_KV_GROUPS
HEAD_DIM = 64
ROT_HALF = HEAD_DIM // 8
ROPE_THETA = 500000.0
CMP_LEN = 32
CMP_STRIDE = 16
SEL_BLOCK = 64
N_SEL = 16
WINDOW = 512
Q_BLOCK = 128
N_GATES = 3
NEG_INF = -1e30

LANES = 128
SUBLANES = 8
V7X_VMEM_LIMIT = 56 * 1024 * 1024

BF16 = jnp.bfloat16
F32 = jnp.float32


def _params(sem):
    return pltpu.CompilerParams(dimension_semantics=sem, vmem_limit_bytes=V7X_VMEM_LIMIT)


def _const_spec(shape):
    nd = len(shape)
    return pl.BlockSpec(shape, lambda *_: (0,) * nd, pipeline_mode=pl.Buffered(1))


def _rms(x, gain):
    return x * lax.rsqrt(jnp.mean(x * x, axis=-1, keepdims=True) + RMS_EPS) * gain


def _sigmoid(x):
    return 1.0 / (1.0 + jnp.exp(-x))


def _dot(a, b):
    return jnp.dot(a, b, preferred_element_type=F32)


def _dot_nt(a, b):
    return lax.dot_general(a, b, (((1,), (1,)), ((), ())), preferred_element_type=F32)


def _dot_tn(a, b):
    return lax.dot_general(a, b, (((0,), (0,)), ((), ())), preferred_element_type=F32)


FFN_ROWS = 512
FFN_CHUNK = 512


def _ffn_body(x_ref, gpre_ref, gpost_ref, wg_ref, wu_ref, wd_ref, o_ref):
    x = x_ref[...]
    h = _rms(x, gpre_ref[...]).astype(BF16)
    d_ff = wd_ref.shape[0]
    acc = jnp.zeros(x.shape, F32)
    for c0 in range(0, d_ff, FFN_CHUNK):
        c1 = min(c0 + FFN_CHUNK, d_ff)
        g = _dot(h, wg_ref[:, c0:c1])
        u = _dot(h, wu_ref[:, c0:c1])
        a = (g * _sigmoid(g) * u).astype(BF16)
        acc = acc + _dot(a, wd_ref[c0:c1, :])
    o_ref[...] = x + FFN_RESIDUAL_WEIGHT * _rms(acc, gpost_ref[...])


def _ffn_half(x2, g_pre, g_post, w_gate, w_up, w_down):
    t, d = x2.shape
    f = w_gate.shape[1]
    tm = min(FFN_ROWS, t)
    row = pl.BlockSpec((tm, d), lambda i: (i, 0))
    return pl.pallas_call(
        _ffn_body,
        grid=(t // tm,),
        in_specs=[row, _const_spec((1, d)), _const_spec((1, d)),
                  _const_spec((d, f)), _const_spec((d, f)), _const_spec((f, d))],
        out_specs=row,
        out_shape=jax.ShapeDtypeStruct((t, d), F32),
        compiler_params=_params(("arbitrary",)),
        name="ffn_half",
    )(x2, g_pre.reshape(1, d), g_post.reshape(1, d),
      w_gate.astype(BF16), w_up.astype(BF16), w_down.astype(BF16))


CONV_ROWS = 256
CONV_HALO = 32
CONV_STRIP = 32


def _conv_body(x_ref, gpre_ref, w1_ref, b1_ref, wdw_ref, bdw_ref, lng_ref, lnb_ref,
               w2_ref, b2_ref, gpost_ref, o_ref, buf_ref, dw_ref, shift_ref):
    ts, d = x_ref.shape[1], x_ref.shape[2]

    @pl.when(pl.program_id(1) == 0)
    def _():
        buf_ref[0:CONV_HALO, :] = jnp.zeros((CONV_HALO, d), F32)

    x = x_ref[0]
    h = _rms(x, gpre_ref[...]).astype(BF16)
    p = _dot(h, w1_ref[...]) + b1_ref[...]
    buf_ref[CONV_HALO:CONV_HALO + ts, :] = p[:, :d] * _sigmoid(p[:, d:])

    base = CONV_HALO - (CONV_WIDTH - 1)
    span = shift_ref.shape[1]
    for b in range(1, SUBLANES):
        shift_ref[b - 1] = buf_ref[b:b + span, :]

    for r0 in range(0, ts, CONV_STRIP):
        acc = jnp.zeros((CONV_STRIP, d), F32)
        for k in range(CONV_WIDTH):
            a, b = divmod(base + k, SUBLANES)
            lo = r0 + a * SUBLANES
            win = buf_ref[lo:lo + CONV_STRIP, :] if b == 0 else shift_ref[b - 1, lo:lo + CONV_STRIP, :]
            acc = acc + wdw_ref[k:k + 1, :] * win
        dw_ref[r0:r0 + CONV_STRIP, :] = acc
    buf_ref[0:CONV_HALO, :] = buf_ref[ts:ts + CONV_HALO, :]

    c = dw_ref[...] + bdw_ref[...]
    mu = jnp.mean(c, axis=-1, keepdims=True)
    cc = c - mu
    var = jnp.mean(cc * cc, axis=-1, keepdims=True)
    y = cc * lax.rsqrt(var + LN_EPS) * lng_ref[...] + lnb_ref[...]
    y = (y * _sigmoid(y)).astype(BF16)
    out = _dot(y, w2_ref[...]) + b2_ref[...]
    o_ref[0] = x + _rms(out, gpost_ref[...])


def _conv_mixer(x, g_pre, g_post, w_pw1, b_pw1, w_dw, b_dw, ln_g, ln_b, w_pw2, b_pw2):
    b, s, d = x.shape
    ts = min(CONV_ROWS, s)
    tile = pl.BlockSpec((1, ts, d), lambda bi, j: (bi, j, 0))
    vec = lambda n: _const_spec((1, n))
    return pl.pallas_call(
        _conv_body,
        grid=(b, s // ts),
        in_specs=[tile, vec(d), _const_spec((d, 2 * d)), vec(2 * d), _const_spec((CONV_WIDTH, d)),
                  vec(d), vec(d), vec(d), _const_spec((d, d)), vec(d), vec(d)],
        out_specs=tile,
        out_shape=jax.ShapeDtypeStruct((b, s, d), F32),
        scratch_shapes=[pltpu.VMEM((ts + CONV_HALO, d), F32), pltpu.VMEM((ts, d), F32),
                        pltpu.VMEM((SUBLANES - 1, ts + (CONV_HALO - 1) // SUBLANES * SUBLANES, d), F32)],
        compiler_params=_params(("arbitrary", "arbitrary")),
        name="conv_mixer",
    )(x, g_pre.reshape(1, d), w_pw1.astype(BF16), b_pw1.reshape(1, 2 * d), w_dw,
      b_dw.reshape(1, d), ln_g.reshape(1, d), ln_b.reshape(1, d), w_pw2.astype(BF16),
      b_pw2.reshape(1, d), g_post.reshape(1, d))


PROJ_ROWS = 512
Q_WIDTH = NSA_HEADS * HEAD_DIM
KV_WIDTH = NSA_KV_GROUPS * HEAD_DIM
KPAD_WIDTH = NSA_KV_GROUPS * LANES
GATE_PAD = LANES
Q_SCALE = HEAD_DIM ** -0.5 * math.log2(math.e)


def _rope_tables(s):
    pos = jnp.arange(s, dtype=F32)
    inv_freq = ROPE_THETA ** (-jnp.arange(0, 2 * ROT_HALF, 2, dtype=F32) / (2 * ROT_HALF))
    ang = pos[:, None] * inv_freq[None, :]
    return jnp.cos(ang), jnp.sin(ang)


def _proj_body(x_ref, gpre_ref, wtok_ref, wtr_ref, bg_ref, cosT_ref, sinT_ref, ck_ref, s1_ref, s2_ref,
               qT_ref, qrT_ref, ksel_ref, kwin_ref, vselT_ref, vwinT_ref, kc_ref, vc_ref, gate_ref):
    tm = x_ref.shape[1]
    h = _rms(x_ref[0], gpre_ref[...]).astype(BF16)
    tok = _dot(h, wtok_ref[...])
    tr = _dot_nt(wtr_ref[...], h)

    q = tr[0:Q_WIDTH] * Q_SCALE
    qT_ref[0] = q.astype(BF16)
    q3 = q.reshape(NSA_HEADS, HEAD_DIM, tm)
    cos, sin = cosT_ref[...], sinT_ref[...]
    x1, x2 = q3[:, 0:ROT_HALF], q3[:, ROT_HALF:2 * ROT_HALF]
    qr = jnp.concatenate([x1 * cos - x2 * sin, x2 * cos + x1 * sin, q3[:, 2 * ROT_HALF:]], axis=1)
    qrT_ref[0] = qr.reshape(Q_WIDTH, tm).astype(BF16)
    vselT_ref[0] = tr[Q_WIDTH:Q_WIDTH + KV_WIDTH].astype(BF16)
    vwinT_ref[0] = tr[Q_WIDTH + KV_WIDTH:Q_WIDTH + 2 * KV_WIDTH].astype(BF16)

    ck, s1, s2 = ck_ref[...], s1_ref[...], s2_ref[...]
    for out_ref, base in ((ksel_ref, 0), (kwin_ref, KPAD_WIDTH)):
        for g in range(NSA_KV_GROUPS):
            k = tok[:, base + g * LANES:base + (g + 1) * LANES]
            kr = k * ck + pltpu.roll(k, ROT_HALF, 1) * s1 + pltpu.roll(k, LANES - ROT_HALF, 1) * s2
            out_ref[0, :, g * LANES:(g + 1) * LANES] = kr.astype(BF16)
    c0 = 2 * KPAD_WIDTH
    kc_ref[0] = tok[:, c0:c0 + KV_WIDTH]
    vc_ref[0] = tok[:, c0 + KV_WIDTH:c0 + 2 * KV_WIDTH]
    gate_ref[0] = _sigmoid(tok[:, c0 + 2 * KV_WIDTH:] + bg_ref[...])


def _nsa_project(x, g_pre, w_in, b_gate):
    b, s, d = x.shape
    tm = min(PROJ_ROWS, s)
    n_gate = NSA_HEADS * N_GATES
    w_q = w_in[:, :Q_WIDTH]
    w_kv = w_in[:, Q_WIDTH:Q_WIDTH + 6 * KV_WIDTH].reshape(d, 6, NSA_KV_GROUPS, HEAD_DIM)
    w_gate = w_in[:, Q_WIDTH + 6 * KV_WIDTH:]

    def padded(w):
        return jnp.pad(w, ((0, 0), (0, 0), (0, LANES - HEAD_DIM))).reshape(d, KPAD_WIDTH)

    w_tok = jnp.concatenate(
        [padded(w_kv[:, 2]), padded(w_kv[:, 4]), w_kv[:, 0].reshape(d, KV_WIDTH),
         w_kv[:, 1].reshape(d, KV_WIDTH), jnp.pad(w_gate, ((0, 0), (0, GATE_PAD - n_gate)))],
        axis=1).astype(BF16)
    w_tr = jnp.concatenate(
        [w_q, w_kv[:, 3].reshape(d, KV_WIDTH), w_kv[:, 5].reshape(d, KV_WIDTH)], axis=1).T.astype(BF16)
    bg = jnp.pad(b_gate, (0, GATE_PAD - n_gate)).reshape(1, GATE_PAD)

    cos, sin = _rope_tables(s)
    zeros = jnp.zeros_like(sin)
    pad_to = lambda parts, fill: jnp.concatenate(
        parts + [jnp.full((s, LANES - 2 * ROT_HALF), fill, F32)], axis=1)
    ck = pad_to([cos, cos], 1.0)
    s1 = pad_to([zeros, sin], 0.0)
    s2 = pad_to([-sin, zeros], 0.0)

    ntok, ntr = w_tok.shape[1], w_tr.shape[0]
    tok_major = lambda w: pl.BlockSpec((1, tm, w), lambda bi, j: (bi, j, 0))
    tr_major = lambda r: pl.BlockSpec((1, r, tm), lambda bi, j: (bi, 0, j))
    return pl.pallas_call(
        _proj_body,
        grid=(b, s // tm),
        in_specs=[tok_major(d), _const_spec((1, d)), _const_spec((d, ntok)), _const_spec((ntr, d)),
                  _const_spec((1, GATE_PAD)),
                  pl.BlockSpec((ROT_HALF, tm), lambda bi, j: (0, j)),
                  pl.BlockSpec((ROT_HALF, tm), lambda bi, j: (0, j)),
                  pl.BlockSpec((tm, LANES), lambda bi, j: (j, 0)),
                  pl.BlockSpec((tm, LANES), lambda bi, j: (j, 0)),
                  pl.BlockSpec((tm, LANES), lambda bi, j: (j, 0))],
        out_specs=[tr_major(Q_WIDTH), tr_major(Q_WIDTH), tok_major(KPAD_WIDTH), tok_major(KPAD_WIDTH),
                   tr_major(KV_WIDTH), tr_major(KV_WIDTH), tok_major(KV_WIDTH), tok_major(KV_WIDTH),
                   tok_major(GATE_PAD)],
        out_shape=[jax.ShapeDtypeStruct((b, Q_WIDTH, s), BF16), jax.ShapeDtypeStruct((b, Q_WIDTH, s), BF16),
                   jax.ShapeDtypeStruct((b, s, KPAD_WIDTH), BF16), jax.ShapeDtypeStruct((b, s, KPAD_WIDTH), BF16),
                   jax.ShapeDtypeStruct((b, KV_WIDTH, s), BF16), jax.ShapeDtypeStruct((b, KV_WIDTH, s), BF16),
                   jax.ShapeDtypeStruct((b, s, KV_WIDTH), F32), jax.ShapeDtypeStruct((b, s, KV_WIDTH), F32),
                   jax.ShapeDtypeStruct((b, s, GATE_PAD), F32)],
        compiler_params=_params(("arbitrary", "arbitrary")),
        name="nsa_project",
    )(x, g_pre.reshape(1, d), w_tok, w_tr, bg, cos.T, sin.T, ck, s1, s2)


def _compress_body(rk_ref, rv_ref, pos_ref, w1_ref, w2k_ref, w2vT_ref, kc_ref, vcT_ref):
    nrow = rk_ref.shape[2]
    half = rk_ref.shape[3]

    def hidden(r, t):
        top = _dot((r + pos_ref[t, 0:1, :]).astype(BF16), w1_ref[t, 0:half, :])
        bot = _dot((r + pos_ref[t, 1:2, :]).astype(BF16), w1_ref[t, half:2 * half, :])
        hid = top + pltpu.roll(bot, nrow - 1, 0)
        return (hid * _sigmoid(hid)).astype(BF16)

    kc_ref[0, 0] = _dot(hidden(rk_ref[0, 0], 0), w2k_ref[...]).astype(BF16)
    vcT_ref[0, 0] = _dot_nt(w2vT_ref[...], hidden(rv_ref[0, 0], 1)).astype(BF16)


def _nsa_compress(kc_raw, vc_raw, cmp_pos, cmp_w1, cmp_w2):
    b, s, _ = kc_raw.shape
    nrow = s // CMP_STRIDE
    half = CMP_STRIDE * HEAD_DIM
    hid = cmp_w1.shape[-1]

    def rows(raw):
        r = raw.reshape(b, nrow, CMP_STRIDE, NSA_KV_GROUPS, HEAD_DIM)
        return r.transpose(0, 3, 1, 2, 4).reshape(b, NSA_KV_GROUPS, nrow, half)

    pos = cmp_pos.reshape(2, 2, half)
    w2k = jnp.pad(cmp_w2[0], ((0, 0), (0, LANES - HEAD_DIM))).astype(BF16)
    w2vT = cmp_w2[1].T.astype(BF16)
    blk = pl.BlockSpec((1, 1, nrow, half), lambda bi, g: (bi, g, 0, 0))
    return pl.pallas_call(
        _compress_body,
        grid=(b, NSA_KV_GROUPS),
        in_specs=[blk, blk, _const_spec((2, 2, half)), _const_spec((2, 2 * half, hid)),
                  _const_spec((hid, LANES)), _const_spec((HEAD_DIM, hid))],
        out_specs=[pl.BlockSpec((1, 1, nrow, LANES), lambda bi, g: (bi, g, 0, 0)),
                   pl.BlockSpec((1, 1, HEAD_DIM, nrow), lambda bi, g: (bi, g, 0, 0))],
        out_shape=[jax.ShapeDtypeStruct((b, NSA_KV_GROUPS, nrow, LANES), BF16),
                   jax.ShapeDtypeStruct((b, NSA_KV_GROUPS, HEAD_DIM, nrow), BF16)],
        compiler_params=_params(("arbitrary", "arbitrary")),
        name="nsa_compress",
    )(rows(kc_raw), rows(vc_raw), pos, cmp_w1.astype(BF16), w2k, w2vT)


SEL_CHUNK = 512
WIN_KEYS = WINDOW + Q_BLOCK
QLANES = NSA_REP * Q_BLOCK
N_FORCED = 3
ONES_ROWS = 16
CMP_ROWS_STEP = 128
MASKED_MAX_FLOOR = -1e29


def _with_ones(vT):
    return jnp.concatenate([vT, jnp.ones((ONES_ROWS, vT.shape[1]), BF16)], axis=0)


def _flash_step(s, vT, m, acc):
    m_new = jnp.maximum(m, jnp.max(s, axis=0, keepdims=True))
    p = jnp.exp2(s - m_new).astype(BF16)
    acc = jnp.exp2(m - m_new) * acc + _dot(_with_ones(vT), p)
    return m_new, acc


def _attn_body(qT_ref, qrT_ref, kc_ref, vcT_ref, ksel_ref, vselT_ref, kwin_ref, vwinT_ref, gate_ref,
               ovT_ref, oh_ref, o_ref, qaug_ref, s0_ref, s1_ref, m_ref, acc_ref, part_ref):
    i = pl.program_id(2)
    t0 = i * Q_BLOCK
    nblk = ovT_ref.shape[0]
    ncmp = kc_ref.shape[2]
    n_top = min(N_SEL, nblk) - N_FORCED

    def lanes_of_heads(ref):
        parts = [ref[0, r * HEAD_DIM:(r + 1) * HEAD_DIM, :] for r in range(NSA_REP)]
        return jnp.concatenate(
            [jnp.concatenate(parts, axis=1), jnp.zeros((LANES - HEAD_DIM, QLANES), BF16)], axis=0)

    qT = lanes_of_heads(qT_ref)
    qrT = lanes_of_heads(qrT_ref)
    pos1 = t0 + lax.broadcasted_iota(jnp.int32, (1, Q_BLOCK), 1)
    pos = jnp.concatenate([pos1] * NSA_REP, axis=1)

    d = HEAD_DIM
    qaug_ref[0:LANES, :] = qrT

    def chunk_start(j):
        return pl.multiple_of(j * SEL_CHUNK, SEL_CHUNK)

    def scores_into(s_ref, j):
        k0 = chunk_start(j)
        ka = jnp.concatenate([ksel_ref[0, pl.ds(k0, SEL_CHUNK), :], oh_ref[pl.ds(k0, SEL_CHUNK), :]], axis=1)
        s_ref[...] = _dot(ka, qaug_ref[...])

    def select_and_window(cmp_rows, steady):
        s = _dot(kc_ref[0, 0, 0:cmp_rows, :], qT)
        cend = lax.broadcasted_iota(jnp.int32, (cmp_rows, QLANES), 0) * CMP_STRIDE + (CMP_LEN - 1)
        s = jnp.where(cend <= pos, s, NEG_INF)
        m = jnp.maximum(jnp.max(s, axis=0, keepdims=True), MASKED_MAX_FLOOR)
        p = jnp.exp2(s - m)
        inv_l = 1.0 / jnp.maximum(jnp.sum(p, axis=0, keepdims=True), 1e-30)
        o_cmp = _dot(vcT_ref[0, 0, :, 0:cmp_rows], p.astype(BF16)) * inv_l

        pn = p * inv_l
        ph = pn[:, 0:Q_BLOCK]
        for r in range(1, NSA_REP):
            ph = ph + pn[:, r * Q_BLOCK:(r + 1) * Q_BLOCK]
        hi = ph.astype(BF16)
        lo = (ph - hi.astype(F32)).astype(BF16)
        ovT = ovT_ref[:, 0:cmp_rows]
        imp = _dot(ovT, hi) + _dot(ovT, lo)

        blk = lax.broadcasted_iota(jnp.int32, (nblk, Q_BLOCK), 0)
        cur = pos1 // SEL_BLOCK
        forced = (blk == 0) | (blk == cur) | (blk == cur - 1)
        v0 = jnp.where(forced, -1.0, jnp.where(blk * SEL_BLOCK <= pos1, imp, -1.0))
        left = v0
        for _ in range(n_top):
            left = jnp.where(left == jnp.max(left, axis=0, keepdims=True), -2.0, left)
        taken = left != v0
        n_taken = jnp.sum(jnp.where(taken, jnp.where(v0 >= 0.0, 1.0, 0.0), 0.0), axis=0, keepdims=True)

        def store_bias(left):
            bias = jnp.where(forced, 0.0, jnp.where(left != v0, 0.0, NEG_INF)).astype(BF16)
            bias = jnp.concatenate([bias] * NSA_REP, axis=1)
            if nblk < LANES:
                bias = jnp.concatenate([bias, jnp.zeros((LANES - nblk, QLANES), BF16)], axis=0)
            qaug_ref[LANES:2 * LANES, :] = bias

        store_bias(left)
        scores_into(s0_ref, 0)

        if steady:
            w0 = pl.multiple_of(t0 - WINDOW, Q_BLOCK)
            s = _dot(kwin_ref[0, pl.ds(w0, WIN_KEYS), :], qrT)
            kk = lax.broadcasted_iota(jnp.int32, (Q_BLOCK, QLANES), 0)
            qq = lax.broadcasted_iota(jnp.int32, (Q_BLOCK, QLANES), 1) % Q_BLOCK
            s = jnp.concatenate([jnp.where(kk > qq, s[0:Q_BLOCK], NEG_INF), s[Q_BLOCK:WINDOW],
                                 jnp.where(kk <= qq, s[WINDOW:WIN_KEYS], NEG_INF)], axis=0)
        else:
            w0 = pl.multiple_of(jnp.maximum(t0 - WINDOW, 0), Q_BLOCK)
            s = _dot(kwin_ref[0, pl.ds(w0, WIN_KEYS), :], qrT)
            kpos = w0 + lax.broadcasted_iota(jnp.int32, (WIN_KEYS, QLANES), 0)
            s = jnp.where(kpos <= pos, jnp.where(kpos > pos - WINDOW, s, NEG_INF), NEG_INF)
        p = jnp.exp2(s - jnp.max(s, axis=0, keepdims=True)).astype(BF16)
        a_w = _dot(_with_ones(vwinT_ref[0, :, pl.ds(w0, WIN_KEYS)]), p)
        gate = gate_ref[0, 0, 0]
        part_ref[...] = gate[0:1] * o_cmp + gate[2:3] * (a_w[0:d] / a_w[d:d + 1])

        @pl.when(jnp.max(n_taken) > n_top)
        def _():
            val = v0
            for _ in range(n_top):
                top = jnp.max(val, axis=0, keepdims=True)
                first = jnp.min(jnp.where(val == top, blk, nblk), axis=0, keepdims=True)
                val = jnp.where(blk == first, -2.0, val)
            store_bias(val)
            scores_into(s0_ref, 0)

    def visible_cmp(block):
        return (block * Q_BLOCK + Q_BLOCK - CMP_LEN) // CMP_STRIDE + 1

    nq = ksel_ref.shape[1] // Q_BLOCK
    steady_from = WINDOW // Q_BLOCK
    first_rows = min(CMP_ROWS_STEP, ncmp)
    assert visible_cmp(steady_from - 1) <= first_rows
    pl.when(i < steady_from)(functools.partial(select_and_window, first_rows, False))
    lo = steady_from
    for rows in range(first_rows, ncmp + 1, CMP_ROWS_STEP):
        hi = next((blk_i for blk_i in range(lo, nq) if visible_cmp(blk_i) > rows), nq)
        if hi > lo:
            pl.when((i >= lo) & (i < hi))(functools.partial(select_and_window, rows, True))
            lo = hi
    assert lo == nq

    def absorb(s_ref, j, diagonal):
        k0 = chunk_start(j)
        s = s_ref[...]
        if diagonal:
            kpos = k0 + lax.broadcasted_iota(jnp.int32, (SEL_CHUNK, QLANES), 0)
            s = jnp.where(kpos <= pos, s, NEG_INF)
        m_new, acc = _flash_step(s, vselT_ref[0, :, pl.ds(k0, SEL_CHUNK)], m_ref[...], acc_ref[...])
        m_ref[...] = m_new
        acc_ref[...] = acc

    last = t0 // SEL_CHUNK
    m_ref[...] = jnp.full(m_ref.shape, NEG_INF, F32)
    acc_ref[...] = jnp.zeros(acc_ref.shape, F32)

    def pair(t, carry):
        scores_into(s1_ref, 2 * t + 1)
        absorb(s0_ref, 2 * t, False)
        scores_into(s0_ref, 2 * t + 2)
        absorb(s1_ref, 2 * t + 1, False)
        return carry

    lax.fori_loop(0, last // 2, pair, 0)

    @pl.when(last % 2 == 1)
    def _():
        scores_into(s1_ref, last)
        absorb(s0_ref, last - 1, False)
        absorb(s1_ref, last, True)

    @pl.when(last % 2 == 0)
    def _():
        absorb(s0_ref, last, True)

    a_s = acc_ref[...]
    out = part_ref[...] + gate_ref[0, 0, 0][1:2] * (a_s[0:d] / a_s[d:d + 1])
    for r in range(NSA_REP):
        o_ref[0, r * HEAD_DIM:(r + 1) * HEAD_DIM, :] = out[:, r * Q_BLOCK:(r + 1) * Q_BLOCK].astype(BF16)


def _nsa_attention(qT, qrT, kcmp, vcmpT, ksel, vselT, kwin, vwinT, gates):
    b, _, s = qT.shape
    nq = s // Q_BLOCK
    nblk = s // SEL_BLOCK
    ncmp = kcmp.shape[2]
    rows = NSA_REP * HEAD_DIM
    assert nblk <= LANES and s % SEL_CHUNK == 0 and s >= WIN_KEYS

    gt = gates[:, :, :NSA_HEADS * N_GATES].reshape(b, nq, Q_BLOCK, NSA_KV_GROUPS, NSA_REP, N_GATES)
    gt = gt.transpose(0, 3, 1, 5, 4, 2).reshape(b, NSA_KV_GROUPS, nq, N_GATES, QLANES)

    cstart = np.arange(ncmp) * CMP_STRIDE
    sstart = np.arange(nblk) * SEL_BLOCK
    ovT = ((cstart[None, :] < sstart[:, None] + SEL_BLOCK) & (cstart[None, :] + CMP_LEN > sstart[:, None]))
    ovT = jnp.asarray(ovT, BF16)
    onehot = jnp.asarray(np.arange(s)[:, None] // SEL_BLOCK == np.arange(LANES)[None, :], BF16)

    q_spec = pl.BlockSpec((1, rows, Q_BLOCK), lambda bi, g, i: (bi, g, i))
    k_spec = pl.BlockSpec((1, s, LANES), lambda bi, g, i: (bi, 0, g))
    vT_spec = pl.BlockSpec((1, HEAD_DIM, s), lambda bi, g, i: (bi, g, 0))
    return pl.pallas_call(
        _attn_body,
        grid=(b, NSA_KV_GROUPS, nq),
        in_specs=[q_spec, q_spec,
                  pl.BlockSpec((1, 1, ncmp, LANES), lambda bi, g, i: (bi, g, 0, 0)),
                  pl.BlockSpec((1, 1, HEAD_DIM, ncmp), lambda bi, g, i: (bi, g, 0, 0)),
                  k_spec, vT_spec, k_spec, vT_spec,
                  pl.BlockSpec((1, 1, 1, N_GATES, QLANES), lambda bi, g, i: (bi, g, i, 0, 0)),
                  _const_spec((nblk, ncmp)), _const_spec((s, LANES))],
        out_specs=q_spec,
        out_shape=jax.ShapeDtypeStruct((b, NSA_HEADS * HEAD_DIM, s), BF16),
        scratch_shapes=[pltpu.VMEM((2 * LANES, QLANES), BF16),
                        pltpu.VMEM((SEL_CHUNK, QLANES), F32), pltpu.VMEM((SEL_CHUNK, QLANES), F32),
                        pltpu.VMEM((1, QLANES), F32), pltpu.VMEM((HEAD_DIM + ONES_ROWS, QLANES), F32),
                        pltpu.VMEM((HEAD_DIM, QLANES), F32)],
        compiler_params=_params(("arbitrary", "arbitrary", "arbitrary")),
        name="nsa_attention",
    )(qT, qrT, kcmp, vcmpT, ksel, vselT, kwin, vwinT, gt, ovT, onehot)


OUT_ROWS = 512


def _out_body(x_ref, aT_ref, w_ref, gpost_ref, o_ref):
    h = _dot_tn(aT_ref[0], w_ref[...])
    o_ref[0] = x_ref[0] + _rms(h, gpost_ref[...])


def _nsa_output(x, attnT, w_out, g_post):
    b, s, d = x.shape
    tm = min(OUT_ROWS, s)
    tile = pl.BlockSpec((1, tm, d), lambda bi, j: (bi, j, 0))
    return pl.pallas_call(
        _out_body,
        grid=(b, s // tm),
        in_specs=[tile, pl.BlockSpec((1, attnT.shape[1], tm), lambda bi, j: (bi, 0, j)),
                  _const_spec(w_out.shape), _const_spec((1, d))],
        out_specs=tile,
        out_shape=jax.ShapeDtypeStruct((b, s, d), F32),
        compiler_params=_params(("arbitrary", "arbitrary")),
        name="nsa_output",
    )(x, attnT, w_out.astype(BF16), g_post.reshape(1, d))


def _nsa_mixer(x, g_pre, g_post, w_in, b_gate, cmp_pos, cmp_w1, cmp_w2, w_out):
    qT, qrT, ksel, kwin, vselT, vwinT, kc_raw, vc_raw, gates = _nsa_project(x, g_pre, w_in, b_gate)
    kcmp, vcmpT = _nsa_compress(kc_raw, vc_raw, cmp_pos, cmp_w1, cmp_w2)
    attnT = _nsa_attention(qT, qrT, kcmp, vcmpT, ksel, vselT, kwin, vwinT, gates)
    return _nsa_output(x, attnT, w_out, g_post)


def kernel(x, mix_norm_pre, mix_norm_post, ffn_norm_pre, ffn_norm_post, ffn_w_gate, ffn_w_up, ffn_w_down,
           conv_w_pw1, conv_b_pw1, conv_w_dw, conv_b_dw, conv_ln_g, conv_ln_b, conv_w_pw2, conv_b_pw2,
           nsa_w_in, nsa_b_gate, nsa_cmp_pos, nsa_cmp_w1, nsa_cmp_w2, nsa_w_out):
    b, s, d = x.shape
    depth = mix_norm_pre.shape[0]
    n_mixers = 2

    def ffn(x, i, half):
        y = _ffn_half(x.reshape(b * s, d), ffn_norm_pre[i, half], ffn_norm_post[i, half],
                      ffn_w_gate[i, half], ffn_w_up[i, half], ffn_w_down[i, half])
        return y.reshape(b, s, d)

    for i in range(depth):
        x = ffn(x, i, 0)
        j = i // n_mixers
        if i % n_mixers == 0:
            x = _conv_mixer(x, mix_norm_pre[i], mix_norm_post[i], conv_w_pw1[j], conv_b_pw1[j], conv_w_dw[j],
                            conv_b_dw[j], conv_ln_g[j], conv_ln_b[j], conv_w_pw2[j], conv_b_pw2[j])
        else:
            x = _nsa_mixer(x, mix_norm_pre[i], mix_norm_post[i], nsa_w_in[j], nsa_b_gate[j], nsa_cmp_pos[j],
                           nsa_cmp_w1[j], nsa_cmp_w2[j], nsa_w_out[j])
        x = ffn(x, i, 1)
    return x
```

```python
import functools
import math

import jax
import jax.numpy as jnp
import numpy as np
from jax import lax
from jax.experimental import pallas as pl
from jax.experimental.pallas import tpu as pltpu

RMS_EPS = 1e-6
LN_EPS = 1e-5
FFN_RESIDUAL_WEIGHT = 0.5
CONV_WIDTH = 31
NSA_HEADS = 16
NSA_KV_GROUPS = 4
NSA_REP = NSA_HEADS // NSA_KV_GROUPS
HEAD_DIM = 64
ROT_HALF = HEAD_DIM // 8
ROPE_THETA = 500000.0
CMP_LEN = 32
CMP_STRIDE = 16
SEL_BLOCK = 64
N_SEL = 16
WINDOW = 512
Q_BLOCK = 128
N_GATES = 3
NEG_INF = -1e30

LANES = 128
SUBLANES = 8
V7X_VMEM_LIMIT = 56 * 1024 * 1024

BF16 = jnp.bfloat16
F32 = jnp.float32


def _params(sem):
    return pltpu.CompilerParams(dimension_semantics=sem, vmem_limit_bytes=V7X_VMEM_LIMIT)


def _const_spec(shape):
    nd = len(shape)
    return pl.BlockSpec(shape, lambda *_: (0,) * nd, pipeline_mode=pl.Buffered(1))


def _rms(x, gain):
    return x * lax.rsqrt(jnp.mean(x * x, axis=-1, keepdims=True) + RMS_EPS) * gain


def _sigmoid(x):
    return 1.0 / (1.0 + jnp.exp(-x))


def _dot(a, b):
    return jnp.dot(a, b, preferred_element_type=F32)


def _dot_nt(a, b):
    return lax.dot_general(a, b, (((1,), (1,)), ((), ())), preferred_element_type=F32)


def _dot_tn(a, b):
    return lax.dot_general(a, b, (((0,), (0,)), ((), ())), preferred_element_type=F32)


FFN_ROWS = 512
FFN_CHUNK = 512


def _ffn_body(x_ref, gpre_ref, gpost_ref, wg_ref, wu_ref, wd_ref, o_ref):
    x = x_ref[...]
    h = _rms(x, gpre_ref[...]).astype(BF16)
    d_ff = wd_ref.shape[0]
    acc = jnp.zeros(x.shape, F32)
    for c0 in range(0, d_ff, FFN_CHUNK):
        c1 = min(c0 + FFN_CHUNK, d_ff)
        g = _dot(h, wg_ref[:, c0:c1])
        u = _dot(h, wu_ref[:, c0:c1])
        a = (g * _sigmoid(g) * u).astype(BF16)
        acc = acc + _dot(a, wd_ref[c0:c1, :])
    o_ref[...] = x + FFN_RESIDUAL_WEIGHT * _rms(acc, gpost_ref[...])


def _ffn_half(x2, g_pre, g_post, w_gate, w_up, w_down):
    t, d = x2.shape
    f = w_gate.shape[1]
    tm = min(FFN_ROWS, t)
    row = pl.BlockSpec((tm, d), lambda i: (i, 0))
    return pl.pallas_call(
        _ffn_body,
        grid=(t // tm,),
        in_specs=[row, _const_spec((1, d)), _const_spec((1, d)),
                  _const_spec((d, f)), _const_spec((d, f)), _const_spec((f, d))],
        out_specs=row,
        out_shape=jax.ShapeDtypeStruct((t, d), F32),
        compiler_params=_params(("arbitrary",)),
        name="ffn_half",
    )(x2, g_pre.reshape(1, d), g_post.reshape(1, d),
      w_gate.astype(BF16), w_up.astype(BF16), w_down.astype(BF16))


CONV_ROWS = 256
CONV_HALO = 32
CONV_STRIP = 32


def _conv_body(x_ref, gpre_ref, w1_ref, b1_ref, wdw_ref, bdw_ref, lng_ref, lnb_ref,
               w2_ref, b2_ref, gpost_ref, o_ref, buf_ref, dw_ref, shift_ref):
    ts, d = x_ref.shape[1], x_ref.shape[2]

    @pl.when(pl.program_id(1) == 0)
    def _():
        buf_ref[0:CONV_HALO, :] = jnp.zeros((CONV_HALO, d), F32)

    x = x_ref[0]
    h = _rms(x, gpre_ref[...]).astype(BF16)
    p = _dot(h, w1_ref[...]) + b1_ref[...]
    buf_ref[CONV_HALO:CONV_HALO + ts, :] = p[:, :d] * _sigmoid(p[:, d:])

    base = CONV_HALO - (CONV_WIDTH - 1)
    span = shift_ref.shape[1]
    for b in range(1, SUBLANES):
        shift_ref[b - 1] = buf_ref[b:b + span, :]

    for r0 in range(0, ts, CONV_STRIP):
        acc = jnp.zeros((CONV_STRIP, d), F32)
        for k in range(CONV_WIDTH):
            a, b = divmod(base + k, SUBLANES)
            lo = r0 + a * SUBLANES
            win = buf_ref[lo:lo + CONV_STRIP, :] if b == 0 else shift_ref[b - 1, lo:lo + CONV_STRIP, :]
            acc = acc + wdw_ref[k:k + 1, :] * win
        dw_ref[r0:r0 + CONV_STRIP, :] = acc
    buf_ref[0:CONV_HALO, :] = buf_ref[ts:ts + CONV_HALO, :]

    c = dw_ref[...] + bdw_ref[...]
    mu = jnp.mean(c, axis=-1, keepdims=True)
    cc = c - mu
    var = jnp.mean(cc * cc, axis=-1, keepdims=True)
    y = cc * lax.rsqrt(var + LN_EPS) * lng_ref[...] + lnb_ref[...]
    y = (y * _sigmoid(y)).astype(BF16)
    out = _dot(y, w2_ref[...]) + b2_ref[...]
    o_ref[0] = x + _rms(out, gpost_ref[...])


def _conv_mixer(x, g_pre, g_post, w_pw1, b_pw1, w_dw, b_dw, ln_g, ln_b, w_pw2, b_pw2):
    b, s, d = x.shape
    ts = min(CONV_ROWS, s)
    tile = pl.BlockSpec((1, ts, d), lambda bi, j: (bi, j, 0))
    vec = lambda n: _const_spec((1, n))
    return pl.pallas_call(
        _conv_body,
        grid=(b, s // ts),
        in_specs=[tile, vec(d), _const_spec((d, 2 * d)), vec(2 * d), _const_spec((CONV_WIDTH, d)),
                  vec(d), vec(d), vec(d), _const_spec((d, d)), vec(d), vec(d)],
        out_specs=tile,
        out_shape=jax.ShapeDtypeStruct((b, s, d), F32),
        scratch_shapes=[pltpu.VMEM((ts + CONV_HALO, d), F32), pltpu.VMEM((ts, d), F32),
                        pltpu.VMEM((SUBLANES - 1, ts + (CONV_HALO - 1) // SUBLANES * SUBLANES, d), F32)],
        compiler_params=_params(("arbitrary", "arbitrary")),
        name="conv_mixer",
    )(x, g_pre.reshape(1, d), w_pw1.astype(BF16), b_pw1.reshape(1, 2 * d), w_dw,
      b_dw.reshape(1, d), ln_g.reshape(1, d), ln_b.reshape(1, d), w_pw2.astype(BF16),
      b_pw2.reshape(1, d), g_post.reshape(1, d))


PROJ_ROWS = 512
Q_WIDTH = NSA_HEADS * HEAD_DIM
KV_WIDTH = NSA_KV_GROUPS * HEAD_DIM
KPAD_WIDTH = NSA_KV_GROUPS * LANES
GATE_PAD = LANES
Q_SCALE = HEAD_DIM ** -0.5 * math.log2(math.e)


def _rope_tables(s):
    pos = jnp.arange(s, dtype=F32)
    inv_freq = ROPE_THETA ** (-jnp.arange(0, 2 * ROT_HALF, 2, dtype=F32) / (2 * ROT_HALF))
    ang = pos[:, None] * inv_freq[None, :]
    return jnp.cos(ang), jnp.sin(ang)


def _proj_body(x_ref, gpre_ref, wtok_ref, wtr_ref, bg_ref, cosT_ref, sinT_ref, ck_ref, s1_ref, s2_ref,
               qT_ref, qrT_ref, ksel_ref, kwin_ref, vselT_ref, vwinT_ref, kc_ref, vc_ref, gate_ref):
    tm = x_ref.shape[1]
    h = _rms(x_ref[0], gpre_ref[...]).astype(BF16)
    tok = _dot(h, wtok_ref[...])
    tr = _dot_nt(wtr_ref[...], h)

    q = tr[0:Q_WIDTH] * Q_SCALE
    qT_ref[0] = q.astype(BF16)
    q3 = q.reshape(NSA_HEADS, HEAD_DIM, tm)
    cos, sin = cosT_ref[...], sinT_ref[...]
    x1, x2 = q3[:, 0:ROT_HALF], q3[:, ROT_HALF:2 * ROT_HALF]
    qr = jnp.concatenate([x1 * cos - x2 * sin, x2 * cos + x1 * sin, q3[:, 2 * ROT_HALF:]], axis=1)
    qrT_ref[0] = qr.reshape(Q_WIDTH, tm).astype(BF16)
    vselT_ref[0] = tr[Q_WIDTH:Q_WIDTH + KV_WIDTH].astype(BF16)
    vwinT_ref[0] = tr[Q_WIDTH + KV_WIDTH:Q_WIDTH + 2 * KV_WIDTH].astype(BF16)

    ck, s1, s2 = ck_ref[...], s1_ref[...], s2_ref[...]
    for out_ref, base in ((ksel_ref, 0), (kwin_ref, KPAD_WIDTH)):
        for g in range(NSA_KV_GROUPS):
            k = tok[:, base + g * LANES:base + (g + 1) * LANES]
            kr = k * ck + pltpu.roll(k, ROT_HALF, 1) * s1 + pltpu.roll(k, LANES - ROT_HALF, 1) * s2
            out_ref[0, :, g * LANES:(g + 1) * LANES] = kr.astype(BF16)
    c0 = 2 * KPAD_WIDTH
    kc_ref[0] = tok[:, c0:c0 + KV_WIDTH]
    vc_ref[0] = tok[:, c0 + KV_WIDTH:c0 + 2 * KV_WIDTH]
    gate_ref[0] = _sigmoid(tok[:, c0 + 2 * KV_WIDTH:] + bg_ref[...])


def _nsa_project(x, g_pre, w_in, b_gate):
    b, s, d = x.shape
    tm = min(PROJ_ROWS, s)
    n_gate = NSA_HEADS * N_GATES
    w_q = w_in[:, :Q_WIDTH]
    w_kv = w_in[:, Q_WIDTH:Q_WIDTH + 6 * KV_WIDTH].reshape(d, 6, NSA_KV_GROUPS, HEAD_DIM)
    w_gate = w_in[:, Q_WIDTH + 6 * KV_WIDTH:]

    def padded(w):
        return jnp.pad(w, ((0, 0), (0, 0), (0, LANES - HEAD_DIM))).reshape(d, KPAD_WIDTH)

    w_tok = jnp.concatenate(
        [padded(w_kv[:, 2]), padded(w_kv[:, 4]), w_kv[:, 0].reshape(d, KV_WIDTH),
         w_kv[:, 1].reshape(d, KV_WIDTH), jnp.pad(w_gate, ((0, 0), (0, GATE_PAD - n_gate)))],
        axis=1).astype(BF16)
    w_tr = jnp.concatenate(
        [w_q, w_kv[:, 3].reshape(d, KV_WIDTH), w_kv[:, 5].reshape(d, KV_WIDTH)], axis=1).T.astype(BF16)
    bg = jnp.pad(b_gate, (0, GATE_PAD - n_gate)).reshape(1, GATE_PAD)

    cos, sin = _rope_tables(s)
    zeros = jnp.zeros_like(sin)
    pad_to = lambda parts, fill: jnp.concatenate(
        parts + [jnp.full((s, LANES - 2 * ROT_HALF), fill, F32)], axis=1)
    ck = pad_to([cos, cos], 1.0)
    s1 = pad_to([zeros, sin], 0.0)
    s2 = pad_to([-sin, zeros], 0.0)

    ntok, ntr = w_tok.shape[1], w_tr.shape[0]
    tok_major = lambda w: pl.BlockSpec((1, tm, w), lambda bi, j: (bi, j, 0))
    tr_major = lambda r: pl.BlockSpec((1, r, tm), lambda bi, j: (bi, 0, j))
    return pl.pallas_call(
        _proj_body,
        grid=(b, s // tm),
        in_specs=[tok_major(d), _const_spec((1, d)), _const_spec((d, ntok)), _const_spec((ntr, d)),
                  _const_spec((1, GATE_PAD)),
                  pl.BlockSpec((ROT_HALF, tm), lambda bi, j: (0, j)),
                  pl.BlockSpec((ROT_HALF, tm), lambda bi, j: (0, j)),
                  pl.BlockSpec((tm, LANES), lambda bi, j: (j, 0)),
                  pl.BlockSpec((tm, LANES), lambda bi, j: (j, 0)),
                  pl.BlockSpec((tm, LANES), lambda bi, j: (j, 0))],
        out_specs=[tr_major(Q_WIDTH), tr_major(Q_WIDTH), tok_major(KPAD_WIDTH), tok_major(KPAD_WIDTH),
                   tr_major(KV_WIDTH), tr_major(KV_WIDTH), tok_major(KV_WIDTH), tok_major(KV_WIDTH),
                   tok_major(GATE_PAD)],
        out_shape=[jax.ShapeDtypeStruct((b, Q_WIDTH, s), BF16), jax.ShapeDtypeStruct((b, Q_WIDTH, s), BF16),
                   jax.ShapeDtypeStruct((b, s, KPAD_WIDTH), BF16), jax.ShapeDtypeStruct((b, s, KPAD_WIDTH), BF16),
                   jax.ShapeDtypeStruct((b, KV_WIDTH, s), BF16), jax.ShapeDtypeStruct((b, KV_WIDTH, s), BF16),
                   jax.ShapeDtypeStruct((b, s, KV_WIDTH), F32), jax.ShapeDtypeStruct((b, s, KV_WIDTH), F32),
                   jax.ShapeDtypeStruct((b, s, GATE_PAD), F32)],
        compiler_params=_params(("arbitrary", "arbitrary")),
        name="nsa_project",
    )(x, g_pre.reshape(1, d), w_tok, w_tr, bg, cos.T, sin.T, ck, s1, s2)


def _compress_body(rk_ref, rv_ref, pos_ref, w1_ref, w2k_ref, w2vT_ref, kc_ref, vcT_ref):
    nrow = rk_ref.shape[2]
    half = rk_ref.shape[3]

    def hidden(r, t):
        top = _dot((r + pos_ref[t, 0:1, :]).astype(BF16), w1_ref[t, 0:half, :])
        bot = _dot((r + pos_ref[t, 1:2, :]).astype(BF16), w1_ref[t, half:2 * half, :])
        hid = top + pltpu.roll(bot, nrow - 1, 0)
        return (hid * _sigmoid(hid)).astype(BF16)

    kc_ref[0, 0] = _dot(hidden(rk_ref[0, 0], 0), w2k_ref[...]).astype(BF16)
    vcT_ref[0, 0] = _dot_nt(w2vT_ref[...], hidden(rv_ref[0, 0], 1)).astype(BF16)


def _nsa_compress(kc_raw, vc_raw, cmp_pos, cmp_w1, cmp_w2):
    b, s, _ = kc_raw.shape
    nrow = s // CMP_STRIDE
    half = CMP_STRIDE * HEAD_DIM
    hid = cmp_w1.shape[-1]

    def rows(raw):
        r = raw.reshape(b, nrow, CMP_STRIDE, NSA_KV_GROUPS, HEAD_DIM)
        return r.transpose(0, 3, 1, 2, 4).reshape(b, NSA_KV_GROUPS, nrow, half)

    pos = cmp_pos.reshape(2, 2, half)
    w2k = jnp.pad(cmp_w2[0], ((0, 0), (0, LANES - HEAD_DIM))).astype(BF16)
    w2vT = cmp_w2[1].T.astype(BF16)
    blk = pl.BlockSpec((1, 1, nrow, half), lambda bi, g: (bi, g, 0, 0))
    return pl.pallas_call(
        _compress_body,
        grid=(b, NSA_KV_GROUPS),
        in_specs=[blk, blk, _const_spec((2, 2, half)), _const_spec((2, 2 * half, hid)),
                  _const_spec((hid, LANES)), _const_spec((HEAD_DIM, hid))],
        out_specs=[pl.BlockSpec((1, 1, nrow, LANES), lambda bi, g: (bi, g, 0, 0)),
                   pl.BlockSpec((1, 1, HEAD_DIM, nrow), lambda bi, g: (bi, g, 0, 0))],
        out_shape=[jax.ShapeDtypeStruct((b, NSA_KV_GROUPS, nrow, LANES), BF16),
                   jax.ShapeDtypeStruct((b, NSA_KV_GROUPS, HEAD_DIM, nrow), BF16)],
        compiler_params=_params(("arbitrary", "arbitrary")),
        name="nsa_compress",
    )(rows(kc_raw), rows(vc_raw), pos, cmp_w1.astype(BF16), w2k, w2vT)


SEL_CHUNK = 512
CHUNK_BLOCKS = SEL_CHUNK // SEL_BLOCK
WIN_KEYS = WINDOW + Q_BLOCK
QLANES = NSA_REP * Q_BLOCK
N_FORCED = 3
ONES_ROWS = 16
CMP_ROWS_STEP = 128
MASKED_MAX_FLOOR = -1e29


def _with_ones(vT):
    return jnp.concatenate([vT, jnp.ones((ONES_ROWS, vT.shape[1]), BF16)], axis=0)


def _attn_body(qT_ref, qrT_ref, kc_ref, vcT_ref, ksel_ref, vselT_ref, kwin_ref, vwinT_ref, gate_ref,
               ovT_ref, oh_ref, o_ref, qaug_ref, s0_ref, s1_ref, m_ref, acc_ref, part_ref, swin_ref,
               top0_ref, top1_ref):
    i = pl.program_id(2)
    t0 = i * Q_BLOCK
    nblk = ovT_ref.shape[0]
    ncmp = kc_ref.shape[2]
    n_top = min(N_SEL, nblk) - N_FORCED

    def lanes_of_heads(ref):
        parts = [ref[0, r * HEAD_DIM:(r + 1) * HEAD_DIM, :] for r in range(NSA_REP)]
        return jnp.concatenate(
            [jnp.concatenate(parts, axis=1), jnp.zeros((LANES - HEAD_DIM, QLANES), BF16)], axis=0)

    qT = lanes_of_heads(qT_ref)
    qrT = lanes_of_heads(qrT_ref)
    pos1 = t0 + lax.broadcasted_iota(jnp.int32, (1, Q_BLOCK), 1)
    pos = jnp.concatenate([pos1] * NSA_REP, axis=1)

    d = HEAD_DIM
    qaug_ref[0:LANES, :] = qrT

    def chunk_start(j):
        return pl.multiple_of(j * SEL_CHUNK, SEL_CHUNK)

    def scores_into(s_ref, top_ref, j):
        k0 = chunk_start(j)
        ka = jnp.concatenate([ksel_ref[0, pl.ds(k0, SEL_CHUNK), :], oh_ref[pl.ds(k0, SEL_CHUNK), :]], axis=1)
        s = _dot(ka, qaug_ref[...])
        s_ref[...] = s
        top_ref[...] = jnp.max(s, axis=0, keepdims=True)

    def select_and_window(cmp_rows, steady):
        s = _dot(kc_ref[0, 0, 0:cmp_rows, :], qT)
        w0 = pl.multiple_of(t0 - WINDOW if steady else jnp.maximum(t0 - WINDOW, 0), Q_BLOCK)
        swin_ref[...] = _dot(kwin_ref[0, pl.ds(w0, WIN_KEYS), :], qrT)
        s0_ref[...] = _dot(ksel_ref[0, 0:SEL_CHUNK, :], qrT)
        cend = lax.broadcasted_iota(jnp.int32, (cmp_rows, QLANES), 0) * CMP_STRIDE + (CMP_LEN - 1)
        s = jnp.where(cend <= pos, s, NEG_INF)
        m = jnp.maximum(jnp.max(s, axis=0, keepdims=True), MASKED_MAX_FLOOR)
        p = jnp.exp2(s - m)
        inv_l = 1.0 / jnp.maximum(jnp.sum(p, axis=0, keepdims=True), 1e-30)
        o_cmp = _dot(vcT_ref[0, 0, :, 0:cmp_rows], p.astype(BF16)) * inv_l

        pn = p * inv_l
        ph = pn[:, 0:Q_BLOCK]
        for r in range(1, NSA_REP):
            ph = ph + pn[:, r * Q_BLOCK:(r + 1) * Q_BLOCK]
        hi = ph.astype(BF16)
        lo = (ph - hi.astype(F32)).astype(BF16)
        ovT = ovT_ref[:, 0:cmp_rows]
        imp = _dot(ovT, hi) + _dot(ovT, lo)

        blk = lax.broadcasted_iota(jnp.int32, (nblk, Q_BLOCK), 0)
        cur = pos1 // SEL_BLOCK
        forced = (blk == 0) | (blk == cur) | (blk == cur - 1)
        v0 = jnp.where(forced, -1.0, jnp.where(blk * SEL_BLOCK <= pos1, imp, -1.0))
        n_piece = WIN_KEYS // Q_BLOCK
        kk = lax.broadcasted_iota(jnp.int32, (Q_BLOCK, QLANES), 0)
        qq = lax.broadcasted_iota(jnp.int32, (Q_BLOCK, QLANES), 1) % Q_BLOCK
        win = {"m": None, "acc": None}

        def piece_max(w):
            rows = slice(w * Q_BLOCK, (w + 1) * Q_BLOCK)
            sp = swin_ref[rows, :]
            if not steady:
                kpos = w0 + w * Q_BLOCK + kk
                sp = jnp.where(kpos <= pos, jnp.where(kpos > pos - WINDOW, sp, NEG_INF), NEG_INF)
            elif w == 0:
                sp = jnp.where(kk > qq, sp, NEG_INF)
            elif w == n_piece - 1:
                sp = jnp.where(kk <= qq, sp, NEG_INF)
            if not steady or w in (0, n_piece - 1):
                swin_ref[rows, :] = sp
            top = jnp.max(sp, axis=0, keepdims=True)
            win["m"] = top if win["m"] is None else jnp.maximum(win["m"], top)

        def piece_absorb(w):
            rows = slice(w * Q_BLOCK, (w + 1) * Q_BLOCK)
            p = jnp.exp2(swin_ref[rows, :] - win["m"]).astype(BF16)
            k0 = pl.multiple_of(w0 + w * Q_BLOCK, Q_BLOCK)
            pv = _dot(_with_ones(vwinT_ref[0, :, pl.ds(k0, Q_BLOCK)]), p)
            win["acc"] = pv if win["acc"] is None else win["acc"] + pv

        for w in range(n_piece):
            piece_max(w)
        left = v0
        pending = list(range(n_piece))
        for r in range(n_top):
            left = jnp.where(left == jnp.max(left, axis=0, keepdims=True), -2.0, left)
            if pending and r % 2 == 0:
                piece_absorb(pending.pop(0))
        for w in pending:
            piece_absorb(w)
        taken = left != v0
        n_taken = jnp.sum(jnp.where(taken, jnp.where(v0 >= 0.0, 1.0, 0.0), 0.0), axis=0, keepdims=True)

        def store_bias(left):
            bias = jnp.where(forced, 0.0, jnp.where(left != v0, 0.0, NEG_INF)).astype(BF16)
            bias = jnp.concatenate([bias] * NSA_REP, axis=1)
            if nblk < LANES:
                bias = jnp.concatenate([bias, jnp.zeros((LANES - nblk, QLANES), BF16)], axis=0)
            qaug_ref[LANES:2 * LANES, :] = bias

        store_bias(left)
        bias0 = jnp.where(forced[0:CHUNK_BLOCKS], 0.0,
                          jnp.where(left[0:CHUNK_BLOCKS] != v0[0:CHUNK_BLOCKS], 0.0, NEG_INF))
        bias0 = jnp.concatenate([bias0] * NSA_REP, axis=1)
        top = None
        for n in range(CHUNK_BLOCKS):
            rows = slice(n * SEL_BLOCK, (n + 1) * SEL_BLOCK)
            sb = s0_ref[rows, :] + bias0[n:n + 1, :]
            s0_ref[rows, :] = sb
            bt = jnp.max(sb, axis=0, keepdims=True)
            top = bt if top is None else jnp.maximum(top, bt)
        top0_ref[...] = top

        a_w = win["acc"]
        gate = gate_ref[0, 0, 0]
        part_ref[...] = gate[0:1] * o_cmp + gate[2:3] * (a_w[0:d] / a_w[d:d + 1])

        @pl.when(jnp.max(n_taken) > n_top)
        def _():
            val = v0
            for _ in range(n_top):
                top = jnp.max(val, axis=0, keepdims=True)
                first = jnp.min(jnp.where(val == top, blk, nblk), axis=0, keepdims=True)
                val = jnp.where(blk == first, -2.0, val)
            store_bias(val)
            scores_into(s0_ref, top0_ref, 0)

    def visible_cmp(block):
        return (block * Q_BLOCK + Q_BLOCK - CMP_LEN) // CMP_STRIDE + 1

    nq = ksel_ref.shape[1] // Q_BLOCK
    steady_from = WINDOW // Q_BLOCK
    first_rows = min(CMP_ROWS_STEP, ncmp)
    assert visible_cmp(steady_from - 1) <= first_rows
    pl.when(i < steady_from)(functools.partial(select_and_window, first_rows, False))
    lo = steady_from
    for rows in range(first_rows, ncmp + 1, CMP_ROWS_STEP):
        hi = next((blk_i for blk_i in range(lo, nq) if visible_cmp(blk_i) > rows), nq)
        if hi > lo:
            pl.when((i >= lo) & (i < hi))(functools.partial(select_and_window, rows, True))
            lo = hi
    assert lo == nq

    def absorb(s_ref, top_ref, j, diagonal):
        k0 = chunk_start(j)
        s = s_ref[...]
        if diagonal:
            kpos = k0 + lax.broadcasted_iota(jnp.int32, (SEL_CHUNK, QLANES), 0)
            s = jnp.where(kpos <= pos, s, NEG_INF)
            top = jnp.max(s, axis=0, keepdims=True)
        else:
            top = top_ref[...]
        m = m_ref[...]
        m_new = jnp.maximum(m, top)
        p = jnp.exp2(s - m_new).astype(BF16)
        acc_ref[...] = jnp.exp2(m - m_new) * acc_ref[...] + _dot(
            _with_ones(vselT_ref[0, :, pl.ds(k0, SEL_CHUNK)]), p)
        m_ref[...] = m_new

    last = t0 // SEL_CHUNK
    m_ref[...] = jnp.full(m_ref.shape, NEG_INF, F32)
    acc_ref[...] = jnp.zeros(acc_ref.shape, F32)

    def pair(t, carry):
        scores_into(s1_ref, top1_ref, 2 * t + 1)
        absorb(s0_ref, top0_ref, 2 * t, False)
        scores_into(s0_ref, top0_ref, 2 * t + 2)
        absorb(s1_ref, top1_ref, 2 * t + 1, False)
        return carry

    lax.fori_loop(0, last // 2, pair, 0)

    @pl.when(last % 2 == 1)
    def _():
        scores_into(s1_ref, top1_ref, last)
        absorb(s0_ref, top0_ref, last - 1, False)
        absorb(s1_ref, top1_ref, last, True)

    @pl.when(last % 2 == 0)
    def _():
        absorb(s0_ref, top0_ref, last, True)

    a_s = acc_ref[...]
    out = part_ref[...] + gate_ref[0, 0, 0][1:2] * (a_s[0:d] / a_s[d:d + 1])
    for r in range(NSA_REP):
        o_ref[0, r * HEAD_DIM:(r + 1) * HEAD_DIM, :] = out[:, r * Q_BLOCK:(r + 1) * Q_BLOCK].astype(BF16)


def _nsa_attention(qT, qrT, kcmp, vcmpT, ksel, vselT, kwin, vwinT, gates):
    b, _, s = qT.shape
    nq = s // Q_BLOCK
    nblk = s // SEL_BLOCK
    ncmp = kcmp.shape[2]
    rows = NSA_REP * HEAD_DIM
    assert nblk <= LANES and s % SEL_CHUNK == 0 and s >= WIN_KEYS

    gt = gates[:, :, :NSA_HEADS * N_GATES].reshape(b, nq, Q_BLOCK, NSA_KV_GROUPS, NSA_REP, N_GATES)
    gt = gt.transpose(0, 3, 1, 5, 4, 2).reshape(b, NSA_KV_GROUPS, nq, N_GATES, QLANES)

    cstart = np.arange(ncmp) * CMP_STRIDE
    sstart = np.arange(nblk) * SEL_BLOCK
    ovT = ((cstart[None, :] < sstart[:, None] + SEL_BLOCK) & (cstart[None, :] + CMP_LEN > sstart[:, None]))
    ovT = jnp.asarray(ovT, BF16)
    onehot = jnp.asarray(np.arange(s)[:, None] // SEL_BLOCK == np.arange(LANES)[None, :], BF16)

    q_spec = pl.BlockSpec((1, rows, Q_BLOCK), lambda bi, g, i: (bi, g, i))
    k_spec = pl.BlockSpec((1, s, LANES), lambda bi, g, i: (bi, 0, g))
    vT_spec = pl.BlockSpec((1, HEAD_DIM, s), lambda bi, g, i: (bi, g, 0))
    return pl.pallas_call(
        _attn_body,
        grid=(b, NSA_KV_GROUPS, nq),
        in_specs=[q_spec, q_spec,
                  pl.BlockSpec((1, 1, ncmp, LANES), lambda bi, g, i: (bi, g, 0, 0)),
                  pl.BlockSpec((1, 1, HEAD_DIM, ncmp), lambda bi, g, i: (bi, g, 0, 0)),
                  k_spec, vT_spec, k_spec, vT_spec,
                  pl.BlockSpec((1, 1, 1, N_GATES, QLANES), lambda bi, g, i: (bi, g, i, 0, 0)),
                  _const_spec((nblk, ncmp)), _const_spec((s, LANES))],
        out_specs=q_spec,
        out_shape=jax.ShapeDtypeStruct((b, NSA_HEADS * HEAD_DIM, s), BF16),
        scratch_shapes=[pltpu.VMEM((2 * LANES, QLANES), BF16),
                        pltpu.VMEM((SEL_CHUNK, QLANES), F32), pltpu.VMEM((SEL_CHUNK, QLANES), F32),
                        pltpu.VMEM((1, QLANES), F32), pltpu.VMEM((HEAD_DIM + ONES_ROWS, QLANES), F32),
                        pltpu.VMEM((HEAD_DIM, QLANES), F32), pltpu.VMEM((WIN_KEYS, QLANES), F32),
                        pltpu.VMEM((1, QLANES), F32), pltpu.VMEM((1, QLANES), F32)],
        compiler_params=_params(("arbitrary", "arbitrary", "arbitrary")),
        name="nsa_attention",
    )(qT, qrT, kcmp, vcmpT, ksel, vselT, kwin, vwinT, gt, ovT, onehot)


OUT_ROWS = 512


def _out_body(x_ref, aT_ref, w_ref, gpost_ref, o_ref):
    h = _dot_tn(aT_ref[0], w_ref[...])
    o_ref[0] = x_ref[0] + _rms(h, gpost_ref[...])


def _nsa_output(x, attnT, w_out, g_post):
    b, s, d = x.shape
    tm = min(OUT_ROWS, s)
    tile = pl.BlockSpec((1, tm, d), lambda bi, j: (bi, j, 0))
    return pl.pallas_call(
        _out_body,
        grid=(b, s // tm),
        in_specs=[tile, pl.BlockSpec((1, attnT.shape[1], tm), lambda bi, j: (bi, 0, j)),
                  _const_spec(w_out.shape), _const_spec((1, d))],
        out_specs=tile,
        out_shape=jax.ShapeDtypeStruct((b, s, d), F32),
        compiler_params=_params(("arbitrary", "arbitrary")),
        name="nsa_output",
    )(x, attnT, w_out.astype(BF16), g_post.reshape(1, d))


def _nsa_mixer(x, g_pre, g_post, w_in, b_gate, cmp_pos, cmp_w1, cmp_w2, w_out):
    qT, qrT, ksel, kwin, vselT, vwinT, kc_raw, vc_raw, gates = _nsa_project(x, g_pre, w_in, b_gate)
    kcmp, vcmpT = _nsa_compress(kc_raw, vc_raw, cmp_pos, cmp_w1, cmp_w2)
    attnT = _nsa_attention(qT, qrT, kcmp, vcmpT, ksel, vselT, kwin, vwinT, gates)
    return _nsa_output(x, attnT, w_out, g_post)


def kernel(x, mix_norm_pre, mix_norm_post, ffn_norm_pre, ffn_norm_post, ffn_w_gate, ffn_w_up, ffn_w_down,
           conv_w_pw1, conv_b_pw1, conv_w_dw, conv_b_dw, conv_ln_g, conv_ln_b, conv_w_pw2, conv_b_pw2,
           nsa_w_in, nsa_b_gate, nsa_cmp_pos, nsa_cmp_w1, nsa_cmp_w2, nsa_w_out):
    b, s, d = x.shape
    depth = mix_norm_pre.shape[0]
    n_mixers = 2

    def ffn(x, i, half):
        y = _ffn_half(x.reshape(b * s, d), ffn_norm_pre[i, half], ffn_norm_post[i, half],
                      ffn_w_gate[i, half], ffn_w_up[i, half], ffn_w_down[i, half])
        return y.reshape(b, s, d)

    for i in range(depth):
        x = ffn(x, i, 0)
        j = i // n_mixers
        if i % n_mixers == 0:
            x = _conv_mixer(x, mix_norm_pre[i], mix_norm_post[i], conv_w_pw1[j], conv_b_pw1[j], conv_w_dw[j],
                            conv_b_dw[j], conv_ln_g[j], conv_ln_b[j], conv_w_pw2[j], conv_b_pw2[j])
        else:
            x = _nsa_mixer(x, mix_norm_pre[i], mix_norm_post[i], nsa_w_in[j], nsa_b_gate[j], nsa_cmp_pos[j],
                           nsa_cmp_w1[j], nsa_cmp_w2[j], nsa_w_out[j])
        x = ffn(x, i, 1)
    return x
```

```python
import functools
import math

import jax
import jax.numpy as jnp
import numpy as np
from jax import lax
from jax.experimental import pallas as pl
from jax.experimental.pallas import tpu as pltpu

RMS_EPS = 1e-6
LN_EPS = 1e-5
FFN_RESIDUAL_WEIGHT = 0.5
CONV_WIDTH = 31
NSA_HEADS = 16
NSA_KV_GROUPS = 4
NSA_REP = NSA_HEADS // NSA_KV_GROUPS
HEAD_DIM = 64
ROT_HALF = HEAD_DIM // 8
ROPE_THETA = 500000.0
CMP_LEN = 32
CMP_STRIDE = 16
SEL_BLOCK = 64
N_SEL = 16
WINDOW = 512
Q_BLOCK = 128
N_GATES = 3
NEG_INF = -1e30

LANES = 128
SUBLANES = 8
V7X_VMEM_LIMIT = 56 * 1024 * 1024

BF16 = jnp.bfloat16
F32 = jnp.float32


def _params(sem):
    return pltpu.CompilerParams(dimension_semantics=sem, vmem_limit_bytes=V7X_VMEM_LIMIT)


def _const_spec(shape):
    nd = len(shape)
    return pl.BlockSpec(shape, lambda *_: (0,) * nd, pipeline_mode=pl.Buffered(1))


def _rms(x, gain):
    return x * lax.rsqrt(jnp.mean(x * x, axis=-1, keepdims=True) + RMS_EPS) * gain


def _sigmoid(x):
    return 1.0 / (1.0 + jnp.exp(-x))


def _dot(a, b):
    return jnp.dot(a, b, preferred_element_type=F32)


def _dot_nt(a, b):
    return lax.dot_general(a, b, (((1,), (1,)), ((), ())), preferred_element_type=F32)


def _dot_tn(a, b):
    return lax.dot_general(a, b, (((0,), (0,)), ((), ())), preferred_element_type=F32)


FFN_ROWS = 512
FFN_CHUNK = 512


def _swiglu_half_step(x, gpre_ref, gpost_ref, wg_ref, wu_ref, wd_ref, o_ref):
    h = _rms(x, gpre_ref[...]).astype(BF16)
    d_ff = wd_ref.shape[0]
    acc = jnp.zeros(x.shape, F32)
    for c0 in range(0, d_ff, FFN_CHUNK):
        c1 = min(c0 + FFN_CHUNK, d_ff)
        g = _dot(h, wg_ref[:, c0:c1])
        u = _dot(h, wu_ref[:, c0:c1])
        a = (g * _sigmoid(g) * u).astype(BF16)
        acc = acc + _dot(a, wd_ref[c0:c1, :])
    o_ref[...] = x + FFN_RESIDUAL_WEIGHT * _rms(acc, gpost_ref[...])


def _ffn_body(x_ref, *refs):
    _swiglu_half_step(x_ref[...], *refs)


def _attn_out_ffn_body(x_ref, aT_ref, wo_ref, gmix_ref, *refs):
    x = x_ref[...] + _rms(_dot_tn(aT_ref[0], wo_ref[...]), gmix_ref[...])
    _swiglu_half_step(x, *refs)


def _ffn_half(x2, g_pre, g_post, w_gate, w_up, w_down, attn_out=None):
    t, d = x2.shape
    f = w_gate.shape[1]
    tm = min(FFN_ROWS, t)
    row = pl.BlockSpec((tm, d), lambda i: (i, 0))
    ffn_specs = [_const_spec((1, d)), _const_spec((1, d)),
                 _const_spec((d, f)), _const_spec((d, f)), _const_spec((f, d))]
    ffn_args = (g_pre.reshape(1, d), g_post.reshape(1, d),
                w_gate.astype(BF16), w_up.astype(BF16), w_down.astype(BF16))
    if attn_out is None:
        body, specs, args = _ffn_body, [row], (x2,)
    else:
        attnT, w_out, g_mix = attn_out
        tiles = attnT.shape[2] // tm
        body = _attn_out_ffn_body
        specs = [row, pl.BlockSpec((1, attnT.shape[1], tm), lambda i: (i // tiles, 0, i % tiles)),
                 _const_spec(w_out.shape), _const_spec((1, d))]
        args = (x2, attnT, w_out.astype(BF16), g_mix.reshape(1, d))
    return pl.pallas_call(
        body,
        grid=(t // tm,),
        in_specs=specs + ffn_specs,
        out_specs=row,
        out_shape=jax.ShapeDtypeStruct((t, d), F32),
        compiler_params=_params(("arbitrary",)),
        name="ffn_half",
    )(*args, *ffn_args)


CONV_ROWS = 256
CONV_HALO = 32
CONV_STRIP = 32


def _conv_body(x_ref, gpre_ref, w1_ref, b1_ref, wdw_ref, bdw_ref, lng_ref, lnb_ref,
               w2_ref, b2_ref, gpost_ref, o_ref, buf_ref, dw_ref, shift_ref):
    ts, d = x_ref.shape[1], x_ref.shape[2]

    @pl.when(pl.program_id(1) == 0)
    def _():
        buf_ref[0:CONV_HALO, :] = jnp.zeros((CONV_HALO, d), F32)

    x = x_ref[0]
    h = _rms(x, gpre_ref[...]).astype(BF16)
    p = _dot(h, w1_ref[...]) + b1_ref[...]
    buf_ref[CONV_HALO:CONV_HALO + ts, :] = p[:, :d] * _sigmoid(p[:, d:])

    base = CONV_HALO - (CONV_WIDTH - 1)
    span = shift_ref.shape[1]
    for b in range(1, SUBLANES):
        shift_ref[b - 1] = buf_ref[b:b + span, :]

    for r0 in range(0, ts, CONV_STRIP):
        acc = jnp.zeros((CONV_STRIP, d), F32)
        for k in range(CONV_WIDTH):
            a, b = divmod(base + k, SUBLANES)
            lo = r0 + a * SUBLANES
            win = buf_ref[lo:lo + CONV_STRIP, :] if b == 0 else shift_ref[b - 1, lo:lo + CONV_STRIP, :]
            acc = acc + wdw_ref[k:k + 1, :] * win
        dw_ref[r0:r0 + CONV_STRIP, :] = acc
    buf_ref[0:CONV_HALO, :] = buf_ref[ts:ts + CONV_HALO, :]

    c = dw_ref[...] + bdw_ref[...]
    mu = jnp.mean(c, axis=-1, keepdims=True)
    cc = c - mu
    var = jnp.mean(cc * cc, axis=-1, keepdims=True)
    y = cc * lax.rsqrt(var + LN_EPS) * lng_ref[...] + lnb_ref[...]
    y = (y * _sigmoid(y)).astype(BF16)
    out = _dot(y, w2_ref[...]) + b2_ref[...]
    o_ref[0] = x + _rms(out, gpost_ref[...])


def _conv_mixer(x, g_pre, g_post, w_pw1, b_pw1, w_dw, b_dw, ln_g, ln_b, w_pw2, b_pw2):
    b, s, d = x.shape
    ts = min(CONV_ROWS, s)
    tile = pl.BlockSpec((1, ts, d), lambda bi, j: (bi, j, 0))
    vec = lambda n: _const_spec((1, n))
    return pl.pallas_call(
        _conv_body,
        grid=(b, s // ts),
        in_specs=[tile, vec(d), _const_spec((d, 2 * d)), vec(2 * d), _const_spec((CONV_WIDTH, d)),
                  vec(d), vec(d), vec(d), _const_spec((d, d)), vec(d), vec(d)],
        out_specs=tile,
        out_shape=jax.ShapeDtypeStruct((b, s, d), F32),
        scratch_shapes=[pltpu.VMEM((ts + CONV_HALO, d), F32), pltpu.VMEM((ts, d), F32),
                        pltpu.VMEM((SUBLANES - 1, ts + (CONV_HALO - 1) // SUBLANES * SUBLANES, d), F32)],
        compiler_params=_params(("arbitrary", "arbitrary")),
        name="conv_mixer",
    )(x, g_pre.reshape(1, d), w_pw1.astype(BF16), b_pw1.reshape(1, 2 * d), w_dw,
      b_dw.reshape(1, d), ln_g.reshape(1, d), ln_b.reshape(1, d), w_pw2.astype(BF16),
      b_pw2.reshape(1, d), g_post.reshape(1, d))


PROJ_ROWS = 512
Q_WIDTH = NSA_HEADS * HEAD_DIM
KV_WIDTH = NSA_KV_GROUPS * HEAD_DIM
KPAD_WIDTH = NSA_KV_GROUPS * LANES
GATE_PAD = LANES
Q_SCALE = HEAD_DIM ** -0.5 * math.log2(math.e)


def _rope_tables(s):
    pos = jnp.arange(s, dtype=F32)
    inv_freq = ROPE_THETA ** (-jnp.arange(0, 2 * ROT_HALF, 2, dtype=F32) / (2 * ROT_HALF))
    ang = pos[:, None] * inv_freq[None, :]
    return jnp.cos(ang), jnp.sin(ang)


def _proj_body(x_ref, gpre_ref, wtok_ref, wtr_ref, bg_ref, cosT_ref, sinT_ref, ck_ref, s1_ref, s2_ref,
               qT_ref, qrT_ref, ksel_ref, kwin_ref, vselT_ref, vwinT_ref, kc_ref, vc_ref, gate_ref):
    tm = x_ref.shape[1]
    h = _rms(x_ref[0], gpre_ref[...]).astype(BF16)
    tok = _dot(h, wtok_ref[...])
    tr = _dot_nt(wtr_ref[...], h)

    q = tr[0:Q_WIDTH] * Q_SCALE
    qT_ref[0] = q.astype(BF16)
    q3 = q.reshape(NSA_HEADS, HEAD_DIM, tm)
    cos, sin = cosT_ref[...], sinT_ref[...]
    x1, x2 = q3[:, 0:ROT_HALF], q3[:, ROT_HALF:2 * ROT_HALF]
    qr = jnp.concatenate([x1 * cos - x2 * sin, x2 * cos + x1 * sin, q3[:, 2 * ROT_HALF:]], axis=1)
    qrT_ref[0] = qr.reshape(Q_WIDTH, tm).astype(BF16)
    vselT_ref[0] = tr[Q_WIDTH:Q_WIDTH + KV_WIDTH].astype(BF16)
    vwinT_ref[0] = tr[Q_WIDTH + KV_WIDTH:Q_WIDTH + 2 * KV_WIDTH].astype(BF16)

    ck, s1, s2 = ck_ref[...], s1_ref[...], s2_ref[...]
    for out_ref, base in ((ksel_ref, 0), (kwin_ref, KPAD_WIDTH)):
        for g in range(NSA_KV_GROUPS):
            k = tok[:, base + g * LANES:base + (g + 1) * LANES]
            kr = k * ck + pltpu.roll(k, ROT_HALF, 1) * s1 + pltpu.roll(k, LANES - ROT_HALF, 1) * s2
            out_ref[0, :, g * LANES:(g + 1) * LANES] = kr.astype(BF16)
    c0 = 2 * KPAD_WIDTH
    kc_ref[0] = tok[:, c0:c0 + KV_WIDTH]
    vc_ref[0] = tok[:, c0 + KV_WIDTH:c0 + 2 * KV_WIDTH]
    gate_ref[0] = _sigmoid(tok[:, c0 + 2 * KV_WIDTH:] + bg_ref[...])


def _nsa_project(x, g_pre, w_in, b_gate):
    b, s, d = x.shape
    tm = min(PROJ_ROWS, s)
    n_gate = NSA_HEADS * N_GATES
    w_q = w_in[:, :Q_WIDTH]
    w_kv = w_in[:, Q_WIDTH:Q_WIDTH + 6 * KV_WIDTH].reshape(d, 6, NSA_KV_GROUPS, HEAD_DIM)
    w_gate = w_in[:, Q_WIDTH + 6 * KV_WIDTH:]

    def padded(w):
        return jnp.pad(w, ((0, 0), (0, 0), (0, LANES - HEAD_DIM))).reshape(d, KPAD_WIDTH)

    w_tok = jnp.concatenate(
        [padded(w_kv[:, 2]), padded(w_kv[:, 4]), w_kv[:, 0].reshape(d, KV_WIDTH),
         w_kv[:, 1].reshape(d, KV_WIDTH), jnp.pad(w_gate, ((0, 0), (0, GATE_PAD - n_gate)))],
        axis=1).astype(BF16)
    w_tr = jnp.concatenate(
        [w_q, w_kv[:, 3].reshape(d, KV_WIDTH), w_kv[:, 5].reshape(d, KV_WIDTH)], axis=1).T.astype(BF16)
    bg = jnp.pad(b_gate, (0, GATE_PAD - n_gate)).reshape(1, GATE_PAD)

    cos, sin = _rope_tables(s)
    zeros = jnp.zeros_like(sin)
    pad_to = lambda parts, fill: jnp.concatenate(
        parts + [jnp.full((s, LANES - 2 * ROT_HALF), fill, F32)], axis=1)
    ck = pad_to([cos, cos], 1.0)
    s1 = pad_to([zeros, sin], 0.0)
    s2 = pad_to([-sin, zeros], 0.0)

    ntok, ntr = w_tok.shape[1], w_tr.shape[0]
    tok_major = lambda w: pl.BlockSpec((1, tm, w), lambda bi, j: (bi, j, 0))
    tr_major = lambda r: pl.BlockSpec((1, r, tm), lambda bi, j: (bi, 0, j))
    return pl.pallas_call(
        _proj_body,
        grid=(b, s // tm),
        in_specs=[tok_major(d), _const_spec((1, d)), _const_spec((d, ntok)), _const_spec((ntr, d)),
                  _const_spec((1, GATE_PAD)),
                  pl.BlockSpec((ROT_HALF, tm), lambda bi, j: (0, j)),
                  pl.BlockSpec((ROT_HALF, tm), lambda bi, j: (0, j)),
                  pl.BlockSpec((tm, LANES), lambda bi, j: (j, 0)),
                  pl.BlockSpec((tm, LANES), lambda bi, j: (j, 0)),
                  pl.BlockSpec((tm, LANES), lambda bi, j: (j, 0))],
        out_specs=[tr_major(Q_WIDTH), tr_major(Q_WIDTH), tok_major(KPAD_WIDTH), tok_major(KPAD_WIDTH),
                   tr_major(KV_WIDTH), tr_major(KV_WIDTH), tok_major(KV_WIDTH), tok_major(KV_WIDTH),
                   tok_major(GATE_PAD)],
        out_shape=[jax.ShapeDtypeStruct((b, Q_WIDTH, s), BF16), jax.ShapeDtypeStruct((b, Q_WIDTH, s), BF16),
                   jax.ShapeDtypeStruct((b, s, KPAD_WIDTH), BF16), jax.ShapeDtypeStruct((b, s, KPAD_WIDTH), BF16),
                   jax.ShapeDtypeStruct((b, KV_WIDTH, s), BF16), jax.ShapeDtypeStruct((b, KV_WIDTH, s), BF16),
                   jax.ShapeDtypeStruct((b, s, KV_WIDTH), F32), jax.ShapeDtypeStruct((b, s, KV_WIDTH), F32),
                   jax.ShapeDtypeStruct((b, s, GATE_PAD), F32)],
        compiler_params=_params(("arbitrary", "arbitrary")),
        name="nsa_project",
    )(x, g_pre.reshape(1, d), w_tok, w_tr, bg, cos.T, sin.T, ck, s1, s2)


def _compress_body(rk_ref, rv_ref, pos_ref, w1_ref, w2k_ref, w2vT_ref, kc_ref, vcT_ref):
    nrow = rk_ref.shape[2]
    half = rk_ref.shape[3]

    def hidden(r, t):
        top = _dot((r + pos_ref[t, 0:1, :]).astype(BF16), w1_ref[t, 0:half, :])
        bot = _dot((r + pos_ref[t, 1:2, :]).astype(BF16), w1_ref[t, half:2 * half, :])
        hid = top + pltpu.roll(bot, nrow - 1, 0)
        return (hid * _sigmoid(hid)).astype(BF16)

    kc_ref[0, 0] = _dot(hidden(rk_ref[0, 0], 0), w2k_ref[...]).astype(BF16)
    vcT_ref[0, 0] = _dot_nt(w2vT_ref[...], hidden(rv_ref[0, 0], 1)).astype(BF16)


def _nsa_compress(kc_raw, vc_raw, cmp_pos, cmp_w1, cmp_w2):
    b, s, _ = kc_raw.shape
    nrow = s // CMP_STRIDE
    half = CMP_STRIDE * HEAD_DIM
    hid = cmp_w1.shape[-1]

    def rows(raw):
        r = raw.reshape(b, nrow, CMP_STRIDE, NSA_KV_GROUPS, HEAD_DIM)
        return r.transpose(0, 3, 1, 2, 4).reshape(b, NSA_KV_GROUPS, nrow, half)

    pos = cmp_pos.reshape(2, 2, half)
    w2k = jnp.pad(cmp_w2[0], ((0, 0), (0, LANES - HEAD_DIM))).astype(BF16)
    w2vT = cmp_w2[1].T.astype(BF16)
    blk = pl.BlockSpec((1, 1, nrow, half), lambda bi, g: (bi, g, 0, 0))
    return pl.pallas_call(
        _compress_body,
        grid=(b, NSA_KV_GROUPS),
        in_specs=[blk, blk, _const_spec((2, 2, half)), _const_spec((2, 2 * half, hid)),
                  _const_spec((hid, LANES)), _const_spec((HEAD_DIM, hid))],
        out_specs=[pl.BlockSpec((1, 1, nrow, LANES), lambda bi, g: (bi, g, 0, 0)),
                   pl.BlockSpec((1, 1, HEAD_DIM, nrow), lambda bi, g: (bi, g, 0, 0))],
        out_shape=[jax.ShapeDtypeStruct((b, NSA_KV_GROUPS, nrow, LANES), BF16),
                   jax.ShapeDtypeStruct((b, NSA_KV_GROUPS, HEAD_DIM, nrow), BF16)],
        compiler_params=_params(("arbitrary", "arbitrary")),
        name="nsa_compress",
    )(rows(kc_raw), rows(vc_raw), pos, cmp_w1.astype(BF16), w2k, w2vT)


SEL_CHUNK = 512
CHUNK_BLOCKS = SEL_CHUNK // SEL_BLOCK
WIN_KEYS = WINDOW + Q_BLOCK
QLANES = NSA_REP * Q_BLOCK
N_FORCED = 3
ONES_ROWS = 16
CMP_ROWS_STEP = 128
MASKED_MAX_FLOOR = -1e29


def _with_ones(vT):
    return jnp.concatenate([vT, jnp.ones((ONES_ROWS, vT.shape[1]), BF16)], axis=0)


def _attn_body(qT_ref, qrT_ref, kc_ref, vcT_ref, ksel_ref, vselT_ref, kwin_ref, vwinT_ref, gate_ref,
               ovT_ref, oh_ref, o_ref, qaug_ref, s0_ref, s1_ref, m_ref, acc_ref, part_ref, swin_ref,
               top0_ref, top1_ref):
    i = pl.program_id(2)
    t0 = i * Q_BLOCK
    nblk = ovT_ref.shape[0]
    ncmp = kc_ref.shape[2]
    n_top = min(N_SEL, nblk) - N_FORCED

    def lanes_of_heads(ref):
        parts = [ref[0, r * HEAD_DIM:(r + 1) * HEAD_DIM, :] for r in range(NSA_REP)]
        return jnp.concatenate(
            [jnp.concatenate(parts, axis=1), jnp.zeros((LANES - HEAD_DIM, QLANES), BF16)], axis=0)

    qT = lanes_of_heads(qT_ref)
    qrT = lanes_of_heads(qrT_ref)
    pos1 = t0 + lax.broadcasted_iota(jnp.int32, (1, Q_BLOCK), 1)
    pos = jnp.concatenate([pos1] * NSA_REP, axis=1)

    d = HEAD_DIM
    qaug_ref[0:LANES, :] = qrT

    def chunk_start(j):
        return pl.multiple_of(j * SEL_CHUNK, SEL_CHUNK)

    def scores_into(s_ref, top_ref, j):
        k0 = chunk_start(j)
        ka = jnp.concatenate([ksel_ref[0, pl.ds(k0, SEL_CHUNK), :], oh_ref[pl.ds(k0, SEL_CHUNK), :]], axis=1)
        s = _dot(ka, qaug_ref[...])
        s_ref[...] = s
        top_ref[...] = jnp.max(s, axis=0, keepdims=True)

    def select_and_window(cmp_rows, steady):
        s = _dot(kc_ref[0, 0, 0:cmp_rows, :], qT)
        w0 = pl.multiple_of(t0 - WINDOW if steady else jnp.maximum(t0 - WINDOW, 0), Q_BLOCK)
        swin_ref[...] = _dot(kwin_ref[0, pl.ds(w0, WIN_KEYS), :], qrT)
        s0_ref[...] = _dot(ksel_ref[0, 0:SEL_CHUNK, :], qrT)
        cend = lax.broadcasted_iota(jnp.int32, (cmp_rows, QLANES), 0) * CMP_STRIDE + (CMP_LEN - 1)
        s = jnp.where(cend <= pos, s, NEG_INF)
        m = jnp.maximum(jnp.max(s, axis=0, keepdims=True), MASKED_MAX_FLOOR)
        p = jnp.exp2(s - m)
        inv_l = 1.0 / jnp.maximum(jnp.sum(p, axis=0, keepdims=True), 1e-30)
        o_cmp = _dot(vcT_ref[0, 0, :, 0:cmp_rows], p.astype(BF16)) * inv_l

        pn = p * inv_l
        ph = pn[:, 0:Q_BLOCK]
        for r in range(1, NSA_REP):
            ph = ph + pn[:, r * Q_BLOCK:(r + 1) * Q_BLOCK]
        hi = ph.astype(BF16)
        lo = (ph - hi.astype(F32)).astype(BF16)
        ovT = ovT_ref[:, 0:cmp_rows]
        imp = _dot(ovT, hi) + _dot(ovT, lo)

        blk = lax.broadcasted_iota(jnp.int32, (nblk, Q_BLOCK), 0)
        cur = pos1 // SEL_BLOCK
        forced = (blk == 0) | (blk == cur) | (blk == cur - 1)
        v0 = jnp.where(forced, -1.0, jnp.where(blk * SEL_BLOCK <= pos1, imp, -1.0))
        n_piece = WIN_KEYS // Q_BLOCK
        kk = lax.broadcasted_iota(jnp.int32, (Q_BLOCK, QLANES), 0)
        qq = lax.broadcasted_iota(jnp.int32, (Q_BLOCK, QLANES), 1) % Q_BLOCK
        win = {"m": None, "acc": None}

        def piece_max(w):
            rows = slice(w * Q_BLOCK, (w + 1) * Q_BLOCK)
            sp = swin_ref[rows, :]
            if not steady:
                kpos = w0 + w * Q_BLOCK + kk
                sp = jnp.where(kpos <= pos, jnp.where(kpos > pos - WINDOW, sp, NEG_INF), NEG_INF)
            elif w == 0:
                sp = jnp.where(kk > qq, sp, NEG_INF)
            elif w == n_piece - 1:
                sp = jnp.where(kk <= qq, sp, NEG_INF)
            if not steady or w in (0, n_piece - 1):
                swin_ref[rows, :] = sp
            top = jnp.max(sp, axis=0, keepdims=True)
            win["m"] = top if win["m"] is None else jnp.maximum(win["m"], top)

        def piece_absorb(w):
            rows = slice(w * Q_BLOCK, (w + 1) * Q_BLOCK)
            p = jnp.exp2(swin_ref[rows, :] - win["m"]).astype(BF16)
            k0 = pl.multiple_of(w0 + w * Q_BLOCK, Q_BLOCK)
            pv = _dot(_with_ones(vwinT_ref[0, :, pl.ds(k0, Q_BLOCK)]), p)
            win["acc"] = pv if win["acc"] is None else win["acc"] + pv

        for w in range(n_piece):
            piece_max(w)
        left = v0
        pending = list(range(n_piece))
        for r in range(n_top):
            left = jnp.where(left == jnp.max(left, axis=0, keepdims=True), -2.0, left)
            if pending and r % 2 == 0:
                piece_absorb(pending.pop(0))
        for w in pending:
            piece_absorb(w)
        taken = left != v0
        n_taken = jnp.sum(jnp.where(taken, jnp.where(v0 >= 0.0, 1.0, 0.0), 0.0), axis=0, keepdims=True)

        def store_bias(left):
            bias = jnp.where(forced, 0.0, jnp.where(left != v0, 0.0, NEG_INF)).astype(BF16)
            bias = jnp.concatenate([bias] * NSA_REP, axis=1)
            if nblk < LANES:
                bias = jnp.concatenate([bias, jnp.zeros((LANES - nblk, QLANES), BF16)], axis=0)
            qaug_ref[LANES:2 * LANES, :] = bias

        store_bias(left)
        bias0 = jnp.where(forced[0:CHUNK_BLOCKS], 0.0,
                          jnp.where(left[0:CHUNK_BLOCKS] != v0[0:CHUNK_BLOCKS], 0.0, NEG_INF))
        bias0 = jnp.concatenate([bias0] * NSA_REP, axis=1)
        top = None
        for n in range(CHUNK_BLOCKS):
            rows = slice(n * SEL_BLOCK, (n + 1) * SEL_BLOCK)
            sb = s0_ref[rows, :] + bias0[n:n + 1, :]
            s0_ref[rows, :] = sb
            bt = jnp.max(sb, axis=0, keepdims=True)
            top = bt if top is None else jnp.maximum(top, bt)
        top0_ref[...] = top

        a_w = win["acc"]
        gate = gate_ref[0, 0, 0]
        part_ref[...] = gate[0:1] * o_cmp + gate[2:3] * (a_w[0:d] / a_w[d:d + 1])

        @pl.when(jnp.max(n_taken) > n_top)
        def _():
            val = v0
            for _ in range(n_top):
                top = jnp.max(val, axis=0, keepdims=True)
                first = jnp.min(jnp.where(val == top, blk, nblk), axis=0, keepdims=True)
                val = jnp.where(blk == first, -2.0, val)
            store_bias(val)
            scores_into(s0_ref, top0_ref, 0)

    def visible_cmp(block):
        return (block * Q_BLOCK + Q_BLOCK - CMP_LEN) // CMP_STRIDE + 1

    nq = ksel_ref.shape[1] // Q_BLOCK
    steady_from = WINDOW // Q_BLOCK
    first_rows = min(CMP_ROWS_STEP, ncmp)
    assert visible_cmp(steady_from - 1) <= first_rows
    pl.when(i < steady_from)(functools.partial(select_and_window, first_rows, False))
    lo = steady_from
    for rows in range(first_rows, ncmp + 1, CMP_ROWS_STEP):
        hi = next((blk_i for blk_i in range(lo, nq) if visible_cmp(blk_i) > rows), nq)
        if hi > lo:
            pl.when((i >= lo) & (i < hi))(functools.partial(select_and_window, rows, True))
            lo = hi
    assert lo == nq

    def absorb(s_ref, top_ref, j, diagonal):
        k0 = chunk_start(j)
        s = s_ref[...]
        if diagonal:
            kpos = k0 + lax.broadcasted_iota(jnp.int32, (SEL_CHUNK, QLANES), 0)
            s = jnp.where(kpos <= pos, s, NEG_INF)
            top = jnp.max(s, axis=0, keepdims=True)
        else:
            top = top_ref[...]
        m = m_ref[...]
        m_new = jnp.maximum(m, top)
        p = jnp.exp2(s - m_new).astype(BF16)
        acc_ref[...] = jnp.exp2(m - m_new) * acc_ref[...] + _dot(
            _with_ones(vselT_ref[0, :, pl.ds(k0, SEL_CHUNK)]), p)
        m_ref[...] = m_new

    last = t0 // SEL_CHUNK
    m_ref[...] = jnp.full(m_ref.shape, NEG_INF, F32)
    acc_ref[...] = jnp.zeros(acc_ref.shape, F32)

    def pair(t):
        scores_into(s1_ref, top1_ref, 2 * t + 1)
        absorb(s0_ref, top0_ref, 2 * t, False)
        scores_into(s0_ref, top0_ref, 2 * t + 2)
        absorb(s1_ref, top1_ref, 2 * t + 1, False)

    def two_pairs(u, carry):
        pair(2 * u)
        pair(2 * u + 1)
        return carry

    n_pairs = last // 2
    lax.fori_loop(0, n_pairs // 2, two_pairs, 0)

    @pl.when(n_pairs % 2 == 1)
    def _():
        pair(n_pairs - 1)

    @pl.when(last % 2 == 1)
    def _():
        scores_into(s1_ref, top1_ref, last)
        absorb(s0_ref, top0_ref, last - 1, False)
        absorb(s1_ref, top1_ref, last, True)

    @pl.when(last % 2 == 0)
    def _():
        absorb(s0_ref, top0_ref, last, True)

    a_s = acc_ref[...]
    out = part_ref[...] + gate_ref[0, 0, 0][1:2] * (a_s[0:d] / a_s[d:d + 1])
    for r in range(NSA_REP):
        o_ref[0, r * HEAD_DIM:(r + 1) * HEAD_DIM, :] = out[:, r * Q_BLOCK:(r + 1) * Q_BLOCK].astype(BF16)


def _nsa_attention(qT, qrT, kcmp, vcmpT, ksel, vselT, kwin, vwinT, gates):
    b, _, s = qT.shape
    nq = s // Q_BLOCK
    nblk = s // SEL_BLOCK
    ncmp = kcmp.shape[2]
    rows = NSA_REP * HEAD_DIM
    assert nblk <= LANES and s % SEL_CHUNK == 0 and s >= WIN_KEYS

    gt = gates[:, :, :NSA_HEADS * N_GATES].reshape(b, nq, Q_BLOCK, NSA_KV_GROUPS, NSA_REP, N_GATES)
    gt = gt.transpose(0, 3, 1, 5, 4, 2).reshape(b, NSA_KV_GROUPS, nq, N_GATES, QLANES)

    cstart = np.arange(ncmp) * CMP_STRIDE
    sstart = np.arange(nblk) * SEL_BLOCK
    ovT = ((cstart[None, :] < sstart[:, None] + SEL_BLOCK) & (cstart[None, :] + CMP_LEN > sstart[:, None]))
    ovT = jnp.asarray(ovT, BF16)
    onehot = jnp.asarray(np.arange(s)[:, None] // SEL_BLOCK == np.arange(LANES)[None, :], BF16)

    q_spec = pl.BlockSpec((1, rows, Q_BLOCK), lambda bi, g, i: (bi, g, i))
    k_spec = pl.BlockSpec((1, s, LANES), lambda bi, g, i: (bi, 0, g))
    vT_spec = pl.BlockSpec((1, HEAD_DIM, s), lambda bi, g, i: (bi, g, 0))
    return pl.pallas_call(
        _attn_body,
        grid=(b, NSA_KV_GROUPS, nq),
        in_specs=[q_spec, q_spec,
                  pl.BlockSpec((1, 1, ncmp, LANES), lambda bi, g, i: (bi, g, 0, 0)),
                  pl.BlockSpec((1, 1, HEAD_DIM, ncmp), lambda bi, g, i: (bi, g, 0, 0)),
                  k_spec, vT_spec, k_spec, vT_spec,
                  pl.BlockSpec((1, 1, 1, N_GATES, QLANES), lambda bi, g, i: (bi, g, i, 0, 0)),
                  _const_spec((nblk, ncmp)), _const_spec((s, LANES))],
        out_specs=q_spec,
        out_shape=jax.ShapeDtypeStruct((b, NSA_HEADS * HEAD_DIM, s), BF16),
        scratch_shapes=[pltpu.VMEM((2 * LANES, QLANES), BF16),
                        pltpu.VMEM((SEL_CHUNK, QLANES), F32), pltpu.VMEM((SEL_CHUNK, QLANES), F32),
                        pltpu.VMEM((1, QLANES), F32), pltpu.VMEM((HEAD_DIM + ONES_ROWS, QLANES), F32),
                        pltpu.VMEM((HEAD_DIM, QLANES), F32), pltpu.VMEM((WIN_KEYS, QLANES), F32),
                        pltpu.VMEM((1, QLANES), F32), pltpu.VMEM((1, QLANES), F32)],
        compiler_params=_params(("arbitrary", "arbitrary", "arbitrary")),
        name="nsa_attention",
    )(qT, qrT, kcmp, vcmpT, ksel, vselT, kwin, vwinT, gt, ovT, onehot)


def _nsa_mixer(x, g_pre, w_in, b_gate, cmp_pos, cmp_w1, cmp_w2):
    qT, qrT, ksel, kwin, vselT, vwinT, kc_raw, vc_raw, gates = _nsa_project(x, g_pre, w_in, b_gate)
    kcmp, vcmpT = _nsa_compress(kc_raw, vc_raw, cmp_pos, cmp_w1, cmp_w2)
    return _nsa_attention(qT, qrT, kcmp, vcmpT, ksel, vselT, kwin, vwinT, gates)


def kernel(x, mix_norm_pre, mix_norm_post, ffn_norm_pre, ffn_norm_post, ffn_w_gate, ffn_w_up, ffn_w_down,
           conv_w_pw1, conv_b_pw1, conv_w_dw, conv_b_dw, conv_ln_g, conv_ln_b, conv_w_pw2, conv_b_pw2,
           nsa_w_in, nsa_b_gate, nsa_cmp_pos, nsa_cmp_w1, nsa_cmp_w2, nsa_w_out):
    b, s, d = x.shape
    depth = mix_norm_pre.shape[0]
    n_mixers = 2

    def ffn(x, i, half, attn_out=None):
        y = _ffn_half(x.reshape(b * s, d), ffn_norm_pre[i, half], ffn_norm_post[i, half],
                      ffn_w_gate[i, half], ffn_w_up[i, half], ffn_w_down[i, half], attn_out)
        return y.reshape(b, s, d)

    for i in range(depth):
        x = ffn(x, i, 0)
        j = i // n_mixers
        if i % n_mixers == 0:
            x = _conv_mixer(x, mix_norm_pre[i], mix_norm_post[i], conv_w_pw1[j], conv_b_pw1[j], conv_w_dw[j],
                            conv_b_dw[j], conv_ln_g[j], conv_ln_b[j], conv_w_pw2[j], conv_b_pw2[j])
            x = ffn(x, i, 1)
        else:
            attnT = _nsa_mixer(x, mix_norm_pre[i], nsa_w_in[j], nsa_b_gate[j], nsa_cmp_pos[j],
                               nsa_cmp_w1[j], nsa_cmp_w2[j])
            x = ffn(x, i, 1, (attnT, nsa_w_out[j], mix_norm_post[i]))
    return x
```

```python
import functools
import math

import jax
import jax.numpy as jnp
import numpy as np
from jax import lax
from jax.experimental import pallas as pl
from jax.experimental.pallas import tpu as pltpu

RMS_EPS = 1e-6
LN_EPS = 1e-5
FFN_RESIDUAL_WEIGHT = 0.5
CONV_WIDTH = 31
NSA_HEADS = 16
NSA_KV_GROUPS = 4
NSA_REP = NSA_HEADS // NSA_KV_GROUPS
HEAD_DIM = 64
ROT_HALF = HEAD_DIM // 8
ROPE_THETA = 500000.0
CMP_LEN = 32
CMP_STRIDE = 16
SEL_BLOCK = 64
N_SEL = 16
WINDOW = 512
Q_BLOCK = 128
N_GATES = 3
NEG_INF = -1e30

LANES = 128
SUBLANES = 8
V7X_VMEM_LIMIT = 56 * 1024 * 1024

BF16 = jnp.bfloat16
F32 = jnp.float32


def _params(sem):
    return pltpu.CompilerParams(dimension_semantics=sem, vmem_limit_bytes=V7X_VMEM_LIMIT)


def _const_spec(shape):
    nd = len(shape)
    return pl.BlockSpec(shape, lambda *_: (0,) * nd, pipeline_mode=pl.Buffered(1))


def _rms(x, gain):
    return x * lax.rsqrt(jnp.mean(x * x, axis=-1, keepdims=True) + RMS_EPS) * gain


def _sigmoid(x):
    return 1.0 / (1.0 + jnp.exp(-x))


def _dot(a, b):
    return jnp.dot(a, b, preferred_element_type=F32)


def _dot_nt(a, b):
    return lax.dot_general(a, b, (((1,), (1,)), ((), ())), preferred_element_type=F32)


def _dot_tn(a, b):
    return lax.dot_general(a, b, (((0,), (0,)), ((), ())), preferred_element_type=F32)


FFN_ROWS = 512
FFN_CHUNK = 512


def _swiglu_half_step(x, gpre_ref, gpost_ref, wg_ref, wu_ref, wd_ref, o_ref):
    h = _rms(x, gpre_ref[...]).astype(BF16)
    d_ff = wd_ref.shape[0]
    acc = jnp.zeros(x.shape, F32)
    for c0 in range(0, d_ff, FFN_CHUNK):
        c1 = min(c0 + FFN_CHUNK, d_ff)
        g = _dot(h, wg_ref[:, c0:c1])
        u = _dot(h, wu_ref[:, c0:c1])
        a = (g * _sigmoid(g) * u).astype(BF16)
        acc = acc + _dot(a, wd_ref[c0:c1, :])
    o_ref[...] = x + FFN_RESIDUAL_WEIGHT * _rms(acc, gpost_ref[...])


def _ffn_body(x_ref, *refs):
    _swiglu_half_step(x_ref[...], *refs)


def _attn_out_ffn_body(x_ref, aT_ref, wo_ref, gmix_ref, *refs):
    x = x_ref[...] + _rms(_dot_tn(aT_ref[0], wo_ref[...]), gmix_ref[...])
    _swiglu_half_step(x, *refs)


def _ffn_half(x2, g_pre, g_post, w_gate, w_up, w_down, attn_out=None):
    t, d = x2.shape
    f = w_gate.shape[1]
    tm = min(FFN_ROWS, t)
    row = pl.BlockSpec((tm, d), lambda i: (i, 0))
    ffn_specs = [_const_spec((1, d)), _const_spec((1, d)),
                 _const_spec((d, f)), _const_spec((d, f)), _const_spec((f, d))]
    ffn_args = (g_pre.reshape(1, d), g_post.reshape(1, d),
                w_gate.astype(BF16), w_up.astype(BF16), w_down.astype(BF16))
    if attn_out is None:
        body, specs, args = _ffn_body, [row], (x2,)
    else:
        attnT, w_out, g_mix = attn_out
        tiles = attnT.shape[2] // tm
        body = _attn_out_ffn_body
        specs = [row, pl.BlockSpec((1, attnT.shape[1], tm), lambda i: (i // tiles, 0, i % tiles)),
                 _const_spec(w_out.shape), _const_spec((1, d))]
        args = (x2, attnT, w_out.astype(BF16), g_mix.reshape(1, d))
    return pl.pallas_call(
        body,
        grid=(t // tm,),
        in_specs=specs + ffn_specs,
        out_specs=row,
        out_shape=jax.ShapeDtypeStruct((t, d), F32),
        compiler_params=_params(("arbitrary",)),
        name="ffn_half",
    )(*args, *ffn_args)


CONV_ROWS = 256
CONV_HALO = 32
CONV_STRIP = 32


def _conv_body(x_ref, gpre_ref, w1_ref, b1_ref, wdw_ref, bdw_ref, lng_ref, lnb_ref,
               w2_ref, b2_ref, gpost_ref, o_ref, buf_ref, dw_ref, shift_ref):
    ts, d = x_ref.shape[1], x_ref.shape[2]

    @pl.when(pl.program_id(1) == 0)
    def _():
        buf_ref[0:CONV_HALO, :] = jnp.zeros((CONV_HALO, d), F32)

    x = x_ref[0]
    h = _rms(x, gpre_ref[...]).astype(BF16)
    p = _dot(h, w1_ref[...]) + b1_ref[...]
    buf_ref[CONV_HALO:CONV_HALO + ts, :] = p[:, :d] * _sigmoid(p[:, d:])

    base = CONV_HALO - (CONV_WIDTH - 1)
    span = shift_ref.shape[1]
    for b in range(1, SUBLANES):
        shift_ref[b - 1] = buf_ref[b:b + span, :]

    for r0 in range(0, ts, CONV_STRIP):
        acc = jnp.zeros((CONV_STRIP, d), F32)
        for k in range(CONV_WIDTH):
            a, b = divmod(base + k, SUBLANES)
            lo = r0 + a * SUBLANES
            win = buf_ref[lo:lo + CONV_STRIP, :] if b == 0 else shift_ref[b - 1, lo:lo + CONV_STRIP, :]
            acc = acc + wdw_ref[k:k + 1, :] * win
        dw_ref[r0:r0 + CONV_STRIP, :] = acc
    buf_ref[0:CONV_HALO, :] = buf_ref[ts:ts + CONV_HALO, :]

    c = dw_ref[...] + bdw_ref[...]
    mu = jnp.mean(c, axis=-1, keepdims=True)
    cc = c - mu
    var = jnp.mean(cc * cc, axis=-1, keepdims=True)
    y = cc * lax.rsqrt(var + LN_EPS) * lng_ref[...] + lnb_ref[...]
    y = (y * _sigmoid(y)).astype(BF16)
    out = _dot(y, w2_ref[...]) + b2_ref[...]
    o_ref[0] = x + _rms(out, gpost_ref[...])


def _conv_mixer(x, g_pre, g_post, w_pw1, b_pw1, w_dw, b_dw, ln_g, ln_b, w_pw2, b_pw2):
    b, s, d = x.shape
    ts = min(CONV_ROWS, s)
    tile = pl.BlockSpec((1, ts, d), lambda bi, j: (bi, j, 0))
    vec = lambda n: _const_spec((1, n))
    return pl.pallas_call(
        _conv_body,
        grid=(b, s // ts),
        in_specs=[tile, vec(d), _const_spec((d, 2 * d)), vec(2 * d), _const_spec((CONV_WIDTH, d)),
                  vec(d), vec(d), vec(d), _const_spec((d, d)), vec(d), vec(d)],
        out_specs=tile,
        out_shape=jax.ShapeDtypeStruct((b, s, d), F32),
        scratch_shapes=[pltpu.VMEM((ts + CONV_HALO, d), F32), pltpu.VMEM((ts, d), F32),
                        pltpu.VMEM((SUBLANES - 1, ts + (CONV_HALO - 1) // SUBLANES * SUBLANES, d), F32)],
        compiler_params=_params(("arbitrary", "arbitrary")),
        name="conv_mixer",
    )(x, g_pre.reshape(1, d), w_pw1.astype(BF16), b_pw1.reshape(1, 2 * d), w_dw,
      b_dw.reshape(1, d), ln_g.reshape(1, d), ln_b.reshape(1, d), w_pw2.astype(BF16),
      b_pw2.reshape(1, d), g_post.reshape(1, d))


PROJ_ROWS = 512
Q_WIDTH = NSA_HEADS * HEAD_DIM
KV_WIDTH = NSA_KV_GROUPS * HEAD_DIM
KPAD_WIDTH = NSA_KV_GROUPS * LANES
GATE_PAD = LANES
Q_SCALE = HEAD_DIM ** -0.5 * math.log2(math.e)


def _rope_tables(s):
    pos = jnp.arange(s, dtype=F32)
    inv_freq = ROPE_THETA ** (-jnp.arange(0, 2 * ROT_HALF, 2, dtype=F32) / (2 * ROT_HALF))
    ang = pos[:, None] * inv_freq[None, :]
    return jnp.cos(ang), jnp.sin(ang)


def _proj_body(x_ref, gpre_ref, wtok_ref, wtr_ref, bg_ref, cosT_ref, sinT_ref, ck_ref, s1_ref, s2_ref,
               qT_ref, qrT_ref, ksel_ref, kwin_ref, vselT_ref, vwinT_ref, kc_ref, vc_ref, gate_ref):
    tm = x_ref.shape[1]
    h = _rms(x_ref[0], gpre_ref[...]).astype(BF16)
    tok = _dot(h, wtok_ref[...])
    tr = _dot_nt(wtr_ref[...], h)

    q = tr[0:Q_WIDTH] * Q_SCALE
    qT_ref[0] = q.astype(BF16)
    q3 = q.reshape(NSA_HEADS, HEAD_DIM, tm)
    cos, sin = cosT_ref[...], sinT_ref[...]
    x1, x2 = q3[:, 0:ROT_HALF], q3[:, ROT_HALF:2 * ROT_HALF]
    qr = jnp.concatenate([x1 * cos - x2 * sin, x2 * cos + x1 * sin, q3[:, 2 * ROT_HALF:]], axis=1)
    qrT_ref[0] = qr.reshape(Q_WIDTH, tm).astype(BF16)
    vselT_ref[0] = tr[Q_WIDTH:Q_WIDTH + KV_WIDTH].astype(BF16)
    vwinT_ref[0] = tr[Q_WIDTH + KV_WIDTH:Q_WIDTH + 2 * KV_WIDTH].astype(BF16)

    ck, s1, s2 = ck_ref[...], s1_ref[...], s2_ref[...]
    for out_ref, base in ((ksel_ref, 0), (kwin_ref, KPAD_WIDTH)):
        for g in range(NSA_KV_GROUPS):
            k = tok[:, base + g * LANES:base + (g + 1) * LANES]
            kr = k * ck + pltpu.roll(k, ROT_HALF, 1) * s1 + pltpu.roll(k, LANES - ROT_HALF, 1) * s2
            out_ref[0, :, g * LANES:(g + 1) * LANES] = kr.astype(BF16)
    c0 = 2 * KPAD_WIDTH
    kc_ref[0] = tok[:, c0:c0 + KV_WIDTH]
    vc_ref[0] = tok[:, c0 + KV_WIDTH:c0 + 2 * KV_WIDTH]
    gate_ref[0] = _sigmoid(tok[:, c0 + 2 * KV_WIDTH:] + bg_ref[...])


def _nsa_project(x, g_pre, w_in, b_gate):
    b, s, d = x.shape
    tm = min(PROJ_ROWS, s)
    n_gate = NSA_HEADS * N_GATES
    w_q = w_in[:, :Q_WIDTH]
    w_kv = w_in[:, Q_WIDTH:Q_WIDTH + 6 * KV_WIDTH].reshape(d, 6, NSA_KV_GROUPS, HEAD_DIM)
    w_gate = w_in[:, Q_WIDTH + 6 * KV_WIDTH:]

    def padded(w):
        return jnp.pad(w, ((0, 0), (0, 0), (0, LANES - HEAD_DIM))).reshape(d, KPAD_WIDTH)

    w_tok = jnp.concatenate(
        [padded(w_kv[:, 2]), padded(w_kv[:, 4]), w_kv[:, 0].reshape(d, KV_WIDTH),
         w_kv[:, 1].reshape(d, KV_WIDTH), jnp.pad(w_gate, ((0, 0), (0, GATE_PAD - n_gate)))],
        axis=1).astype(BF16)
    w_tr = jnp.concatenate(
        [w_q, w_kv[:, 3].reshape(d, KV_WIDTH), w_kv[:, 5].reshape(d, KV_WIDTH)], axis=1).T.astype(BF16)
    bg = jnp.pad(b_gate, (0, GATE_PAD - n_gate)).reshape(1, GATE_PAD)

    cos, sin = _rope_tables(s)
    zeros = jnp.zeros_like(sin)
    pad_to = lambda parts, fill: jnp.concatenate(
        parts + [jnp.full((s, LANES - 2 * ROT_HALF), fill, F32)], axis=1)
    ck = pad_to([cos, cos], 1.0)
    s1 = pad_to([zeros, sin], 0.0)
    s2 = pad_to([-sin, zeros], 0.0)

    ntok, ntr = w_tok.shape[1], w_tr.shape[0]
    tok_major = lambda w: pl.BlockSpec((1, tm, w), lambda bi, j: (bi, j, 0))
    tr_major = lambda r: pl.BlockSpec((1, r, tm), lambda bi, j: (bi, 0, j))
    return pl.pallas_call(
        _proj_body,
        grid=(b, s // tm),
        in_specs=[tok_major(d), _const_spec((1, d)), _const_spec((d, ntok)), _const_spec((ntr, d)),
                  _const_spec((1, GATE_PAD)),
                  pl.BlockSpec((ROT_HALF, tm), lambda bi, j: (0, j)),
                  pl.BlockSpec((ROT_HALF, tm), lambda bi, j: (0, j)),
                  pl.BlockSpec((tm, LANES), lambda bi, j: (j, 0)),
                  pl.BlockSpec((tm, LANES), lambda bi, j: (j, 0)),
                  pl.BlockSpec((tm, LANES), lambda bi, j: (j, 0))],
        out_specs=[tr_major(Q_WIDTH), tr_major(Q_WIDTH), tok_major(KPAD_WIDTH), tok_major(KPAD_WIDTH),
                   tr_major(KV_WIDTH), tr_major(KV_WIDTH), tok_major(KV_WIDTH), tok_major(KV_WIDTH),
                   tok_major(GATE_PAD)],
        out_shape=[jax.ShapeDtypeStruct((b, Q_WIDTH, s), BF16), jax.ShapeDtypeStruct((b, Q_WIDTH, s), BF16),
                   jax.ShapeDtypeStruct((b, s, KPAD_WIDTH), BF16), jax.ShapeDtypeStruct((b, s, KPAD_WIDTH), BF16),
                   jax.ShapeDtypeStruct((b, KV_WIDTH, s), BF16), jax.ShapeDtypeStruct((b, KV_WIDTH, s), BF16),
                   jax.ShapeDtypeStruct((b, s, KV_WIDTH), F32), jax.ShapeDtypeStruct((b, s, KV_WIDTH), F32),
                   jax.ShapeDtypeStruct((b, s, GATE_PAD), F32)],
        compiler_params=_params(("arbitrary", "arbitrary")),
        name="nsa_project",
    )(x, g_pre.reshape(1, d), w_tok, w_tr, bg, cos.T, sin.T, ck, s1, s2)


def _compress_body(rk_ref, rv_ref, pos_ref, w1_ref, w2k_ref, w2vT_ref, kc_ref, vcT_ref):
    nrow = rk_ref.shape[2]
    half = rk_ref.shape[3]

    def hidden(r, t):
        top = _dot((r + pos_ref[t, 0:1, :]).astype(BF16), w1_ref[t, 0:half, :])
        bot = _dot((r + pos_ref[t, 1:2, :]).astype(BF16), w1_ref[t, half:2 * half, :])
        hid = top + pltpu.roll(bot, nrow - 1, 0)
        return (hid * _sigmoid(hid)).astype(BF16)

    kc_ref[0, 0] = _dot(hidden(rk_ref[0, 0], 0), w2k_ref[...]).astype(BF16)
    vcT_ref[0, 0] = _dot_nt(w2vT_ref[...], hidden(rv_ref[0, 0], 1)).astype(BF16)


def _nsa_compress(kc_raw, vc_raw, cmp_pos, cmp_w1, cmp_w2):
    b, s, _ = kc_raw.shape
    nrow = s // CMP_STRIDE
    half = CMP_STRIDE * HEAD_DIM
    hid = cmp_w1.shape[-1]

    def rows(raw):
        r = raw.reshape(b, nrow, CMP_STRIDE, NSA_KV_GROUPS, HEAD_DIM)
        return r.transpose(0, 3, 1, 2, 4).reshape(b, NSA_KV_GROUPS, nrow, half)

    pos = cmp_pos.reshape(2, 2, half)
    w2k = jnp.pad(cmp_w2[0], ((0, 0), (0, LANES - HEAD_DIM))).astype(BF16)
    w2vT = cmp_w2[1].T.astype(BF16)
    blk = pl.BlockSpec((1, 1, nrow, half), lambda bi, g: (bi, g, 0, 0))
    return pl.pallas_call(
        _compress_body,
        grid=(b, NSA_KV_GROUPS),
        in_specs=[blk, blk, _const_spec((2, 2, half)), _const_spec((2, 2 * half, hid)),
                  _const_spec((hid, LANES)), _const_spec((HEAD_DIM, hid))],
        out_specs=[pl.BlockSpec((1, 1, nrow, LANES), lambda bi, g: (bi, g, 0, 0)),
                   pl.BlockSpec((1, 1, HEAD_DIM, nrow), lambda bi, g: (bi, g, 0, 0))],
        out_shape=[jax.ShapeDtypeStruct((b, NSA_KV_GROUPS, nrow, LANES), BF16),
                   jax.ShapeDtypeStruct((b, NSA_KV_GROUPS, HEAD_DIM, nrow), BF16)],
        compiler_params=_params(("arbitrary", "arbitrary")),
        name="nsa_compress",
    )(rows(kc_raw), rows(vc_raw), pos, cmp_w1.astype(BF16), w2k, w2vT)


SEL_CHUNK = 512
CHUNK_BLOCKS = SEL_CHUNK // SEL_BLOCK
WIN_KEYS = WINDOW + Q_BLOCK
QLANES = NSA_REP * Q_BLOCK
STEP_BLOCKS = 2
STEP_LANES = STEP_BLOCKS * QLANES
N_FORCED = 3
ONES_ROWS = 16
CMP_ROWS_STEP = 128
MASKED_MAX_FLOOR = -1e29


def _with_ones(vT):
    return jnp.concatenate([vT, jnp.ones((ONES_ROWS, vT.shape[1]), BF16)], axis=0)


def _attn_body(qT_ref, qrT_ref, kc_ref, vcT_ref, ksel_ref, vselT_ref, kwin_ref, vwinT_ref, gate_ref,
               ovT_ref, oh_ref, o_ref, qaug_ref, s0_ref, s1_ref, m_ref, acc_ref, part_ref, swin_ref,
               top0_ref, top1_ref):
    i = pl.program_id(2)
    t_step = i * (STEP_BLOCKS * Q_BLOCK)
    nblk = ovT_ref.shape[0]
    ncmp = kc_ref.shape[2]
    n_top = min(N_SEL, nblk) - N_FORCED

    def lanes_of_heads(ref, e):
        cols = slice(e * Q_BLOCK, (e + 1) * Q_BLOCK)
        parts = [ref[0, r * HEAD_DIM:(r + 1) * HEAD_DIM, cols] for r in range(NSA_REP)]
        return jnp.concatenate(
            [jnp.concatenate(parts, axis=1), jnp.zeros((LANES - HEAD_DIM, QLANES), BF16)], axis=0)

    def positions(e):
        return t_step + e * Q_BLOCK + lax.broadcasted_iota(jnp.int32, (1, Q_BLOCK), 1)

    pos_step = jnp.concatenate([positions(e) for e in range(STEP_BLOCKS) for _ in range(NSA_REP)], axis=1)
    d = HEAD_DIM

    def chunk_start(j):
        return pl.multiple_of(j * SEL_CHUNK, SEL_CHUNK)

    def scores_into(s_ref, top_ref, j):
        k0 = chunk_start(j)
        ka = jnp.concatenate([ksel_ref[0, pl.ds(k0, SEL_CHUNK), :], oh_ref[pl.ds(k0, SEL_CHUNK), :]], axis=1)
        s = _dot(ka, qaug_ref[...])
        s_ref[...] = s
        top_ref[...] = jnp.max(s, axis=0, keepdims=True)

    def block_select_and_window(e, cmp_rows, steady):
        t0 = t_step + e * Q_BLOCK
        lanes = slice(e * QLANES, (e + 1) * QLANES)
        qT = lanes_of_heads(qT_ref, e)
        qrT = lanes_of_heads(qrT_ref, e)
        pos1 = positions(e)
        pos = jnp.concatenate([pos1] * NSA_REP, axis=1)
        qaug_ref[0:LANES, lanes] = qrT
        s = _dot(kc_ref[0, 0, 0:cmp_rows, :], qT)
        w0 = pl.multiple_of(t0 - WINDOW if steady else jnp.maximum(t0 - WINDOW, 0), Q_BLOCK)
        swin_ref[e] = _dot(kwin_ref[0, pl.ds(w0, WIN_KEYS), :], qrT)
        s0_ref[:, lanes] = _dot(ksel_ref[0, 0:SEL_CHUNK, :], qrT)
        cend = lax.broadcasted_iota(jnp.int32, (cmp_rows, QLANES), 0) * CMP_STRIDE + (CMP_LEN - 1)
        s = jnp.where(cend <= pos, s, NEG_INF)
        m = jnp.maximum(jnp.max(s, axis=0, keepdims=True), MASKED_MAX_FLOOR)
        p = jnp.exp2(s - m)
        inv_l = 1.0 / jnp.maximum(jnp.sum(p, axis=0, keepdims=True), 1e-30)
        o_cmp = _dot(vcT_ref[0, 0, :, 0:cmp_rows], p.astype(BF16)) * inv_l

        pn = p * inv_l
        ph = pn[:, 0:Q_BLOCK]
        for r in range(1, NSA_REP):
            ph = ph + pn[:, r * Q_BLOCK:(r + 1) * Q_BLOCK]
        hi = ph.astype(BF16)
        lo = (ph - hi.astype(F32)).astype(BF16)
        ovT = ovT_ref[:, 0:cmp_rows]
        imp = _dot(ovT, hi) + _dot(ovT, lo)

        blk = lax.broadcasted_iota(jnp.int32, (nblk, Q_BLOCK), 0)
        cur = pos1 // SEL_BLOCK
        forced = (blk == 0) | (blk == cur) | (blk == cur - 1)
        v0 = jnp.where(forced, -1.0, jnp.where(blk * SEL_BLOCK <= pos1, imp, -1.0))
        n_piece = WIN_KEYS // Q_BLOCK
        kk = lax.broadcasted_iota(jnp.int32, (Q_BLOCK, QLANES), 0)
        qq = lax.broadcasted_iota(jnp.int32, (Q_BLOCK, QLANES), 1) % Q_BLOCK
        win = {"m": None, "acc": None}

        def piece_max(w):
            rows = slice(w * Q_BLOCK, (w + 1) * Q_BLOCK)
            sp = swin_ref[e, rows, :]
            if not steady:
                kpos = w0 + w * Q_BLOCK + kk
                sp = jnp.where(kpos <= pos, jnp.where(kpos > pos - WINDOW, sp, NEG_INF), NEG_INF)
            elif w == 0:
                sp = jnp.where(kk > qq, sp, NEG_INF)
            elif w == n_piece - 1:
                sp = jnp.where(kk <= qq, sp, NEG_INF)
            if not steady or w in (0, n_piece - 1):
                swin_ref[e, rows, :] = sp
            top = jnp.max(sp, axis=0, keepdims=True)
            win["m"] = top if win["m"] is None else jnp.maximum(win["m"], top)

        def piece_absorb(w):
            rows = slice(w * Q_BLOCK, (w + 1) * Q_BLOCK)
            p = jnp.exp2(swin_ref[e, rows, :] - win["m"]).astype(BF16)
            k0 = pl.multiple_of(w0 + w * Q_BLOCK, Q_BLOCK)
            pv = _dot(_with_ones(vwinT_ref[0, :, pl.ds(k0, Q_BLOCK)]), p)
            win["acc"] = pv if win["acc"] is None else win["acc"] + pv

        for w in range(n_piece):
            piece_max(w)
        left = v0
        pending = list(range(n_piece))
        for r in range(n_top):
            left = jnp.where(left == jnp.max(left, axis=0, keepdims=True), -2.0, left)
            if pending and r % 2 == 0:
                piece_absorb(pending.pop(0))
        for w in pending:
            piece_absorb(w)
        taken = left != v0
        n_taken = jnp.sum(jnp.where(taken, jnp.where(v0 >= 0.0, 1.0, 0.0), 0.0), axis=0, keepdims=True)

        def store_bias(left):
            bias = jnp.where(forced, 0.0, jnp.where(left != v0, 0.0, NEG_INF)).astype(BF16)
            bias = jnp.concatenate([bias] * NSA_REP, axis=1)
            if nblk < LANES:
                bias = jnp.concatenate([bias, jnp.zeros((LANES - nblk, QLANES), BF16)], axis=0)
            qaug_ref[LANES:2 * LANES, lanes] = bias

        store_bias(left)
        bias0 = jnp.where(forced[0:CHUNK_BLOCKS], 0.0,
                          jnp.where(left[0:CHUNK_BLOCKS] != v0[0:CHUNK_BLOCKS], 0.0, NEG_INF))
        bias0 = jnp.concatenate([bias0] * NSA_REP, axis=1)
        top = None
        for n in range(CHUNK_BLOCKS):
            rows = slice(n * SEL_BLOCK, (n + 1) * SEL_BLOCK)
            sb = s0_ref[rows, lanes] + bias0[n:n + 1, :]
            s0_ref[rows, lanes] = sb
            bt = jnp.max(sb, axis=0, keepdims=True)
            top = bt if top is None else jnp.maximum(top, bt)
        top0_ref[:, lanes] = top

        a_w = win["acc"]
        gate = gate_ref[0, 0, e]
        part_ref[:, lanes] = gate[0:1] * o_cmp + gate[2:3] * (a_w[0:d] / a_w[d:d + 1])

        def select_exactly():
            val = v0
            for _ in range(n_top):
                top = jnp.max(val, axis=0, keepdims=True)
                first = jnp.min(jnp.where(val == top, blk, nblk), axis=0, keepdims=True)
                val = jnp.where(blk == first, -2.0, val)
            store_bias(val)
            ka = jnp.concatenate([ksel_ref[0, 0:SEL_CHUNK, :], oh_ref[0:SEL_CHUNK, :]], axis=1)
            s = _dot(ka, qaug_ref[:, lanes])
            s0_ref[:, lanes] = s
            top0_ref[:, lanes] = jnp.max(s, axis=0, keepdims=True)

        return jnp.max(n_taken) > n_top, select_exactly

    def select_and_window(cmp_rows, steady):
        redo = [block_select_and_window(e, cmp_rows, steady) for e in range(STEP_BLOCKS)]
        for overflowed, select_exactly in redo:
            pl.when(overflowed)(select_exactly)

    step_tokens = STEP_BLOCKS * Q_BLOCK

    def visible_cmp(step):
        return (step * step_tokens + step_tokens - CMP_LEN) // CMP_STRIDE + 1

    n_steps = ksel_ref.shape[1] // step_tokens
    steady_from = -(-WINDOW // step_tokens)
    first_rows = min(CMP_ROWS_STEP, ncmp)
    assert visible_cmp(steady_from - 1) <= first_rows
    pl.when(i < steady_from)(functools.partial(select_and_window, first_rows, False))
    lo = steady_from
    for rows in range(first_rows, ncmp + 1, CMP_ROWS_STEP):
        hi = next((step for step in range(lo, n_steps) if visible_cmp(step) > rows), n_steps)
        if hi > lo:
            pl.when((i >= lo) & (i < hi))(functools.partial(select_and_window, rows, True))
            lo = hi
    assert lo == n_steps

    def absorb(s_ref, top_ref, j, diagonal):
        k0 = chunk_start(j)
        s = s_ref[...]
        if diagonal:
            kpos = k0 + lax.broadcasted_iota(jnp.int32, (SEL_CHUNK, STEP_LANES), 0)
            s = jnp.where(kpos <= pos_step, s, NEG_INF)
            top = jnp.max(s, axis=0, keepdims=True)
        else:
            top = top_ref[...]
        m = m_ref[...]
        m_new = jnp.maximum(m, top)
        p = jnp.exp2(s - m_new).astype(BF16)
        acc_ref[...] = jnp.exp2(m - m_new) * acc_ref[...] + _dot(
            _with_ones(vselT_ref[0, :, pl.ds(k0, SEL_CHUNK)]), p)
        m_ref[...] = m_new

    last = t_step // SEL_CHUNK
    m_ref[...] = jnp.full(m_ref.shape, NEG_INF, F32)
    acc_ref[...] = jnp.zeros(acc_ref.shape, F32)

    def pair(t):
        scores_into(s1_ref, top1_ref, 2 * t + 1)
        absorb(s0_ref, top0_ref, 2 * t, False)
        scores_into(s0_ref, top0_ref, 2 * t + 2)
        absorb(s1_ref, top1_ref, 2 * t + 1, False)

    def two_pairs(u, carry):
        pair(2 * u)
        pair(2 * u + 1)
        return carry

    n_pairs = last // 2
    lax.fori_loop(0, n_pairs // 2, two_pairs, 0)

    @pl.when(n_pairs % 2 == 1)
    def _():
        pair(n_pairs - 1)

    @pl.when(last % 2 == 1)
    def _():
        scores_into(s1_ref, top1_ref, last)
        absorb(s0_ref, top0_ref, last - 1, False)
        absorb(s1_ref, top1_ref, last, True)

    @pl.when(last % 2 == 0)
    def _():
        absorb(s0_ref, top0_ref, last, True)

    a_s = acc_ref[...]
    gate_sel = jnp.concatenate([gate_ref[0, 0, e][1:2] for e in range(STEP_BLOCKS)], axis=1)
    out = part_ref[...] + gate_sel * (a_s[0:d] / a_s[d:d + 1])
    for e in range(STEP_BLOCKS):
        for r in range(NSA_REP):
            lane0 = e * QLANES + r * Q_BLOCK
            o_ref[0, r * HEAD_DIM:(r + 1) * HEAD_DIM, e * Q_BLOCK:(e + 1) * Q_BLOCK] = (
                out[:, lane0:lane0 + Q_BLOCK].astype(BF16))


def _nsa_attention(qT, qrT, kcmp, vcmpT, ksel, vselT, kwin, vwinT, gates):
    b, _, s = qT.shape
    nq = s // Q_BLOCK
    nblk = s // SEL_BLOCK
    ncmp = kcmp.shape[2]
    rows = NSA_REP * HEAD_DIM
    assert nblk <= LANES and s % SEL_CHUNK == 0 and s >= WIN_KEYS

    gt = gates[:, :, :NSA_HEADS * N_GATES].reshape(b, nq, Q_BLOCK, NSA_KV_GROUPS, NSA_REP, N_GATES)
    gt = gt.transpose(0, 3, 1, 5, 4, 2).reshape(b, NSA_KV_GROUPS, nq, N_GATES, QLANES)

    cstart = np.arange(ncmp) * CMP_STRIDE
    sstart = np.arange(nblk) * SEL_BLOCK
    ovT = ((cstart[None, :] < sstart[:, None] + SEL_BLOCK) & (cstart[None, :] + CMP_LEN > sstart[:, None]))
    ovT = jnp.asarray(ovT, BF16)
    onehot = jnp.asarray(np.arange(s)[:, None] // SEL_BLOCK == np.arange(LANES)[None, :], BF16)

    assert nq % STEP_BLOCKS == 0
    q_spec = pl.BlockSpec((1, rows, STEP_BLOCKS * Q_BLOCK), lambda bi, g, i: (bi, g, i))
    k_spec = pl.BlockSpec((1, s, LANES), lambda bi, g, i: (bi, 0, g))
    vT_spec = pl.BlockSpec((1, HEAD_DIM, s), lambda bi, g, i: (bi, g, 0))
    return pl.pallas_call(
        _attn_body,
        grid=(b, NSA_KV_GROUPS, nq // STEP_BLOCKS),
        in_specs=[q_spec, q_spec,
                  pl.BlockSpec((1, 1, ncmp, LANES), lambda bi, g, i: (bi, g, 0, 0)),
                  pl.BlockSpec((1, 1, HEAD_DIM, ncmp), lambda bi, g, i: (bi, g, 0, 0)),
                  k_spec, vT_spec, k_spec, vT_spec,
                  pl.BlockSpec((1, 1, STEP_BLOCKS, N_GATES, QLANES), lambda bi, g, i: (bi, g, i, 0, 0)),
                  _const_spec((nblk, ncmp)), _const_spec((s, LANES))],
        out_specs=q_spec,
        out_shape=jax.ShapeDtypeStruct((b, NSA_HEADS * HEAD_DIM, s), BF16),
        scratch_shapes=[pltpu.VMEM((2 * LANES, STEP_LANES), BF16),
                        pltpu.VMEM((SEL_CHUNK, STEP_LANES), F32), pltpu.VMEM((SEL_CHUNK, STEP_LANES), F32),
                        pltpu.VMEM((1, STEP_LANES), F32), pltpu.VMEM((HEAD_DIM + ONES_ROWS, STEP_LANES), F32),
                        pltpu.VMEM((HEAD_DIM, STEP_LANES), F32), pltpu.VMEM((STEP_BLOCKS, WIN_KEYS, QLANES), F32),
                        pltpu.VMEM((1, STEP_LANES), F32), pltpu.VMEM((1, STEP_LANES), F32)],
        compiler_params=_params(("arbitrary", "arbitrary", "arbitrary")),
        name="nsa_attention",
    )(qT, qrT, kcmp, vcmpT, ksel, vselT, kwin, vwinT, gt, ovT, onehot)


def _nsa_mixer(x, g_pre, w_in, b_gate, cmp_pos, cmp_w1, cmp_w2):
    qT, qrT, ksel, kwin, vselT, vwinT, kc_raw, vc_raw, gates = _nsa_project(x, g_pre, w_in, b_gate)
    kcmp, vcmpT = _nsa_compress(kc_raw, vc_raw, cmp_pos, cmp_w1, cmp_w2)
    return _nsa_attention(qT, qrT, kcmp, vcmpT, ksel, vselT, kwin, vwinT, gates)


def kernel(x, mix_norm_pre, mix_norm_post, ffn_norm_pre, ffn_norm_post, ffn_w_gate, ffn_w_up, ffn_w_down,
           conv_w_pw1, conv_b_pw1, conv_w_dw, conv_b_dw, conv_ln_g, conv_ln_b, conv_w_pw2, conv_b_pw2,
           nsa_w_in, nsa_b_gate, nsa_cmp_pos, nsa_cmp_w1, nsa_cmp_w2, nsa_w_out):
    b, s, d = x.shape
    depth = mix_norm_pre.shape[0]
    n_mixers = 2

    def ffn(x, i, half, attn_out=None):
        y = _ffn_half(x.reshape(b * s, d), ffn_norm_pre[i, half], ffn_norm_post[i, half],
                      ffn_w_gate[i, half], ffn_w_up[i, half], ffn_w_down[i, half], attn_out)
        return y.reshape(b, s, d)

    for i in range(depth):
        x = ffn(x, i, 0)
        j = i // n_mixers
        if i % n_mixers == 0:
            x = _conv_mixer(x, mix_norm_pre[i], mix_norm_post[i], conv_w_pw1[j], conv_b_pw1[j], conv_w_dw[j],
                            conv_b_dw[j], conv_ln_g[j], conv_ln_b[j], conv_w_pw2[j], conv_b_pw2[j])
            x = ffn(x, i, 1)
        else:
            attnT = _nsa_mixer(x, mix_norm_pre[i], nsa_w_in[j], nsa_b_gate[j], nsa_cmp_pos[j],
                               nsa_cmp_w1[j], nsa_cmp_w2[j])
            x = ffn(x, i, 1, (attnT, nsa_w_out[j], mix_norm_post[i]))
    return x
```

```python
import functools
import math

import jax
import jax.numpy as jnp
import numpy as np
from jax import lax
from jax.experimental import pallas as pl
from jax.experimental.pallas import tpu as pltpu

RMS_EPS = 1e-6
LN_EPS = 1e-5
FFN_RESIDUAL_WEIGHT = 0.5
CONV_WIDTH = 31
NSA_HEADS = 16
NSA_KV_GROUPS = 4
NSA_REP = NSA_HEADS // NSA_KV_GROUPS
HEAD_DIM = 64
ROT_HALF = HEAD_DIM // 8
ROPE_THETA = 500000.0
CMP_LEN = 32
CMP_STRIDE = 16
SEL_BLOCK = 64
N_SEL = 16
WINDOW = 512
Q_BLOCK = 128
N_GATES = 3
NEG_INF = -1e30

LANES = 128
SUBLANES = 8
V7X_VMEM_LIMIT = 56 * 1024 * 1024

BF16 = jnp.bfloat16
F32 = jnp.float32


def _params(sem):
    return pltpu.CompilerParams(dimension_semantics=sem, vmem_limit_bytes=V7X_VMEM_LIMIT)


def _const_spec(shape):
    nd = len(shape)
    return pl.BlockSpec(shape, lambda *_: (0,) * nd, pipeline_mode=pl.Buffered(1))


def _rms(x, gain):
    return x * lax.rsqrt(jnp.mean(x * x, axis=-1, keepdims=True) + RMS_EPS) * gain


def _sigmoid(x):
    return 1.0 / (1.0 + jnp.exp(-x))


def _dot(a, b):
    return jnp.dot(a, b, preferred_element_type=F32)


def _dot_nt(a, b):
    return lax.dot_general(a, b, (((1,), (1,)), ((), ())), preferred_element_type=F32)


def _dot_tn(a, b):
    return lax.dot_general(a, b, (((0,), (0,)), ((), ())), preferred_element_type=F32)


FFN_ROWS = 512
FFN_CHUNK = 512


def _swiglu_half_step(x, gpre_ref, gpost_ref, wg_ref, wu_ref, wd_ref, o_ref):
    h = _rms(x, gpre_ref[...]).astype(BF16)
    d_ff = wd_ref.shape[0]
    acc = jnp.zeros(x.shape, F32)
    for c0 in range(0, d_ff, FFN_CHUNK):
        c1 = min(c0 + FFN_CHUNK, d_ff)
        g = _dot(h, wg_ref[:, c0:c1])
        u = _dot(h, wu_ref[:, c0:c1])
        a = (g * _sigmoid(g) * u).astype(BF16)
        acc = acc + _dot(a, wd_ref[c0:c1, :])
    o_ref[...] = x + FFN_RESIDUAL_WEIGHT * _rms(acc, gpost_ref[...])


def _ffn_body(x_ref, *refs):
    _swiglu_half_step(x_ref[...], *refs)


def _attn_out_ffn_body(x_ref, aT_ref, wo_ref, gmix_ref, *refs):
    x = x_ref[...] + _rms(_dot_tn(aT_ref[0], wo_ref[...]), gmix_ref[...])
    _swiglu_half_step(x, *refs)


def _ffn_half(x2, g_pre, g_post, w_gate, w_up, w_down, attn_out=None):
    t, d = x2.shape
    f = w_gate.shape[1]
    tm = min(FFN_ROWS, t)
    row = pl.BlockSpec((tm, d), lambda i: (i, 0))
    ffn_specs = [_const_spec((1, d)), _const_spec((1, d)),
                 _const_spec((d, f)), _const_spec((d, f)), _const_spec((f, d))]
    ffn_args = (g_pre.reshape(1, d), g_post.reshape(1, d),
                w_gate.astype(BF16), w_up.astype(BF16), w_down.astype(BF16))
    if attn_out is None:
        body, specs, args = _ffn_body, [row], (x2,)
    else:
        attnT, w_out, g_mix = attn_out
        tiles = attnT.shape[2] // tm
        body = _attn_out_ffn_body
        specs = [row, pl.BlockSpec((1, attnT.shape[1], tm), lambda i: (i // tiles, 0, i % tiles)),
                 _const_spec(w_out.shape), _const_spec((1, d))]
        args = (x2, attnT, w_out.astype(BF16), g_mix.reshape(1, d))
    return pl.pallas_call(
        body,
        grid=(t // tm,),
        in_specs=specs + ffn_specs,
        out_specs=row,
        out_shape=jax.ShapeDtypeStruct((t, d), F32),
        compiler_params=_params(("arbitrary",)),
        name="ffn_half",
    )(*args, *ffn_args)


CONV_ROWS = 256
CONV_HALO = 32
CONV_STRIP = 32


def _conv_body(x_ref, gpre_ref, w1_ref, b1_ref, wdw_ref, bdw_ref, lng_ref, lnb_ref,
               w2_ref, b2_ref, gpost_ref, o_ref, buf_ref, dw_ref, shift_ref):
    ts, d = x_ref.shape[1], x_ref.shape[2]

    @pl.when(pl.program_id(1) == 0)
    def _():
        buf_ref[0:CONV_HALO, :] = jnp.zeros((CONV_HALO, d), F32)

    x = x_ref[0]
    h = _rms(x, gpre_ref[...]).astype(BF16)
    p = _dot(h, w1_ref[...]) + b1_ref[...]
    buf_ref[CONV_HALO:CONV_HALO + ts, :] = p[:, :d] * _sigmoid(p[:, d:])

    base = CONV_HALO - (CONV_WIDTH - 1)
    span = shift_ref.shape[1]
    for b in range(1, SUBLANES):
        shift_ref[b - 1] = buf_ref[b:b + span, :]

    for r0 in range(0, ts, CONV_STRIP):
        acc = jnp.zeros((CONV_STRIP, d), F32)
        for k in range(CONV_WIDTH):
            a, b = divmod(base + k, SUBLANES)
            lo = r0 + a * SUBLANES
            win = buf_ref[lo:lo + CONV_STRIP, :] if b == 0 else shift_ref[b - 1, lo:lo + CONV_STRIP, :]
            acc = acc + wdw_ref[k:k + 1, :] * win
        dw_ref[r0:r0 + CONV_STRIP, :] = acc
    buf_ref[0:CONV_HALO, :] = buf_ref[ts:ts + CONV_HALO, :]

    c = dw_ref[...] + bdw_ref[...]
    mu = jnp.mean(c, axis=-1, keepdims=True)
    cc = c - mu
    var = jnp.mean(cc * cc, axis=-1, keepdims=True)
    y = cc * lax.rsqrt(var + LN_EPS) * lng_ref[...] + lnb_ref[...]
    y = (y * _sigmoid(y)).astype(BF16)
    out = _dot(y, w2_ref[...]) + b2_ref[...]
    o_ref[0] = x + _rms(out, gpost_ref[...])


def _conv_mixer(x, g_pre, g_post, w_pw1, b_pw1, w_dw, b_dw, ln_g, ln_b, w_pw2, b_pw2):
    b, s, d = x.shape
    ts = min(CONV_ROWS, s)
    tile = pl.BlockSpec((1, ts, d), lambda bi, j: (bi, j, 0))
    vec = lambda n: _const_spec((1, n))
    return pl.pallas_call(
        _conv_body,
        grid=(b, s // ts),
        in_specs=[tile, vec(d), _const_spec((d, 2 * d)), vec(2 * d), _const_spec((CONV_WIDTH, d)),
                  vec(d), vec(d), vec(d), _const_spec((d, d)), vec(d), vec(d)],
        out_specs=tile,
        out_shape=jax.ShapeDtypeStruct((b, s, d), F32),
        scratch_shapes=[pltpu.VMEM((ts + CONV_HALO, d), F32), pltpu.VMEM((ts, d), F32),
                        pltpu.VMEM((SUBLANES - 1, ts + (CONV_HALO - 1) // SUBLANES * SUBLANES, d), F32)],
        compiler_params=_params(("arbitrary", "arbitrary")),
        name="conv_mixer",
    )(x, g_pre.reshape(1, d), w_pw1.astype(BF16), b_pw1.reshape(1, 2 * d), w_dw,
      b_dw.reshape(1, d), ln_g.reshape(1, d), ln_b.reshape(1, d), w_pw2.astype(BF16),
      b_pw2.reshape(1, d), g_post.reshape(1, d))


PROJ_ROWS = 512
Q_WIDTH = NSA_HEADS * HEAD_DIM
KV_WIDTH = NSA_KV_GROUPS * HEAD_DIM
KPAD_WIDTH = NSA_KV_GROUPS * LANES
GATE_PAD = LANES
Q_SCALE = HEAD_DIM ** -0.5 * math.log2(math.e)


def _rope_tables(s):
    pos = jnp.arange(s, dtype=F32)
    inv_freq = ROPE_THETA ** (-jnp.arange(0, 2 * ROT_HALF, 2, dtype=F32) / (2 * ROT_HALF))
    ang = pos[:, None] * inv_freq[None, :]
    return jnp.cos(ang), jnp.sin(ang)


def _proj_body(x_ref, gpre_ref, wtok_ref, wtr_ref, bg_ref, cosT_ref, sinT_ref, ck_ref, s1_ref, s2_ref,
               qT_ref, qrT_ref, ksel_ref, kwin_ref, vselT_ref, vwinT_ref, kc_ref, vc_ref, gate_ref):
    tm = x_ref.shape[1]
    h = _rms(x_ref[0], gpre_ref[...]).astype(BF16)
    tok = _dot(h, wtok_ref[...])
    tr = _dot_nt(wtr_ref[...], h)

    q = tr[0:Q_WIDTH] * Q_SCALE
    qT_ref[0] = q.astype(BF16)
    q3 = q.reshape(NSA_HEADS, HEAD_DIM, tm)
    cos, sin = cosT_ref[...], sinT_ref[...]
    x1, x2 = q3[:, 0:ROT_HALF], q3[:, ROT_HALF:2 * ROT_HALF]
    qr = jnp.concatenate([x1 * cos - x2 * sin, x2 * cos + x1 * sin, q3[:, 2 * ROT_HALF:]], axis=1)
    qrT_ref[0] = qr.reshape(Q_WIDTH, tm).astype(BF16)
    vselT_ref[0] = tr[Q_WIDTH:Q_WIDTH + KV_WIDTH].astype(BF16)
    vwinT_ref[0] = tr[Q_WIDTH + KV_WIDTH:Q_WIDTH + 2 * KV_WIDTH].astype(BF16)

    ck, s1, s2 = ck_ref[...], s1_ref[...], s2_ref[...]
    for out_ref, base in ((ksel_ref, 0), (kwin_ref, KPAD_WIDTH)):
        for g in range(NSA_KV_GROUPS):
            k = tok[:, base + g * LANES:base + (g + 1) * LANES]
            kr = k * ck + pltpu.roll(k, ROT_HALF, 1) * s1 + pltpu.roll(k, LANES - ROT_HALF, 1) * s2
            out_ref[0, :, g * LANES:(g + 1) * LANES] = kr.astype(BF16)
    c0 = 2 * KPAD_WIDTH
    kc_ref[0] = tok[:, c0:c0 + KV_WIDTH]
    vc_ref[0] = tok[:, c0 + KV_WIDTH:c0 + 2 * KV_WIDTH]
    gate_ref[0] = _sigmoid(tok[:, c0 + 2 * KV_WIDTH:] + bg_ref[...])


def _nsa_project(x, g_pre, w_in, b_gate):
    b, s, d = x.shape
    tm = min(PROJ_ROWS, s)
    n_gate = NSA_HEADS * N_GATES
    w_q = w_in[:, :Q_WIDTH]
    w_kv = w_in[:, Q_WIDTH:Q_WIDTH + 6 * KV_WIDTH].reshape(d, 6, NSA_KV_GROUPS, HEAD_DIM)
    w_gate = w_in[:, Q_WIDTH + 6 * KV_WIDTH:]

    def padded(w):
        return jnp.pad(w, ((0, 0), (0, 0), (0, LANES - HEAD_DIM))).reshape(d, KPAD_WIDTH)

    w_tok = jnp.concatenate(
        [padded(w_kv[:, 2]), padded(w_kv[:, 4]), w_kv[:, 0].reshape(d, KV_WIDTH),
         w_kv[:, 1].reshape(d, KV_WIDTH), jnp.pad(w_gate, ((0, 0), (0, GATE_PAD - n_gate)))],
        axis=1).astype(BF16)
    w_tr = jnp.concatenate(
        [w_q, w_kv[:, 3].reshape(d, KV_WIDTH), w_kv[:, 5].reshape(d, KV_WIDTH)], axis=1).T.astype(BF16)
    bg = jnp.pad(b_gate, (0, GATE_PAD - n_gate)).reshape(1, GATE_PAD)

    cos, sin = _rope_tables(s)
    zeros = jnp.zeros_like(sin)
    pad_to = lambda parts, fill: jnp.concatenate(
        parts + [jnp.full((s, LANES - 2 * ROT_HALF), fill, F32)], axis=1)
    ck = pad_to([cos, cos], 1.0)
    s1 = pad_to([zeros, sin], 0.0)
    s2 = pad_to([-sin, zeros], 0.0)

    ntok, ntr = w_tok.shape[1], w_tr.shape[0]
    tok_major = lambda w: pl.BlockSpec((1, tm, w), lambda bi, j: (bi, j, 0))
    tr_major = lambda r: pl.BlockSpec((1, r, tm), lambda bi, j: (bi, 0, j))
    return pl.pallas_call(
        _proj_body,
        grid=(b, s // tm),
        in_specs=[tok_major(d), _const_spec((1, d)), _const_spec((d, ntok)), _const_spec((ntr, d)),
                  _const_spec((1, GATE_PAD)),
                  pl.BlockSpec((ROT_HALF, tm), lambda bi, j: (0, j)),
                  pl.BlockSpec((ROT_HALF, tm), lambda bi, j: (0, j)),
                  pl.BlockSpec((tm, LANES), lambda bi, j: (j, 0)),
                  pl.BlockSpec((tm, LANES), lambda bi, j: (j, 0)),
                  pl.BlockSpec((tm, LANES), lambda bi, j: (j, 0))],
        out_specs=[tr_major(Q_WIDTH), tr_major(Q_WIDTH), tok_major(KPAD_WIDTH), tok_major(KPAD_WIDTH),
                   tr_major(KV_WIDTH), tr_major(KV_WIDTH), tok_major(KV_WIDTH), tok_major(KV_WIDTH),
                   tok_major(GATE_PAD)],
        out_shape=[jax.ShapeDtypeStruct((b, Q_WIDTH, s), BF16), jax.ShapeDtypeStruct((b, Q_WIDTH, s), BF16),
                   jax.ShapeDtypeStruct((b, s, KPAD_WIDTH), BF16), jax.ShapeDtypeStruct((b, s, KPAD_WIDTH), BF16),
                   jax.ShapeDtypeStruct((b, KV_WIDTH, s), BF16), jax.ShapeDtypeStruct((b, KV_WIDTH, s), BF16),
                   jax.ShapeDtypeStruct((b, s, KV_WIDTH), F32), jax.ShapeDtypeStruct((b, s, KV_WIDTH), F32),
                   jax.ShapeDtypeStruct((b, s, GATE_PAD), F32)],
        compiler_params=_params(("arbitrary", "arbitrary")),
        name="nsa_project",
    )(x, g_pre.reshape(1, d), w_tok, w_tr, bg, cos.T, sin.T, ck, s1, s2)


def _compress_body(rk_ref, rv_ref, pos_ref, w1_ref, w2k_ref, w2vT_ref, kc_ref, vcT_ref):
    nrow = rk_ref.shape[2]
    half = rk_ref.shape[3]

    def hidden(r, t):
        top = _dot((r + pos_ref[t, 0:1, :]).astype(BF16), w1_ref[t, 0:half, :])
        bot = _dot((r + pos_ref[t, 1:2, :]).astype(BF16), w1_ref[t, half:2 * half, :])
        hid = top + pltpu.roll(bot, nrow - 1, 0)
        return (hid * _sigmoid(hid)).astype(BF16)

    kc_ref[0, 0] = _dot(hidden(rk_ref[0, 0], 0), w2k_ref[...]).astype(BF16)
    vcT_ref[0, 0] = _dot_nt(w2vT_ref[...], hidden(rv_ref[0, 0], 1)).astype(BF16)


def _nsa_compress(kc_raw, vc_raw, cmp_pos, cmp_w1, cmp_w2):
    b, s, _ = kc_raw.shape
    nrow = s // CMP_STRIDE
    half = CMP_STRIDE * HEAD_DIM
    hid = cmp_w1.shape[-1]

    def rows(raw):
        r = raw.reshape(b, nrow, CMP_STRIDE, NSA_KV_GROUPS, HEAD_DIM)
        return r.transpose(0, 3, 1, 2, 4).reshape(b, NSA_KV_GROUPS, nrow, half)

    pos = cmp_pos.reshape(2, 2, half)
    w2k = jnp.pad(cmp_w2[0], ((0, 0), (0, LANES - HEAD_DIM))).astype(BF16)
    w2vT = cmp_w2[1].T.astype(BF16)
    blk = pl.BlockSpec((1, 1, nrow, half), lambda bi, g: (bi, g, 0, 0))
    return pl.pallas_call(
        _compress_body,
        grid=(b, NSA_KV_GROUPS),
        in_specs=[blk, blk, _const_spec((2, 2, half)), _const_spec((2, 2 * half, hid)),
                  _const_spec((hid, LANES)), _const_spec((HEAD_DIM, hid))],
        out_specs=[pl.BlockSpec((1, 1, nrow, LANES), lambda bi, g: (bi, g, 0, 0)),
                   pl.BlockSpec((1, 1, HEAD_DIM, nrow), lambda bi, g: (bi, g, 0, 0))],
        out_shape=[jax.ShapeDtypeStruct((b, NSA_KV_GROUPS, nrow, LANES), BF16),
                   jax.ShapeDtypeStruct((b, NSA_KV_GROUPS, HEAD_DIM, nrow), BF16)],
        compiler_params=_params(("arbitrary", "arbitrary")),
        name="nsa_compress",
    )(rows(kc_raw), rows(vc_raw), pos, cmp_w1.astype(BF16), w2k, w2vT)


SEL_CHUNK = 512
CHUNK_BLOCKS = SEL_CHUNK // SEL_BLOCK
WIN_KEYS = WINDOW + Q_BLOCK
QLANES = NSA_REP * Q_BLOCK
STEP_BLOCKS = 4
STEP_LANES = STEP_BLOCKS * QLANES
N_FORCED = 3
ONES_ROWS = 16
CMP_ROWS_STEP = 128
MASKED_MAX_FLOOR = -1e29


def _with_ones(vT):
    return jnp.concatenate([vT, jnp.ones((ONES_ROWS, vT.shape[1]), BF16)], axis=0)


def _attn_body(qT_ref, qrT_ref, kc_ref, vcT_ref, ksel_ref, vselT_ref, kwin_ref, vwinT_ref, gate_ref,
               ovT_ref, oh_ref, o_ref, qaug_ref, s0_ref, s1_ref, m_ref, acc_ref, part_ref, swin_ref,
               top0_ref, top1_ref):
    i = pl.program_id(2)
    t_step = i * (STEP_BLOCKS * Q_BLOCK)
    nblk = ovT_ref.shape[0]
    ncmp = kc_ref.shape[2]
    n_top = min(N_SEL, nblk) - N_FORCED

    def lanes_of_heads(ref, e):
        cols = slice(e * Q_BLOCK, (e + 1) * Q_BLOCK)
        parts = [ref[0, r * HEAD_DIM:(r + 1) * HEAD_DIM, cols] for r in range(NSA_REP)]
        return jnp.concatenate(
            [jnp.concatenate(parts, axis=1), jnp.zeros((LANES - HEAD_DIM, QLANES), BF16)], axis=0)

    def positions(e):
        return t_step + e * Q_BLOCK + lax.broadcasted_iota(jnp.int32, (1, Q_BLOCK), 1)

    pos_step = jnp.concatenate([positions(e) for e in range(STEP_BLOCKS) for _ in range(NSA_REP)], axis=1)
    d = HEAD_DIM

    def chunk_start(j):
        return pl.multiple_of(j * SEL_CHUNK, SEL_CHUNK)

    def scores_into(s_ref, top_ref, j):
        k0 = chunk_start(j)
        ka = jnp.concatenate([ksel_ref[0, pl.ds(k0, SEL_CHUNK), :], oh_ref[pl.ds(k0, SEL_CHUNK), :]], axis=1)
        s = _dot(ka, qaug_ref[...])
        s_ref[...] = s
        top_ref[...] = jnp.max(s, axis=0, keepdims=True)

    def block_select_and_window(e, cmp_rows, steady):
        t0 = t_step + e * Q_BLOCK
        lanes = slice(e * QLANES, (e + 1) * QLANES)
        qT = lanes_of_heads(qT_ref, e)
        qrT = lanes_of_heads(qrT_ref, e)
        pos1 = positions(e)
        pos = jnp.concatenate([pos1] * NSA_REP, axis=1)
        qaug_ref[0:LANES, lanes] = qrT
        s = _dot(kc_ref[0, 0, 0:cmp_rows, :], qT)
        w0 = pl.multiple_of(t0 - WINDOW if steady else jnp.maximum(t0 - WINDOW, 0), Q_BLOCK)
        swin_ref[e] = _dot(kwin_ref[0, pl.ds(w0, WIN_KEYS), :], qrT)
        s0_ref[:, lanes] = _dot(ksel_ref[0, 0:SEL_CHUNK, :], qrT)
        cend = lax.broadcasted_iota(jnp.int32, (cmp_rows, QLANES), 0) * CMP_STRIDE + (CMP_LEN - 1)
        s = jnp.where(cend <= pos, s, NEG_INF)
        m = jnp.maximum(jnp.max(s, axis=0, keepdims=True), MASKED_MAX_FLOOR)
        p = jnp.exp2(s - m)
        inv_l = 1.0 / jnp.maximum(jnp.sum(p, axis=0, keepdims=True), 1e-30)
        o_cmp = _dot(vcT_ref[0, 0, :, 0:cmp_rows], p.astype(BF16)) * inv_l

        pn = p * inv_l
        ph = pn[:, 0:Q_BLOCK]
        for r in range(1, NSA_REP):
            ph = ph + pn[:, r * Q_BLOCK:(r + 1) * Q_BLOCK]
        hi = ph.astype(BF16)
        lo = (ph - hi.astype(F32)).astype(BF16)
        ovT = ovT_ref[:, 0:cmp_rows]
        imp = _dot(ovT, hi) + _dot(ovT, lo)

        blk = lax.broadcasted_iota(jnp.int32, (nblk, Q_BLOCK), 0)
        cur = pos1 // SEL_BLOCK
        forced = (blk == 0) | (blk == cur) | (blk == cur - 1)
        v0 = jnp.where(forced, -1.0, jnp.where(blk * SEL_BLOCK <= pos1, imp, -1.0))
        n_piece = WIN_KEYS // Q_BLOCK
        kk = lax.broadcasted_iota(jnp.int32, (Q_BLOCK, QLANES), 0)
        qq = lax.broadcasted_iota(jnp.int32, (Q_BLOCK, QLANES), 1) % Q_BLOCK
        win = {"m": None, "acc": None}

        def piece_max(w):
            rows = slice(w * Q_BLOCK, (w + 1) * Q_BLOCK)
            sp = swin_ref[e, rows, :]
            if not steady:
                kpos = w0 + w * Q_BLOCK + kk
                sp = jnp.where(kpos <= pos, jnp.where(kpos > pos - WINDOW, sp, NEG_INF), NEG_INF)
            elif w == 0:
                sp = jnp.where(kk > qq, sp, NEG_INF)
            elif w == n_piece - 1:
                sp = jnp.where(kk <= qq, sp, NEG_INF)
            if not steady or w in (0, n_piece - 1):
                swin_ref[e, rows, :] = sp
            top = jnp.max(sp, axis=0, keepdims=True)
            win["m"] = top if win["m"] is None else jnp.maximum(win["m"], top)

        def piece_absorb(w):
            rows = slice(w * Q_BLOCK, (w + 1) * Q_BLOCK)
            p = jnp.exp2(swin_ref[e, rows, :] - win["m"]).astype(BF16)
            k0 = pl.multiple_of(w0 + w * Q_BLOCK, Q_BLOCK)
            pv = _dot(_with_ones(vwinT_ref[0, :, pl.ds(k0, Q_BLOCK)]), p)
            win["acc"] = pv if win["acc"] is None else win["acc"] + pv

        for w in range(n_piece):
            piece_max(w)
        left = v0
        pending = list(range(n_piece))
        for r in range(n_top):
            left = jnp.where(left == jnp.max(left, axis=0, keepdims=True), -2.0, left)
            if pending and r % 2 == 0:
                piece_absorb(pending.pop(0))
        for w in pending:
            piece_absorb(w)
        taken = left != v0
        n_taken = jnp.sum(jnp.where(taken, jnp.where(v0 >= 0.0, 1.0, 0.0), 0.0), axis=0, keepdims=True)

        def store_bias(left):
            bias = jnp.where(forced, 0.0, jnp.where(left != v0, 0.0, NEG_INF)).astype(BF16)
            bias = jnp.concatenate([bias] * NSA_REP, axis=1)
            if nblk < LANES:
                bias = jnp.concatenate([bias, jnp.zeros((LANES - nblk, QLANES), BF16)], axis=0)
            qaug_ref[LANES:2 * LANES, lanes] = bias

        store_bias(left)
        bias0 = jnp.where(forced[0:CHUNK_BLOCKS], 0.0,
                          jnp.where(left[0:CHUNK_BLOCKS] != v0[0:CHUNK_BLOCKS], 0.0, NEG_INF))
        bias0 = jnp.concatenate([bias0] * NSA_REP, axis=1)
        top = None
        for n in range(CHUNK_BLOCKS):
            rows = slice(n * SEL_BLOCK, (n + 1) * SEL_BLOCK)
            sb = s0_ref[rows, lanes] + bias0[n:n + 1, :]
            s0_ref[rows, lanes] = sb
            bt = jnp.max(sb, axis=0, keepdims=True)
            top = bt if top is None else jnp.maximum(top, bt)
        top0_ref[:, lanes] = top

        a_w = win["acc"]
        gate = gate_ref[0, 0, e]
        part_ref[:, lanes] = gate[0:1] * o_cmp + gate[2:3] * (a_w[0:d] / a_w[d:d + 1])

        def select_exactly():
            val = v0
            for _ in range(n_top):
                top = jnp.max(val, axis=0, keepdims=True)
                first = jnp.min(jnp.where(val == top, blk, nblk), axis=0, keepdims=True)
                val = jnp.where(blk == first, -2.0, val)
            store_bias(val)
            ka = jnp.concatenate([ksel_ref[0, 0:SEL_CHUNK, :], oh_ref[0:SEL_CHUNK, :]], axis=1)
            s = _dot(ka, qaug_ref[:, lanes])
            s0_ref[:, lanes] = s
            top0_ref[:, lanes] = jnp.max(s, axis=0, keepdims=True)

        return jnp.max(n_taken) > n_top, select_exactly

    def select_and_window(cmp_rows, steady):
        redo = [block_select_and_window(e, cmp_rows, steady) for e in range(STEP_BLOCKS)]
        for overflowed, select_exactly in redo:
            pl.when(overflowed)(select_exactly)

    step_tokens = STEP_BLOCKS * Q_BLOCK

    def visible_cmp(step):
        return (step * step_tokens + step_tokens - CMP_LEN) // CMP_STRIDE + 1

    n_steps = ksel_ref.shape[1] // step_tokens
    steady_from = -(-WINDOW // step_tokens)
    first_rows = min(CMP_ROWS_STEP, ncmp)
    assert visible_cmp(steady_from - 1) <= first_rows
    pl.when(i < steady_from)(functools.partial(select_and_window, first_rows, False))
    lo = steady_from
    for rows in range(first_rows, ncmp + 1, CMP_ROWS_STEP):
        hi = next((step for step in range(lo, n_steps) if visible_cmp(step) > rows), n_steps)
        if hi > lo:
            pl.when((i >= lo) & (i < hi))(functools.partial(select_and_window, rows, True))
            lo = hi
    assert lo == n_steps

    def absorb(s_ref, top_ref, j, diagonal):
        k0 = chunk_start(j)
        s = s_ref[...]
        if diagonal:
            kpos = k0 + lax.broadcasted_iota(jnp.int32, (SEL_CHUNK, STEP_LANES), 0)
            s = jnp.where(kpos <= pos_step, s, NEG_INF)
            top = jnp.max(s, axis=0, keepdims=True)
        else:
            top = top_ref[...]
        m = m_ref[...]
        m_new = jnp.maximum(m, top)
        p = jnp.exp2(s - m_new).astype(BF16)
        acc_ref[...] = jnp.exp2(m - m_new) * acc_ref[...] + _dot(
            _with_ones(vselT_ref[0, :, pl.ds(k0, SEL_CHUNK)]), p)
        m_ref[...] = m_new

    last = t_step // SEL_CHUNK
    m_ref[...] = jnp.full(m_ref.shape, NEG_INF, F32)
    acc_ref[...] = jnp.zeros(acc_ref.shape, F32)

    def pair(t):
        scores_into(s1_ref, top1_ref, 2 * t + 1)
        absorb(s0_ref, top0_ref, 2 * t, False)
        scores_into(s0_ref, top0_ref, 2 * t + 2)
        absorb(s1_ref, top1_ref, 2 * t + 1, False)

    def two_pairs(u, carry):
        pair(2 * u)
        pair(2 * u + 1)
        return carry

    n_pairs = last // 2
    lax.fori_loop(0, n_pairs // 2, two_pairs, 0)

    @pl.when(n_pairs % 2 == 1)
    def _():
        pair(n_pairs - 1)

    @pl.when(last % 2 == 1)
    def _():
        scores_into(s1_ref, top1_ref, last)
        absorb(s0_ref, top0_ref, last - 1, False)
        absorb(s1_ref, top1_ref, last, True)

    @pl.when(last % 2 == 0)
    def _():
        absorb(s0_ref, top0_ref, last, True)

    a_s = acc_ref[...]
    gate_sel = jnp.concatenate([gate_ref[0, 0, e][1:2] for e in range(STEP_BLOCKS)], axis=1)
    out = part_ref[...] + gate_sel * (a_s[0:d] / a_s[d:d + 1])
    for e in range(STEP_BLOCKS):
        for r in range(NSA_REP):
            lane0 = e * QLANES + r * Q_BLOCK
            o_ref[0, r * HEAD_DIM:(r + 1) * HEAD_DIM, e * Q_BLOCK:(e + 1) * Q_BLOCK] = (
                out[:, lane0:lane0 + Q_BLOCK].astype(BF16))


def _nsa_attention(qT, qrT, kcmp, vcmpT, ksel, vselT, kwin, vwinT, gates):
    b, _, s = qT.shape
    nq = s // Q_BLOCK
    nblk = s // SEL_BLOCK
    ncmp = kcmp.shape[2]
    rows = NSA_REP * HEAD_DIM
    assert nblk <= LANES and s % SEL_CHUNK == 0 and s >= WIN_KEYS

    gt = gates[:, :, :NSA_HEADS * N_GATES].reshape(b, nq, Q_BLOCK, NSA_KV_GROUPS, NSA_REP, N_GATES)
    gt = gt.transpose(0, 3, 1, 5, 4, 2).reshape(b, NSA_KV_GROUPS, nq, N_GATES, QLANES)

    cstart = np.arange(ncmp) * CMP_STRIDE
    sstart = np.arange(nblk) * SEL_BLOCK
    ovT = ((cstart[None, :] < sstart[:, None] + SEL_BLOCK) & (cstart[None, :] + CMP_LEN > sstart[:, None]))
    ovT = jnp.asarray(ovT, BF16)
    onehot = jnp.asarray(np.arange(s)[:, None] // SEL_BLOCK == np.arange(LANES)[None, :], BF16)

    assert nq % STEP_BLOCKS == 0
    q_spec = pl.BlockSpec((1, rows, STEP_BLOCKS * Q_BLOCK), lambda bi, g, i: (bi, g, i))
    k_spec = pl.BlockSpec((1, s, LANES), lambda bi, g, i: (bi, 0, g))
    vT_spec = pl.BlockSpec((1, HEAD_DIM, s), lambda bi, g, i: (bi, g, 0))
    return pl.pallas_call(
        _attn_body,
        grid=(b, NSA_KV_GROUPS, nq // STEP_BLOCKS),
        in_specs=[q_spec, q_spec,
                  pl.BlockSpec((1, 1, ncmp, LANES), lambda bi, g, i: (bi, g, 0, 0)),
                  pl.BlockSpec((1, 1, HEAD_DIM, ncmp), lambda bi, g, i: (bi, g, 0, 0)),
                  k_spec, vT_spec, k_spec, vT_spec,
                  pl.BlockSpec((1, 1, STEP_BLOCKS, N_GATES, QLANES), lambda bi, g, i: (bi, g, i, 0, 0)),
                  _const_spec((nblk, ncmp)), _const_spec((s, LANES))],
        out_specs=q_spec,
        out_shape=jax.ShapeDtypeStruct((b, NSA_HEADS * HEAD_DIM, s), BF16),
        scratch_shapes=[pltpu.VMEM((2 * LANES, STEP_LANES), BF16),
                        pltpu.VMEM((SEL_CHUNK, STEP_LANES), F32), pltpu.VMEM((SEL_CHUNK, STEP_LANES), F32),
                        pltpu.VMEM((1, STEP_LANES), F32), pltpu.VMEM((HEAD_DIM + ONES_ROWS, STEP_LANES), F32),
                        pltpu.VMEM((HEAD_DIM, STEP_LANES), F32), pltpu.VMEM((STEP_BLOCKS, WIN_KEYS, QLANES), F32),
                        pltpu.VMEM((1, STEP_LANES), F32), pltpu.VMEM((1, STEP_LANES), F32)],
        compiler_params=_params(("arbitrary", "arbitrary", "arbitrary")),
        name="nsa_attention",
    )(qT, qrT, kcmp, vcmpT, ksel, vselT, kwin, vwinT, gt, ovT, onehot)


def _nsa_mixer(x, g_pre, w_in, b_gate, cmp_pos, cmp_w1, cmp_w2):
    qT, qrT, ksel, kwin, vselT, vwinT, kc_raw, vc_raw, gates = _nsa_project(x, g_pre, w_in, b_gate)
    kcmp, vcmpT = _nsa_compress(kc_raw, vc_raw, cmp_pos, cmp_w1, cmp_w2)
    return _nsa_attention(qT, qrT, kcmp, vcmpT, ksel, vselT, kwin, vwinT, gates)


def kernel(x, mix_norm_pre, mix_norm_post, ffn_norm_pre, ffn_norm_post, ffn_w_gate, ffn_w_up, ffn_w_down,
           conv_w_pw1, conv_b_pw1, conv_w_dw, conv_b_dw, conv_ln_g, conv_ln_b, conv_w_pw2, conv_b_pw2,
           nsa_w_in, nsa_b_gate, nsa_cmp_pos, nsa_cmp_w1, nsa_cmp_w2, nsa_w_out):
    b, s, d = x.shape
    depth = mix_norm_pre.shape[0]
    n_mixers = 2

    def ffn(x, i, half, attn_out=None):
        y = _ffn_half(x.reshape(b * s, d), ffn_norm_pre[i, half], ffn_norm_post[i, half],
                      ffn_w_gate[i, half], ffn_w_up[i, half], ffn_w_down[i, half], attn_out)
        return y.reshape(b, s, d)

    for i in range(depth):
        x = ffn(x, i, 0)
        j = i // n_mixers
        if i % n_mixers == 0:
            x = _conv_mixer(x, mix_norm_pre[i], mix_norm_post[i], conv_w_pw1[j], conv_b_pw1[j], conv_w_dw[j],
                            conv_b_dw[j], conv_ln_g[j], conv_ln_b[j], conv_w_pw2[j], conv_b_pw2[j])
            x = ffn(x, i, 1)
        else:
            attnT = _nsa_mixer(x, mix_norm_pre[i], nsa_w_in[j], nsa_b_gate[j], nsa_cmp_pos[j],
                               nsa_cmp_w1[j], nsa_cmp_w2[j])
            x = ffn(x, i, 1, (attnT, nsa_w_out[j], mix_norm_post[i]))
    return x
```

```python
import functools
import math

import jax
import jax.numpy as jnp
import numpy as np
from jax import lax
from jax.experimental import pallas as pl
from jax.experimental.pallas import tpu as pltpu

RMS_EPS = 1e-6
LN_EPS = 1e-5
FFN_RESIDUAL_WEIGHT = 0.5
CONV_WIDTH = 31
NSA_HEADS = 16
NSA_KV_GROUPS = 4
NSA_REP = NSA_HEADS // NSA_KV_GROUPS
HEAD_DIM = 64
ROT_HALF = HEAD_DIM // 8
ROPE_THETA = 500000.0
CMP_LEN = 32
CMP_STRIDE = 16
SEL_BLOCK = 64
N_SEL = 16
WINDOW = 512
Q_BLOCK = 128
N_GATES = 3
NEG_INF = -1e30

LANES = 128
SUBLANES = 8
V7X_VMEM_LIMIT = 56 * 1024 * 1024

BF16 = jnp.bfloat16
F32 = jnp.float32


def _params(sem):
    return pltpu.CompilerParams(dimension_semantics=sem, vmem_limit_bytes=V7X_VMEM_LIMIT)


def _const_spec(shape):
    nd = len(shape)
    return pl.BlockSpec(shape, lambda *_: (0,) * nd, pipeline_mode=pl.Buffered(1))


def _rms(x, gain):
    return x * lax.rsqrt(jnp.mean(x * x, axis=-1, keepdims=True) + RMS_EPS) * gain


def _sigmoid(x):
    return 1.0 / (1.0 + jnp.exp(-x))


def _dot(a, b):
    return jnp.dot(a, b, preferred_element_type=F32)


def _dot_nt(a, b):
    return lax.dot_general(a, b, (((1,), (1,)), ((), ())), preferred_element_type=F32)


def _dot_tn(a, b):
    return lax.dot_general(a, b, (((0,), (0,)), ((), ())), preferred_element_type=F32)


FFN_ROWS = 512
FFN_CHUNK = 512


def _swiglu_half_step(x, gpre_ref, gpost_ref, wg_ref, wu_ref, wd_ref, o_ref):
    h = _rms(x, gpre_ref[...]).astype(BF16)
    d_ff = wd_ref.shape[0]
    acc = jnp.zeros(x.shape, F32)
    for c0 in range(0, d_ff, FFN_CHUNK):
        c1 = min(c0 + FFN_CHUNK, d_ff)
        g = _dot(h, wg_ref[:, c0:c1])
        u = _dot(h, wu_ref[:, c0:c1])
        a = (g * _sigmoid(g) * u).astype(BF16)
        acc = acc + _dot(a, wd_ref[c0:c1, :])
    o_ref[...] = x + FFN_RESIDUAL_WEIGHT * _rms(acc, gpost_ref[...])


def _ffn_body(x_ref, *refs):
    _swiglu_half_step(x_ref[...], *refs)


def _attn_out_ffn_body(x_ref, aT_ref, wo_ref, gmix_ref, *refs):
    x = x_ref[...] + _rms(_dot_tn(aT_ref[0], wo_ref[...]), gmix_ref[...])
    _swiglu_half_step(x, *refs)


def _ffn_half(x2, g_pre, g_post, w_gate, w_up, w_down, attn_out=None):
    t, d = x2.shape
    f = w_gate.shape[1]
    tm = min(FFN_ROWS, t)
    row = pl.BlockSpec((tm, d), lambda i: (i, 0))
    ffn_specs = [_const_spec((1, d)), _const_spec((1, d)),
                 _const_spec((d, f)), _const_spec((d, f)), _const_spec((f, d))]
    ffn_args = (g_pre.reshape(1, d), g_post.reshape(1, d),
                w_gate.astype(BF16), w_up.astype(BF16), w_down.astype(BF16))
    if attn_out is None:
        body, specs, args = _ffn_body, [row], (x2,)
    else:
        attnT, w_out, g_mix = attn_out
        tiles = attnT.shape[2] // tm
        body = _attn_out_ffn_body
        specs = [row, pl.BlockSpec((1, attnT.shape[1], tm), lambda i: (i // tiles, 0, i % tiles)),
                 _const_spec(w_out.shape), _const_spec((1, d))]
        args = (x2, attnT, w_out.astype(BF16), g_mix.reshape(1, d))
    return pl.pallas_call(
        body,
        grid=(t // tm,),
        in_specs=specs + ffn_specs,
        out_specs=row,
        out_shape=jax.ShapeDtypeStruct((t, d), F32),
        compiler_params=_params(("arbitrary",)),
        name="ffn_half",
    )(*args, *ffn_args)


CONV_ROWS = 256
CONV_HALO = 32
CONV_STRIP = 32


def _conv_body(x_ref, gpre_ref, w1_ref, b1_ref, wdw_ref, bdw_ref, lng_ref, lnb_ref,
               w2_ref, b2_ref, gpost_ref, o_ref, buf_ref, dw_ref, shift_ref):
    ts, d = x_ref.shape[1], x_ref.shape[2]

    @pl.when(pl.program_id(1) == 0)
    def _():
        buf_ref[0:CONV_HALO, :] = jnp.zeros((CONV_HALO, d), F32)

    x = x_ref[0]
    h = _rms(x, gpre_ref[...]).astype(BF16)
    p = _dot(h, w1_ref[...]) + b1_ref[...]
    buf_ref[CONV_HALO:CONV_HALO + ts, :] = p[:, :d] * _sigmoid(p[:, d:])

    base = CONV_HALO - (CONV_WIDTH - 1)
    span = shift_ref.shape[1]
    for b in range(1, SUBLANES):
        shift_ref[b - 1] = buf_ref[b:b + span, :]

    for r0 in range(0, ts, CONV_STRIP):
        acc = jnp.zeros((CONV_STRIP, d), F32)
        for k in range(CONV_WIDTH):
            a, b = divmod(base + k, SUBLANES)
            lo = r0 + a * SUBLANES
            win = buf_ref[lo:lo + CONV_STRIP, :] if b == 0 else shift_ref[b - 1, lo:lo + CONV_STRIP, :]
            acc = acc + wdw_ref[k:k + 1, :] * win
        dw_ref[r0:r0 + CONV_STRIP, :] = acc
    buf_ref[0:CONV_HALO, :] = buf_ref[ts:ts + CONV_HALO, :]

    c = dw_ref[...] + bdw_ref[...]
    mu = jnp.mean(c, axis=-1, keepdims=True)
    cc = c - mu
    var = jnp.mean(cc * cc, axis=-1, keepdims=True)
    y = cc * lax.rsqrt(var + LN_EPS) * lng_ref[...] + lnb_ref[...]
    y = (y * _sigmoid(y)).astype(BF16)
    out = _dot(y, w2_ref[...]) + b2_ref[...]
    o_ref[0] = x + _rms(out, gpost_ref[...])


def _conv_mixer(x, g_pre, g_post, w_pw1, b_pw1, w_dw, b_dw, ln_g, ln_b, w_pw2, b_pw2):
    b, s, d = x.shape
    ts = min(CONV_ROWS, s)
    tile = pl.BlockSpec((1, ts, d), lambda bi, j: (bi, j, 0))
    vec = lambda n: _const_spec((1, n))
    return pl.pallas_call(
        _conv_body,
        grid=(b, s // ts),
        in_specs=[tile, vec(d), _const_spec((d, 2 * d)), vec(2 * d), _const_spec((CONV_WIDTH, d)),
                  vec(d), vec(d), vec(d), _const_spec((d, d)), vec(d), vec(d)],
        out_specs=tile,
        out_shape=jax.ShapeDtypeStruct((b, s, d), F32),
        scratch_shapes=[pltpu.VMEM((ts + CONV_HALO, d), F32), pltpu.VMEM((ts, d), F32),
                        pltpu.VMEM((SUBLANES - 1, ts + (CONV_HALO - 1) // SUBLANES * SUBLANES, d), F32)],
        compiler_params=_params(("arbitrary", "arbitrary")),
        name="conv_mixer",
    )(x, g_pre.reshape(1, d), w_pw1.astype(BF16), b_pw1.reshape(1, 2 * d), w_dw,
      b_dw.reshape(1, d), ln_g.reshape(1, d), ln_b.reshape(1, d), w_pw2.astype(BF16),
      b_pw2.reshape(1, d), g_post.reshape(1, d))


PROJ_ROWS = 512
Q_WIDTH = NSA_HEADS * HEAD_DIM
KV_WIDTH = NSA_KV_GROUPS * HEAD_DIM
KPAD_WIDTH = NSA_KV_GROUPS * LANES
GATE_PAD = LANES
Q_SCALE = HEAD_DIM ** -0.5 * math.log2(math.e)


def _rope_tables(s):
    pos = jnp.arange(s, dtype=F32)
    inv_freq = ROPE_THETA ** (-jnp.arange(0, 2 * ROT_HALF, 2, dtype=F32) / (2 * ROT_HALF))
    ang = pos[:, None] * inv_freq[None, :]
    return jnp.cos(ang), jnp.sin(ang)


def _proj_body(x_ref, gpre_ref, wtok_ref, wtr_ref, bg_ref, cosT_ref, sinT_ref, ck_ref, s1_ref, s2_ref,
               qT_ref, qrT_ref, ksel_ref, kwin_ref, vselT_ref, vwinT_ref, kc_ref, vc_ref, gate_ref):
    tm = x_ref.shape[1]
    h = _rms(x_ref[0], gpre_ref[...]).astype(BF16)
    tok = _dot(h, wtok_ref[...])
    tr = _dot_nt(wtr_ref[...], h)

    q = tr[0:Q_WIDTH] * Q_SCALE
    qT_ref[0] = q.astype(BF16)
    q3 = q.reshape(NSA_HEADS, HEAD_DIM, tm)
    cos, sin = cosT_ref[...], sinT_ref[...]
    x1, x2 = q3[:, 0:ROT_HALF], q3[:, ROT_HALF:2 * ROT_HALF]
    qr = jnp.concatenate([x1 * cos - x2 * sin, x2 * cos + x1 * sin, q3[:, 2 * ROT_HALF:]], axis=1)
    qrT_ref[0] = qr.reshape(Q_WIDTH, tm).astype(BF16)
    vselT_ref[0] = tr[Q_WIDTH:Q_WIDTH + KV_WIDTH].astype(BF16)
    vwinT_ref[0] = tr[Q_WIDTH + KV_WIDTH:Q_WIDTH + 2 * KV_WIDTH].astype(BF16)

    ck, s1, s2 = ck_ref[...], s1_ref[...], s2_ref[...]
    for out_ref, base in ((ksel_ref, 0), (kwin_ref, KPAD_WIDTH)):
        for g in range(NSA_KV_GROUPS):
            k = tok[:, base + g * LANES:base + (g + 1) * LANES]
            kr = k * ck + pltpu.roll(k, ROT_HALF, 1) * s1 + pltpu.roll(k, LANES - ROT_HALF, 1) * s2
            out_ref[0, :, g * LANES:(g + 1) * LANES] = kr.astype(BF16)
    c0 = 2 * KPAD_WIDTH
    kc_ref[0] = tok[:, c0:c0 + KV_WIDTH]
    vc_ref[0] = tok[:, c0 + KV_WIDTH:c0 + 2 * KV_WIDTH]
    gate_ref[0] = _sigmoid(tok[:, c0 + 2 * KV_WIDTH:] + bg_ref[...])


def _nsa_project(x, g_pre, w_in, b_gate):
    b, s, d = x.shape
    tm = min(PROJ_ROWS, s)
    n_gate = NSA_HEADS * N_GATES
    w_q = w_in[:, :Q_WIDTH]
    w_kv = w_in[:, Q_WIDTH:Q_WIDTH + 6 * KV_WIDTH].reshape(d, 6, NSA_KV_GROUPS, HEAD_DIM)
    w_gate = w_in[:, Q_WIDTH + 6 * KV_WIDTH:]

    def padded(w):
        return jnp.pad(w, ((0, 0), (0, 0), (0, LANES - HEAD_DIM))).reshape(d, KPAD_WIDTH)

    w_tok = jnp.concatenate(
        [padded(w_kv[:, 2]), padded(w_kv[:, 4]), w_kv[:, 0].reshape(d, KV_WIDTH),
         w_kv[:, 1].reshape(d, KV_WIDTH), jnp.pad(w_gate, ((0, 0), (0, GATE_PAD - n_gate)))],
        axis=1).astype(BF16)
    w_tr = jnp.concatenate(
        [w_q, w_kv[:, 3].reshape(d, KV_WIDTH), w_kv[:, 5].reshape(d, KV_WIDTH)], axis=1).T.astype(BF16)
    bg = jnp.pad(b_gate, (0, GATE_PAD - n_gate)).reshape(1, GATE_PAD)

    cos, sin = _rope_tables(s)
    zeros = jnp.zeros_like(sin)
    pad_to = lambda parts, fill: jnp.concatenate(
        parts + [jnp.full((s, LANES - 2 * ROT_HALF), fill, F32)], axis=1)
    ck = pad_to([cos, cos], 1.0)
    s1 = pad_to([zeros, sin], 0.0)
    s2 = pad_to([-sin, zeros], 0.0)

    ntok, ntr = w_tok.shape[1], w_tr.shape[0]
    tok_major = lambda w: pl.BlockSpec((1, tm, w), lambda bi, j: (bi, j, 0))
    tr_major = lambda r: pl.BlockSpec((1, r, tm), lambda bi, j: (bi, 0, j))
    return pl.pallas_call(
        _proj_body,
        grid=(b, s // tm),
        in_specs=[tok_major(d), _const_spec((1, d)), _const_spec((d, ntok)), _const_spec((ntr, d)),
                  _const_spec((1, GATE_PAD)),
                  pl.BlockSpec((ROT_HALF, tm), lambda bi, j: (0, j)),
                  pl.BlockSpec((ROT_HALF, tm), lambda bi, j: (0, j)),
                  pl.BlockSpec((tm, LANES), lambda bi, j: (j, 0)),
                  pl.BlockSpec((tm, LANES), lambda bi, j: (j, 0)),
                  pl.BlockSpec((tm, LANES), lambda bi, j: (j, 0))],
        out_specs=[tr_major(Q_WIDTH), tr_major(Q_WIDTH), tok_major(KPAD_WIDTH), tok_major(KPAD_WIDTH),
                   tr_major(KV_WIDTH), tr_major(KV_WIDTH), tok_major(KV_WIDTH), tok_major(KV_WIDTH),
                   tok_major(GATE_PAD)],
        out_shape=[jax.ShapeDtypeStruct((b, Q_WIDTH, s), BF16), jax.ShapeDtypeStruct((b, Q_WIDTH, s), BF16),
                   jax.ShapeDtypeStruct((b, s, KPAD_WIDTH), BF16), jax.ShapeDtypeStruct((b, s, KPAD_WIDTH), BF16),
                   jax.ShapeDtypeStruct((b, KV_WIDTH, s), BF16), jax.ShapeDtypeStruct((b, KV_WIDTH, s), BF16),
                   jax.ShapeDtypeStruct((b, s, KV_WIDTH), F32), jax.ShapeDtypeStruct((b, s, KV_WIDTH), F32),
                   jax.ShapeDtypeStruct((b, s, GATE_PAD), F32)],
        compiler_params=_params(("arbitrary", "arbitrary")),
        name="nsa_project",
    )(x, g_pre.reshape(1, d), w_tok, w_tr, bg, cos.T, sin.T, ck, s1, s2)


def _compress_body(rk_ref, rv_ref, pos_ref, w1_ref, w2k_ref, w2vT_ref, kc_ref, vcT_ref):
    nrow = rk_ref.shape[2]
    half = rk_ref.shape[3]

    def hidden(r, t):
        top = _dot((r + pos_ref[t, 0:1, :]).astype(BF16), w1_ref[t, 0:half, :])
        bot = _dot((r + pos_ref[t, 1:2, :]).astype(BF16), w1_ref[t, half:2 * half, :])
        hid = top + pltpu.roll(bot, nrow - 1, 0)
        return (hid * _sigmoid(hid)).astype(BF16)

    kc_ref[0, 0] = _dot(hidden(rk_ref[0, 0], 0), w2k_ref[...]).astype(BF16)
    vcT_ref[0, 0] = _dot_nt(w2vT_ref[...], hidden(rv_ref[0, 0], 1)).astype(BF16)


def _nsa_compress(kc_raw, vc_raw, cmp_pos, cmp_w1, cmp_w2):
    b, s, _ = kc_raw.shape
    nrow = s // CMP_STRIDE
    half = CMP_STRIDE * HEAD_DIM
    hid = cmp_w1.shape[-1]

    def rows(raw):
        r = raw.reshape(b, nrow, CMP_STRIDE, NSA_KV_GROUPS, HEAD_DIM)
        return r.transpose(0, 3, 1, 2, 4).reshape(b, NSA_KV_GROUPS, nrow, half)

    pos = cmp_pos.reshape(2, 2, half)
    w2k = jnp.pad(cmp_w2[0], ((0, 0), (0, LANES - HEAD_DIM))).astype(BF16)
    w2vT = cmp_w2[1].T.astype(BF16)
    blk = pl.BlockSpec((1, 1, nrow, half), lambda bi, g: (bi, g, 0, 0))
    return pl.pallas_call(
        _compress_body,
        grid=(b, NSA_KV_GROUPS),
        in_specs=[blk, blk, _const_spec((2, 2, half)), _const_spec((2, 2 * half, hid)),
                  _const_spec((hid, LANES)), _const_spec((HEAD_DIM, hid))],
        out_specs=[pl.BlockSpec((1, 1, nrow, LANES), lambda bi, g: (bi, g, 0, 0)),
                   pl.BlockSpec((1, 1, HEAD_DIM, nrow), lambda bi, g: (bi, g, 0, 0))],
        out_shape=[jax.ShapeDtypeStruct((b, NSA_KV_GROUPS, nrow, LANES), BF16),
                   jax.ShapeDtypeStruct((b, NSA_KV_GROUPS, HEAD_DIM, nrow), BF16)],
        compiler_params=_params(("arbitrary", "arbitrary")),
        name="nsa_compress",
    )(rows(kc_raw), rows(vc_raw), pos, cmp_w1.astype(BF16), w2k, w2vT)


SEL_CHUNK = 512
CHUNK_BLOCKS = SEL_CHUNK // SEL_BLOCK
WIN_KEYS = WINDOW + Q_BLOCK
QLANES = NSA_REP * Q_BLOCK
STEP_BLOCKS = 4
STEP_LANES = STEP_BLOCKS * QLANES
N_FORCED = 3
ONES_ROWS = 16
CMP_ROWS_STEP = 256
MASKED_MAX_FLOOR = -1e29


def _with_ones(vT):
    return jnp.concatenate([vT, jnp.ones((ONES_ROWS, vT.shape[1]), BF16)], axis=0)


def _attn_body(qT_ref, qrT_ref, kc_ref, vcT_ref, ksel_ref, vselT_ref, kwin_ref, vwinT_ref, gate_ref,
               ovT_ref, oh_ref, o_ref, qaug_ref, s0_ref, s1_ref, m_ref, acc_ref, part_ref, swin_ref,
               top0_ref, top1_ref):
    i = pl.program_id(2)
    t_step = i * (STEP_BLOCKS * Q_BLOCK)
    nblk = ovT_ref.shape[0]
    ncmp = kc_ref.shape[2]
    n_top = min(N_SEL, nblk) - N_FORCED

    def lanes_of_heads(ref, e):
        cols = slice(e * Q_BLOCK, (e + 1) * Q_BLOCK)
        parts = [ref[0, r * HEAD_DIM:(r + 1) * HEAD_DIM, cols] for r in range(NSA_REP)]
        return jnp.concatenate(
            [jnp.concatenate(parts, axis=1), jnp.zeros((LANES - HEAD_DIM, QLANES), BF16)], axis=0)

    def positions(e):
        return t_step + e * Q_BLOCK + lax.broadcasted_iota(jnp.int32, (1, Q_BLOCK), 1)

    pos_step = jnp.concatenate([positions(e) for e in range(STEP_BLOCKS) for _ in range(NSA_REP)], axis=1)
    d = HEAD_DIM

    def chunk_start(j):
        return pl.multiple_of(j * SEL_CHUNK, SEL_CHUNK)

    def scores_into(s_ref, top_ref, j):
        k0 = chunk_start(j)
        ka = jnp.concatenate([ksel_ref[0, pl.ds(k0, SEL_CHUNK), :], oh_ref[pl.ds(k0, SEL_CHUNK), :]], axis=1)
        s = _dot(ka, qaug_ref[...])
        s_ref[...] = s
        top_ref[...] = jnp.max(s, axis=0, keepdims=True)

    def block_select_and_window(e, cmp_rows, steady):
        t0 = t_step + e * Q_BLOCK
        lanes = slice(e * QLANES, (e + 1) * QLANES)
        qT = lanes_of_heads(qT_ref, e)
        qrT = lanes_of_heads(qrT_ref, e)
        pos1 = positions(e)
        pos = jnp.concatenate([pos1] * NSA_REP, axis=1)
        qaug_ref[0:LANES, lanes] = qrT
        s = _dot(kc_ref[0, 0, 0:cmp_rows, :], qT)
        w0 = pl.multiple_of(t0 - WINDOW if steady else jnp.maximum(t0 - WINDOW, 0), Q_BLOCK)
        swin_ref[e] = _dot(kwin_ref[0, pl.ds(w0, WIN_KEYS), :], qrT)
        s0_ref[:, lanes] = _dot(ksel_ref[0, 0:SEL_CHUNK, :], qrT)
        cend = lax.broadcasted_iota(jnp.int32, (cmp_rows, QLANES), 0) * CMP_STRIDE + (CMP_LEN - 1)
        s = jnp.where(cend <= pos, s, NEG_INF)
        m = jnp.maximum(jnp.max(s, axis=0, keepdims=True), MASKED_MAX_FLOOR)
        p = jnp.exp2(s - m)
        inv_l = 1.0 / jnp.maximum(jnp.sum(p, axis=0, keepdims=True), 1e-30)
        o_cmp = _dot(vcT_ref[0, 0, :, 0:cmp_rows], p.astype(BF16)) * inv_l

        pn = p * inv_l
        ph = pn[:, 0:Q_BLOCK]
        for r in range(1, NSA_REP):
            ph = ph + pn[:, r * Q_BLOCK:(r + 1) * Q_BLOCK]
        hi = ph.astype(BF16)
        lo = (ph - hi.astype(F32)).astype(BF16)
        ovT = ovT_ref[:, 0:cmp_rows]
        imp = _dot(ovT, hi) + _dot(ovT, lo)

        blk = lax.broadcasted_iota(jnp.int32, (nblk, Q_BLOCK), 0)
        cur = pos1 // SEL_BLOCK
        forced = (blk == 0) | (blk == cur) | (blk == cur - 1)
        v0 = jnp.where(forced, -1.0, jnp.where(blk * SEL_BLOCK <= pos1, imp, -1.0))
        n_piece = WIN_KEYS // Q_BLOCK
        kk = lax.broadcasted_iota(jnp.int32, (Q_BLOCK, QLANES), 0)
        qq = lax.broadcasted_iota(jnp.int32, (Q_BLOCK, QLANES), 1) % Q_BLOCK
        win = {"m": None, "acc": None}

        def piece_max(w):
            rows = slice(w * Q_BLOCK, (w + 1) * Q_BLOCK)
            sp = swin_ref[e, rows, :]
            if not steady:
                kpos = w0 + w * Q_BLOCK + kk
                sp = jnp.where(kpos <= pos, jnp.where(kpos > pos - WINDOW, sp, NEG_INF), NEG_INF)
            elif w == 0:
                sp = jnp.where(kk > qq, sp, NEG_INF)
            elif w == n_piece - 1:
                sp = jnp.where(kk <= qq, sp, NEG_INF)
            if not steady or w in (0, n_piece - 1):
                swin_ref[e, rows, :] = sp
            top = jnp.max(sp, axis=0, keepdims=True)
            win["m"] = top if win["m"] is None else jnp.maximum(win["m"], top)

        def piece_absorb(w):
            rows = slice(w * Q_BLOCK, (w + 1) * Q_BLOCK)
            p = jnp.exp2(swin_ref[e, rows, :] - win["m"]).astype(BF16)
            k0 = pl.multiple_of(w0 + w * Q_BLOCK, Q_BLOCK)
            pv = _dot(_with_ones(vwinT_ref[0, :, pl.ds(k0, Q_BLOCK)]), p)
            win["acc"] = pv if win["acc"] is None else win["acc"] + pv

        for w in range(n_piece):
            piece_max(w)
        left = v0
        pending = list(range(n_piece))
        for r in range(n_top):
            left = jnp.where(left == jnp.max(left, axis=0, keepdims=True), -2.0, left)
            if pending and r % 2 == 0:
                piece_absorb(pending.pop(0))
        for w in pending:
            piece_absorb(w)
        taken = left != v0
        n_taken = jnp.sum(jnp.where(taken, jnp.where(v0 >= 0.0, 1.0, 0.0), 0.0), axis=0, keepdims=True)

        def store_bias(left):
            bias = jnp.where(forced, 0.0, jnp.where(left != v0, 0.0, NEG_INF)).astype(BF16)
            bias = jnp.concatenate([bias] * NSA_REP, axis=1)
            if nblk < LANES:
                bias = jnp.concatenate([bias, jnp.zeros((LANES - nblk, QLANES), BF16)], axis=0)
            qaug_ref[LANES:2 * LANES, lanes] = bias

        store_bias(left)
        bias0 = jnp.where(forced[0:CHUNK_BLOCKS], 0.0,
                          jnp.where(left[0:CHUNK_BLOCKS] != v0[0:CHUNK_BLOCKS], 0.0, NEG_INF))
        bias0 = jnp.concatenate([bias0] * NSA_REP, axis=1)
        top = None
        for n in range(CHUNK_BLOCKS):
            rows = slice(n * SEL_BLOCK, (n + 1) * SEL_BLOCK)
            sb = s0_ref[rows, lanes] + bias0[n:n + 1, :]
            s0_ref[rows, lanes] = sb
            bt = jnp.max(sb, axis=0, keepdims=True)
            top = bt if top is None else jnp.maximum(top, bt)
        top0_ref[:, lanes] = top

        a_w = win["acc"]
        gate = gate_ref[0, 0, e]
        part_ref[:, lanes] = gate[0:1] * o_cmp + gate[2:3] * (a_w[0:d] / a_w[d:d + 1])

        def select_exactly():
            def drop_first_max(_, val):
                top = jnp.max(val, axis=0, keepdims=True)
                first = jnp.min(jnp.where(val == top, blk, nblk), axis=0, keepdims=True)
                return jnp.where(blk == first, -2.0, val)

            store_bias(lax.fori_loop(0, n_top, drop_first_max, v0))

        return jnp.max(n_taken) > n_top, select_exactly

    def select_and_window(cmp_rows, steady):
        redo = [block_select_and_window(e, cmp_rows, steady) for e in range(STEP_BLOCKS)]
        any_overflowed = False
        for overflowed, select_exactly in redo:
            pl.when(overflowed)(select_exactly)
            any_overflowed = jnp.logical_or(any_overflowed, overflowed)
        pl.when(any_overflowed)(functools.partial(scores_into, s0_ref, top0_ref, 0))

    step_tokens = STEP_BLOCKS * Q_BLOCK

    def visible_cmp(step):
        return (step * step_tokens + step_tokens - CMP_LEN) // CMP_STRIDE + 1

    n_steps = ksel_ref.shape[1] // step_tokens
    steady_from = -(-WINDOW // step_tokens)
    first_rows = min(CMP_ROWS_STEP, ncmp)
    assert visible_cmp(steady_from - 1) <= first_rows
    pl.when(i < steady_from)(functools.partial(select_and_window, first_rows, False))
    lo = steady_from
    for rows in range(first_rows, ncmp + 1, CMP_ROWS_STEP):
        hi = next((step for step in range(lo, n_steps) if visible_cmp(step) > rows), n_steps)
        if hi > lo:
            pl.when((i >= lo) & (i < hi))(functools.partial(select_and_window, rows, True))
            lo = hi
    assert lo == n_steps

    def absorb(s_ref, top_ref, j, diagonal):
        k0 = chunk_start(j)
        s = s_ref[...]
        if diagonal:
            kpos = k0 + lax.broadcasted_iota(jnp.int32, (SEL_CHUNK, STEP_LANES), 0)
            s = jnp.where(kpos <= pos_step, s, NEG_INF)
            top = jnp.max(s, axis=0, keepdims=True)
        else:
            top = top_ref[...]
        m = m_ref[...]
        m_new = jnp.maximum(m, top)
        p = jnp.exp2(s - m_new).astype(BF16)
        acc_ref[...] = jnp.exp2(m - m_new) * acc_ref[...] + _dot(
            _with_ones(vselT_ref[0, :, pl.ds(k0, SEL_CHUNK)]), p)
        m_ref[...] = m_new

    last = t_step // SEL_CHUNK
    m_ref[...] = jnp.full(m_ref.shape, NEG_INF, F32)
    acc_ref[...] = jnp.zeros(acc_ref.shape, F32)

    def pair(t):
        scores_into(s1_ref, top1_ref, 2 * t + 1)
        absorb(s0_ref, top0_ref, 2 * t, False)
        scores_into(s0_ref, top0_ref, 2 * t + 2)
        absorb(s1_ref, top1_ref, 2 * t + 1, False)

    def pair_step(t, carry):
        pair(t)
        return carry

    lax.fori_loop(0, last // 2, pair_step, 0)

    @pl.when(last % 2 == 1)
    def _():
        scores_into(s1_ref, top1_ref, last)
        absorb(s0_ref, top0_ref, last - 1, False)
        absorb(s1_ref, top1_ref, last, True)

    @pl.when(last % 2 == 0)
    def _():
        absorb(s0_ref, top0_ref, last, True)

    a_s = acc_ref[...]
    gate_sel = jnp.concatenate([gate_ref[0, 0, e][1:2] for e in range(STEP_BLOCKS)], axis=1)
    out = part_ref[...] + gate_sel * (a_s[0:d] / a_s[d:d + 1])
    for e in range(STEP_BLOCKS):
        for r in range(NSA_REP):
            lane0 = e * QLANES + r * Q_BLOCK
            o_ref[0, r * HEAD_DIM:(r + 1) * HEAD_DIM, e * Q_BLOCK:(e + 1) * Q_BLOCK] = (
                out[:, lane0:lane0 + Q_BLOCK].astype(BF16))


def _nsa_attention(qT, qrT, kcmp, vcmpT, ksel, vselT, kwin, vwinT, gates):
    b, _, s = qT.shape
    nq = s // Q_BLOCK
    nblk = s // SEL_BLOCK
    ncmp = kcmp.shape[2]
    rows = NSA_REP * HEAD_DIM
    assert nblk <= LANES and s % SEL_CHUNK == 0 and s >= WIN_KEYS

    gt = gates[:, :, :NSA_HEADS * N_GATES].reshape(b, nq, Q_BLOCK, NSA_KV_GROUPS, NSA_REP, N_GATES)
    gt = gt.transpose(0, 3, 1, 5, 4, 2).reshape(b, NSA_KV_GROUPS, nq, N_GATES, QLANES)

    cstart = np.arange(ncmp) * CMP_STRIDE
    sstart = np.arange(nblk) * SEL_BLOCK
    ovT = ((cstart[None, :] < sstart[:, None] + SEL_BLOCK) & (cstart[None, :] + CMP_LEN > sstart[:, None]))
    ovT = jnp.asarray(ovT, BF16)
    onehot = jnp.asarray(np.arange(s)[:, None] // SEL_BLOCK == np.arange(LANES)[None, :], BF16)

    assert nq % STEP_BLOCKS == 0
    q_spec = pl.BlockSpec((1, rows, STEP_BLOCKS * Q_BLOCK), lambda bi, g, i: (bi, g, i))
    k_spec = pl.BlockSpec((1, s, LANES), lambda bi, g, i: (bi, 0, g))
    vT_spec = pl.BlockSpec((1, HEAD_DIM, s), lambda bi, g, i: (bi, g, 0))
    return pl.pallas_call(
        _attn_body,
        grid=(b, NSA_KV_GROUPS, nq // STEP_BLOCKS),
        in_specs=[q_spec, q_spec,
                  pl.BlockSpec((1, 1, ncmp, LANES), lambda bi, g, i: (bi, g, 0, 0)),
                  pl.BlockSpec((1, 1, HEAD_DIM, ncmp), lambda bi, g, i: (bi, g, 0, 0)),
                  k_spec, vT_spec, k_spec, vT_spec,
                  pl.BlockSpec((1, 1, STEP_BLOCKS, N_GATES, QLANES), lambda bi, g, i: (bi, g, i, 0, 0)),
                  _const_spec((nblk, ncmp)), _const_spec((s, LANES))],
        out_specs=q_spec,
        out_shape=jax.ShapeDtypeStruct((b, NSA_HEADS * HEAD_DIM, s), BF16),
        scratch_shapes=[pltpu.VMEM((2 * LANES, STEP_LANES), BF16),
                        pltpu.VMEM((SEL_CHUNK, STEP_LANES), F32), pltpu.VMEM((SEL_CHUNK, STEP_LANES), F32),
                        pltpu.VMEM((1, STEP_LANES), F32), pltpu.VMEM((HEAD_DIM + ONES_ROWS, STEP_LANES), F32),
                        pltpu.VMEM((HEAD_DIM, STEP_LANES), F32), pltpu.VMEM((STEP_BLOCKS, WIN_KEYS, QLANES), F32),
                        pltpu.VMEM((1, STEP_LANES), F32), pltpu.VMEM((1, STEP_LANES), F32)],
        compiler_params=_params(("arbitrary", "arbitrary", "arbitrary")),
        name="nsa_attention",
    )(qT, qrT, kcmp, vcmpT, ksel, vselT, kwin, vwinT, gt, ovT, onehot)


def _nsa_mixer(x, g_pre, w_in, b_gate, cmp_pos, cmp_w1, cmp_w2):
    qT, qrT, ksel, kwin, vselT, vwinT, kc_raw, vc_raw, gates = _nsa_project(x, g_pre, w_in, b_gate)
    kcmp, vcmpT = _nsa_compress(kc_raw, vc_raw, cmp_pos, cmp_w1, cmp_w2)
    return _nsa_attention(qT, qrT, kcmp, vcmpT, ksel, vselT, kwin, vwinT, gates)


def kernel(x, mix_norm_pre, mix_norm_post, ffn_norm_pre, ffn_norm_post, ffn_w_gate, ffn_w_up, ffn_w_down,
           conv_w_pw1, conv_b_pw1, conv_w_dw, conv_b_dw, conv_ln_g, conv_ln_b, conv_w_pw2, conv_b_pw2,
           nsa_w_in, nsa_b_gate, nsa_cmp_pos, nsa_cmp_w1, nsa_cmp_w2, nsa_w_out):
    b, s, d = x.shape
    depth = mix_norm_pre.shape[0]
    n_mixers = 2

    def ffn(x, i, half, attn_out=None):
        y = _ffn_half(x.reshape(b * s, d), ffn_norm_pre[i, half], ffn_norm_post[i, half],
                      ffn_w_gate[i, half], ffn_w_up[i, half], ffn_w_down[i, half], attn_out)
        return y.reshape(b, s, d)

    for i in range(depth):
        x = ffn(x, i, 0)
        j = i // n_mixers
        if i % n_mixers == 0:
            x = _conv_mixer(x, mix_norm_pre[i], mix_norm_post[i], conv_w_pw1[j], conv_b_pw1[j], conv_w_dw[j],
                            conv_b_dw[j], conv_ln_g[j], conv_ln_b[j], conv_w_pw2[j], conv_b_pw2[j])
            x = ffn(x, i, 1)
        else:
            attnT = _nsa_mixer(x, mix_norm_pre[i], nsa_w_in[j], nsa_b_gate[j], nsa_cmp_pos[j],
                               nsa_cmp_w1[j], nsa_cmp_w2[j])
            x = ffn(x, i, 1, (attnT, nsa_w_out[j], mix_norm_post[i]))
    return x
```

```python
import functools
import math

import jax
import jax.numpy as jnp
import numpy as np
from jax import lax
from jax.experimental import pallas as pl
from jax.experimental.pallas import tpu as pltpu

RMS_EPS = 1e-6
LN_EPS = 1e-5
FFN_RESIDUAL_WEIGHT = 0.5
CONV_WIDTH = 31
NSA_HEADS = 16
NSA_KV_GROUPS = 4
NSA_REP = NSA_HEADS // NSA_KV_GROUPS
HEAD_DIM = 64
ROT_HALF = HEAD_DIM // 8
ROPE_THETA = 500000.0
CMP_LEN = 32
CMP_STRIDE = 16
SEL_BLOCK = 64
N_SEL = 16
WINDOW = 512
Q_BLOCK = 128
N_GATES = 3
NEG_INF = -1e30

LANES = 128
SUBLANES = 8
V7X_VMEM_LIMIT = 56 * 1024 * 1024

BF16 = jnp.bfloat16
F32 = jnp.float32


def _params(sem):
    return pltpu.CompilerParams(dimension_semantics=sem, vmem_limit_bytes=V7X_VMEM_LIMIT)


def _const_spec(shape):
    nd = len(shape)
    return pl.BlockSpec(shape, lambda *_: (0,) * nd, pipeline_mode=pl.Buffered(1))


def _rms(x, gain):
    return x * lax.rsqrt(jnp.mean(x * x, axis=-1, keepdims=True) + RMS_EPS) * gain


def _sigmoid(x):
    return 1.0 / (1.0 + jnp.exp(-x))


def _dot(a, b):
    return jnp.dot(a, b, preferred_element_type=F32)


def _dot_nt(a, b):
    return lax.dot_general(a, b, (((1,), (1,)), ((), ())), preferred_element_type=F32)


def _dot_tn(a, b):
    return lax.dot_general(a, b, (((0,), (0,)), ((), ())), preferred_element_type=F32)


FFN_ROWS = 512
FFN_CHUNK = 512


def _swiglu_half_step(x, gpre_ref, gpost_ref, wg_ref, wu_ref, wd_ref, o_ref):
    h = _rms(x, gpre_ref[...]).astype(BF16)
    d_ff = wd_ref.shape[0]
    acc = jnp.zeros(x.shape, F32)
    for c0 in range(0, d_ff, FFN_CHUNK):
        c1 = min(c0 + FFN_CHUNK, d_ff)
        g = _dot(h, wg_ref[:, c0:c1])
        u = _dot(h, wu_ref[:, c0:c1])
        a = (g * _sigmoid(g) * u).astype(BF16)
        acc = acc + _dot(a, wd_ref[c0:c1, :])
    o_ref[...] = x + FFN_RESIDUAL_WEIGHT * _rms(acc, gpost_ref[...])


def _ffn_body(x_ref, *refs):
    _swiglu_half_step(x_ref[...], *refs)


def _attn_out_ffn_body(x_ref, aT_ref, wo_ref, gmix_ref, *refs):
    x = x_ref[...] + _rms(_dot_tn(aT_ref[0], wo_ref[...]), gmix_ref[...])
    _swiglu_half_step(x, *refs)


def _ffn_half(x2, g_pre, g_post, w_gate, w_up, w_down, attn_out=None):
    t, d = x2.shape
    f = w_gate.shape[1]
    tm = min(FFN_ROWS, t)
    row = pl.BlockSpec((tm, d), lambda i: (i, 0))
    ffn_specs = [_const_spec((1, d)), _const_spec((1, d)),
                 _const_spec((d, f)), _const_spec((d, f)), _const_spec((f, d))]
    ffn_args = (g_pre.reshape(1, d), g_post.reshape(1, d),
                w_gate.astype(BF16), w_up.astype(BF16), w_down.astype(BF16))
    if attn_out is None:
        body, specs, args = _ffn_body, [row], (x2,)
    else:
        attnT, w_out, g_mix = attn_out
        tiles = attnT.shape[2] // tm
        body = _attn_out_ffn_body
        specs = [row, pl.BlockSpec((1, attnT.shape[1], tm), lambda i: (i // tiles, 0, i % tiles)),
                 _const_spec(w_out.shape), _const_spec((1, d))]
        args = (x2, attnT, w_out.astype(BF16), g_mix.reshape(1, d))
    return pl.pallas_call(
        body,
        grid=(t // tm,),
        in_specs=specs + ffn_specs,
        out_specs=row,
        out_shape=jax.ShapeDtypeStruct((t, d), F32),
        compiler_params=_params(("arbitrary",)),
        name="ffn_half",
    )(*args, *ffn_args)


CONV_ROWS = 256
CONV_HALO = 32
CONV_STRIP = 32


def _conv_body(x_ref, gpre_ref, w1_ref, b1_ref, wdw_ref, bdw_ref, lng_ref, lnb_ref,
               w2_ref, b2_ref, gpost_ref, o_ref, buf_ref, dw_ref, shift_ref):
    ts, d = x_ref.shape[1], x_ref.shape[2]

    @pl.when(pl.program_id(1) == 0)
    def _():
        buf_ref[0:CONV_HALO, :] = jnp.zeros((CONV_HALO, d), F32)

    x = x_ref[0]
    h = _rms(x, gpre_ref[...]).astype(BF16)
    p = _dot(h, w1_ref[...]) + b1_ref[...]
    buf_ref[CONV_HALO:CONV_HALO + ts, :] = p[:, :d] * _sigmoid(p[:, d:])

    base = CONV_HALO - (CONV_WIDTH - 1)
    span = shift_ref.shape[1]
    for b in range(1, SUBLANES):
        shift_ref[b - 1] = buf_ref[b:b + span, :]

    for r0 in range(0, ts, CONV_STRIP):
        acc = jnp.zeros((CONV_STRIP, d), F32)
        for k in range(CONV_WIDTH):
            a, b = divmod(base + k, SUBLANES)
            lo = r0 + a * SUBLANES
            win = buf_ref[lo:lo + CONV_STRIP, :] if b == 0 else shift_ref[b - 1, lo:lo + CONV_STRIP, :]
            acc = acc + wdw_ref[k:k + 1, :] * win
        dw_ref[r0:r0 + CONV_STRIP, :] = acc
    buf_ref[0:CONV_HALO, :] = buf_ref[ts:ts + CONV_HALO, :]

    c = dw_ref[...] + bdw_ref[...]
    mu = jnp.mean(c, axis=-1, keepdims=True)
    cc = c - mu
    var = jnp.mean(cc * cc, axis=-1, keepdims=True)
    y = cc * lax.rsqrt(var + LN_EPS) * lng_ref[...] + lnb_ref[...]
    y = (y * _sigmoid(y)).astype(BF16)
    out = _dot(y, w2_ref[...]) + b2_ref[...]
    o_ref[0] = x + _rms(out, gpost_ref[...])


def _conv_mixer(x, g_pre, g_post, w_pw1, b_pw1, w_dw, b_dw, ln_g, ln_b, w_pw2, b_pw2):
    b, s, d = x.shape
    ts = min(CONV_ROWS, s)
    tile = pl.BlockSpec((1, ts, d), lambda bi, j: (bi, j, 0))
    vec = lambda n: _const_spec((1, n))
    return pl.pallas_call(
        _conv_body,
        grid=(b, s // ts),
        in_specs=[tile, vec(d), _const_spec((d, 2 * d)), vec(2 * d), _const_spec((CONV_WIDTH, d)),
                  vec(d), vec(d), vec(d), _const_spec((d, d)), vec(d), vec(d)],
        out_specs=tile,
        out_shape=jax.ShapeDtypeStruct((b, s, d), F32),
        scratch_shapes=[pltpu.VMEM((ts + CONV_HALO, d), F32), pltpu.VMEM((ts, d), F32),
                        pltpu.VMEM((SUBLANES - 1, ts + (CONV_HALO - 1) // SUBLANES * SUBLANES, d), F32)],
        compiler_params=_params(("arbitrary", "arbitrary")),
        name="conv_mixer",
    )(x, g_pre.reshape(1, d), w_pw1.astype(BF16), b_pw1.reshape(1, 2 * d), w_dw,
      b_dw.reshape(1, d), ln_g.reshape(1, d), ln_b.reshape(1, d), w_pw2.astype(BF16),
      b_pw2.reshape(1, d), g_post.reshape(1, d))


PROJ_ROWS = 512
Q_WIDTH = NSA_HEADS * HEAD_DIM
KV_WIDTH = NSA_KV_GROUPS * HEAD_DIM
KPAD_WIDTH = NSA_KV_GROUPS * LANES
GATE_PAD = LANES
Q_SCALE = HEAD_DIM ** -0.5 * math.log2(math.e)


def _rope_tables(s):
    pos = jnp.arange(s, dtype=F32)
    inv_freq = ROPE_THETA ** (-jnp.arange(0, 2 * ROT_HALF, 2, dtype=F32) / (2 * ROT_HALF))
    ang = pos[:, None] * inv_freq[None, :]
    return jnp.cos(ang), jnp.sin(ang)


def _proj_body(x_ref, gpre_ref, wtok_ref, wtr_ref, bg_ref, cosT_ref, sinT_ref, ck_ref, s1_ref, s2_ref,
               qT_ref, qrT_ref, ksel_ref, kwin_ref, vselT_ref, vwinT_ref, kc_ref, vc_ref, gate_ref, raw_ref):
    tm = x_ref.shape[1]
    h = _rms(x_ref[0], gpre_ref[...]).astype(BF16)
    tok = _dot(h, wtok_ref[...])
    tr = _dot_nt(wtr_ref[...], h)

    q = tr[0:Q_WIDTH] * Q_SCALE
    qT_ref[0] = q.astype(BF16)
    q3 = q.reshape(NSA_HEADS, HEAD_DIM, tm)
    cos, sin = cosT_ref[...], sinT_ref[...]
    x1, x2 = q3[:, 0:ROT_HALF], q3[:, ROT_HALF:2 * ROT_HALF]
    qr = jnp.concatenate([x1 * cos - x2 * sin, x2 * cos + x1 * sin, q3[:, 2 * ROT_HALF:]], axis=1)
    qrT_ref[0] = qr.reshape(Q_WIDTH, tm).astype(BF16)
    vselT_ref[0] = tr[Q_WIDTH:Q_WIDTH + KV_WIDTH].astype(BF16)
    vwinT_ref[0] = tr[Q_WIDTH + KV_WIDTH:Q_WIDTH + 2 * KV_WIDTH].astype(BF16)

    ck, s1, s2 = ck_ref[...], s1_ref[...], s2_ref[...]
    for out_ref, base in ((ksel_ref, 0), (kwin_ref, KPAD_WIDTH)):
        for g in range(NSA_KV_GROUPS):
            k = tok[:, base + g * LANES:base + (g + 1) * LANES]
            kr = k * ck + pltpu.roll(k, ROT_HALF, 1) * s1 + pltpu.roll(k, LANES - ROT_HALF, 1) * s2
            out_ref[0, :, g * LANES:(g + 1) * LANES] = kr.astype(BF16)
    c0 = 2 * KPAD_WIDTH
    gate_ref[0] = _sigmoid(tok[:, c0 + 2 * KV_WIDTH:] + bg_ref[...])

    tiles = KV_WIDTH // LANES
    for t in range(2 * tiles):
        raw_ref[t] = tok[:, c0 + t * LANES:c0 + (t + 1) * LANES]
    nrow = tm // CMP_STRIDE
    low_half = lax.broadcasted_iota(jnp.int32, (nrow, LANES), 1) < HEAD_DIM
    for out_ref, base in ((kc_ref, 0), (vc_ref, tiles)):
        for h in range(tiles):
            for i in range(CMP_STRIDE // 2):
                a0 = raw_ref[base + h, pl.ds(2 * i, nrow, stride=CMP_STRIDE), :]
                a1 = raw_ref[base + h, pl.ds(2 * i + 1, nrow, stride=CMP_STRIDE), :]
                cols = slice(i * LANES, (i + 1) * LANES)
                out_ref[0, 2 * h, :, cols] = jnp.where(low_half, a0, pltpu.roll(a1, HEAD_DIM, 1))
                out_ref[0, 2 * h + 1, :, cols] = jnp.where(low_half, pltpu.roll(a0, HEAD_DIM, 1), a1)


def _nsa_project(x, g_pre, w_in, b_gate):
    b, s, d = x.shape
    tm = min(PROJ_ROWS, s)
    n_gate = NSA_HEADS * N_GATES
    w_q = w_in[:, :Q_WIDTH]
    w_kv = w_in[:, Q_WIDTH:Q_WIDTH + 6 * KV_WIDTH].reshape(d, 6, NSA_KV_GROUPS, HEAD_DIM)
    w_gate = w_in[:, Q_WIDTH + 6 * KV_WIDTH:]

    def padded(w):
        return jnp.pad(w, ((0, 0), (0, 0), (0, LANES - HEAD_DIM))).reshape(d, KPAD_WIDTH)

    w_tok = jnp.concatenate(
        [padded(w_kv[:, 2]), padded(w_kv[:, 4]), w_kv[:, 0].reshape(d, KV_WIDTH),
         w_kv[:, 1].reshape(d, KV_WIDTH), jnp.pad(w_gate, ((0, 0), (0, GATE_PAD - n_gate)))],
        axis=1).astype(BF16)
    w_tr = jnp.concatenate(
        [w_q, w_kv[:, 3].reshape(d, KV_WIDTH), w_kv[:, 5].reshape(d, KV_WIDTH)], axis=1).T.astype(BF16)
    bg = jnp.pad(b_gate, (0, GATE_PAD - n_gate)).reshape(1, GATE_PAD)

    cos, sin = _rope_tables(s)
    zeros = jnp.zeros_like(sin)
    pad_to = lambda parts, fill: jnp.concatenate(
        parts + [jnp.full((s, LANES - 2 * ROT_HALF), fill, F32)], axis=1)
    ck = pad_to([cos, cos], 1.0)
    s1 = pad_to([zeros, sin], 0.0)
    s2 = pad_to([-sin, zeros], 0.0)

    ntok, ntr = w_tok.shape[1], w_tr.shape[0]
    tok_major = lambda w: pl.BlockSpec((1, tm, w), lambda bi, j: (bi, j, 0))
    tr_major = lambda r: pl.BlockSpec((1, r, tm), lambda bi, j: (bi, 0, j))
    cmp_shape = (b, NSA_KV_GROUPS, s // CMP_STRIDE, CMP_STRIDE * HEAD_DIM)
    cmp_rows = pl.BlockSpec((1, NSA_KV_GROUPS, tm // CMP_STRIDE, CMP_STRIDE * HEAD_DIM),
                            lambda bi, j: (bi, 0, j, 0))
    return pl.pallas_call(
        _proj_body,
        grid=(b, s // tm),
        in_specs=[tok_major(d), _const_spec((1, d)), _const_spec((d, ntok)), _const_spec((ntr, d)),
                  _const_spec((1, GATE_PAD)),
                  pl.BlockSpec((ROT_HALF, tm), lambda bi, j: (0, j)),
                  pl.BlockSpec((ROT_HALF, tm), lambda bi, j: (0, j)),
                  pl.BlockSpec((tm, LANES), lambda bi, j: (j, 0)),
                  pl.BlockSpec((tm, LANES), lambda bi, j: (j, 0)),
                  pl.BlockSpec((tm, LANES), lambda bi, j: (j, 0))],
        out_specs=[tr_major(Q_WIDTH), tr_major(Q_WIDTH), tok_major(KPAD_WIDTH), tok_major(KPAD_WIDTH),
                   tr_major(KV_WIDTH), tr_major(KV_WIDTH), cmp_rows, cmp_rows,
                   tok_major(GATE_PAD)],
        out_shape=[jax.ShapeDtypeStruct((b, Q_WIDTH, s), BF16), jax.ShapeDtypeStruct((b, Q_WIDTH, s), BF16),
                   jax.ShapeDtypeStruct((b, s, KPAD_WIDTH), BF16), jax.ShapeDtypeStruct((b, s, KPAD_WIDTH), BF16),
                   jax.ShapeDtypeStruct((b, KV_WIDTH, s), BF16), jax.ShapeDtypeStruct((b, KV_WIDTH, s), BF16),
                   jax.ShapeDtypeStruct(cmp_shape, F32), jax.ShapeDtypeStruct(cmp_shape, F32),
                   jax.ShapeDtypeStruct((b, s, GATE_PAD), F32)],
        scratch_shapes=[pltpu.VMEM((2 * KV_WIDTH // LANES, tm, LANES), F32)],
        compiler_params=_params(("arbitrary", "arbitrary")),
        name="nsa_project",
    )(x, g_pre.reshape(1, d), w_tok, w_tr, bg, cos.T, sin.T, ck, s1, s2)


def _compress_body(rk_ref, rv_ref, pos_ref, w1_ref, w2k_ref, w2vT_ref, kc_ref, vcT_ref):
    nrow = rk_ref.shape[2]
    half = rk_ref.shape[3]

    def hidden(r, t):
        top = _dot((r + pos_ref[t, 0:1, :]).astype(BF16), w1_ref[t, 0:half, :])
        bot = _dot((r + pos_ref[t, 1:2, :]).astype(BF16), w1_ref[t, half:2 * half, :])
        hid = top + pltpu.roll(bot, nrow - 1, 0)
        return (hid * _sigmoid(hid)).astype(BF16)

    kc_ref[0, 0] = _dot(hidden(rk_ref[0, 0], 0), w2k_ref[...]).astype(BF16)
    vcT_ref[0, 0] = _dot_nt(w2vT_ref[...], hidden(rv_ref[0, 0], 1)).astype(BF16)


def _nsa_compress(kc_raw, vc_raw, cmp_pos, cmp_w1, cmp_w2):
    b, _, nrow, half = kc_raw.shape
    hid = cmp_w1.shape[-1]
    pos = cmp_pos.reshape(2, 2, half)
    w2k = jnp.pad(cmp_w2[0], ((0, 0), (0, LANES - HEAD_DIM))).astype(BF16)
    w2vT = cmp_w2[1].T.astype(BF16)
    blk = pl.BlockSpec((1, 1, nrow, half), lambda bi, g: (bi, g, 0, 0))
    return pl.pallas_call(
        _compress_body,
        grid=(b, NSA_KV_GROUPS),
        in_specs=[blk, blk, _const_spec((2, 2, half)), _const_spec((2, 2 * half, hid)),
                  _const_spec((hid, LANES)), _const_spec((HEAD_DIM, hid))],
        out_specs=[pl.BlockSpec((1, 1, nrow, LANES), lambda bi, g: (bi, g, 0, 0)),
                   pl.BlockSpec((1, 1, HEAD_DIM, nrow), lambda bi, g: (bi, g, 0, 0))],
        out_shape=[jax.ShapeDtypeStruct((b, NSA_KV_GROUPS, nrow, LANES), BF16),
                   jax.ShapeDtypeStruct((b, NSA_KV_GROUPS, HEAD_DIM, nrow), BF16)],
        compiler_params=_params(("arbitrary", "arbitrary")),
        name="nsa_compress",
    )(kc_raw, vc_raw, pos, cmp_w1.astype(BF16), w2k, w2vT)


SEL_CHUNK = 512
CHUNK_BLOCKS = SEL_CHUNK // SEL_BLOCK
WIN_KEYS = WINDOW + Q_BLOCK
QLANES = NSA_REP * Q_BLOCK
STEP_BLOCKS = 4
STEP_LANES = STEP_BLOCKS * QLANES
N_FORCED = 3
ONES_ROWS = 16
CMP_ROWS_STEP = 256
MASKED_MAX_FLOOR = -1e29


def _with_ones(vT):
    return jnp.concatenate([vT, jnp.ones((ONES_ROWS, vT.shape[1]), BF16)], axis=0)


def _attn_body(qT_ref, qrT_ref, kc_ref, vcT_ref, ksel_ref, vselT_ref, kwin_ref, vwinT_ref, gate_ref,
               ovT_ref, oh_ref, o_ref, qaug_ref, s0_ref, s1_ref, m_ref, acc_ref, part_ref, swin_ref,
               top0_ref, top1_ref):
    i = pl.program_id(2)
    t_step = i * (STEP_BLOCKS * Q_BLOCK)
    nblk = ovT_ref.shape[0]
    ncmp = kc_ref.shape[2]
    n_top = min(N_SEL, nblk) - N_FORCED

    def lanes_of_heads(ref, e):
        cols = slice(e * Q_BLOCK, (e + 1) * Q_BLOCK)
        parts = [ref[0, r * HEAD_DIM:(r + 1) * HEAD_DIM, cols] for r in range(NSA_REP)]
        return jnp.concatenate(
            [jnp.concatenate(parts, axis=1), jnp.zeros((LANES - HEAD_DIM, QLANES), BF16)], axis=0)

    def positions(e):
        return t_step + e * Q_BLOCK + lax.broadcasted_iota(jnp.int32, (1, Q_BLOCK), 1)

    pos_step = jnp.concatenate([positions(e) for e in range(STEP_BLOCKS) for _ in range(NSA_REP)], axis=1)
    d = HEAD_DIM

    def chunk_start(j):
        return pl.multiple_of(j * SEL_CHUNK, SEL_CHUNK)

    def scores_into(s_ref, top_ref, j):
        k0 = chunk_start(j)
        ka = jnp.concatenate([ksel_ref[0, pl.ds(k0, SEL_CHUNK), :], oh_ref[pl.ds(k0, SEL_CHUNK), :]], axis=1)
        s = _dot(ka, qaug_ref[...])
        s_ref[...] = s
        top_ref[...] = jnp.max(s, axis=0, keepdims=True)

    def block_select_and_window(e, cmp_rows, steady):
        t0 = t_step + e * Q_BLOCK
        lanes = slice(e * QLANES, (e + 1) * QLANES)
        qT = lanes_of_heads(qT_ref, e)
        qrT = lanes_of_heads(qrT_ref, e)
        pos1 = positions(e)
        pos = jnp.concatenate([pos1] * NSA_REP, axis=1)
        qaug_ref[0:LANES, lanes] = qrT
        s = _dot(kc_ref[0, 0, 0:cmp_rows, :], qT)
        w0 = pl.multiple_of(t0 - WINDOW if steady else jnp.maximum(t0 - WINDOW, 0), Q_BLOCK)
        swin_ref[e] = _dot(kwin_ref[0, pl.ds(w0, WIN_KEYS), :], qrT)
        s0_ref[:, lanes] = _dot(ksel_ref[0, 0:SEL_CHUNK, :], qrT)
        cend = lax.broadcasted_iota(jnp.int32, (cmp_rows, QLANES), 0) * CMP_STRIDE + (CMP_LEN - 1)
        s = jnp.where(cend <= pos, s, NEG_INF)
        m = jnp.maximum(jnp.max(s, axis=0, keepdims=True), MASKED_MAX_FLOOR)
        p = jnp.exp2(s - m)
        inv_l = 1.0 / jnp.maximum(jnp.sum(p, axis=0, keepdims=True), 1e-30)
        o_cmp = _dot(vcT_ref[0, 0, :, 0:cmp_rows], p.astype(BF16)) * inv_l

        pn = p * inv_l
        ph = pn[:, 0:Q_BLOCK]
        for r in range(1, NSA_REP):
            ph = ph + pn[:, r * Q_BLOCK:(r + 1) * Q_BLOCK]
        hi = ph.astype(BF16)
        lo = (ph - hi.astype(F32)).astype(BF16)
        ovT = ovT_ref[:, 0:cmp_rows]
        imp = _dot(ovT, hi) + _dot(ovT, lo)

        blk = lax.broadcasted_iota(jnp.int32, (nblk, Q_BLOCK), 0)
        cur = pos1 // SEL_BLOCK
        forced = (blk == 0) | (blk == cur) | (blk == cur - 1)
        v0 = jnp.where(forced, -1.0, jnp.where(blk * SEL_BLOCK <= pos1, imp, -1.0))
        n_piece = WIN_KEYS // Q_BLOCK
        kk = lax.broadcasted_iota(jnp.int32, (Q_BLOCK, QLANES), 0)
        qq = lax.broadcasted_iota(jnp.int32, (Q_BLOCK, QLANES), 1) % Q_BLOCK
        win = {"m": None, "acc": None}

        def piece_max(w):
            rows = slice(w * Q_BLOCK, (w + 1) * Q_BLOCK)
            sp = swin_ref[e, rows, :]
            if not steady:
                kpos = w0 + w * Q_BLOCK + kk
                sp = jnp.where(kpos <= pos, jnp.where(kpos > pos - WINDOW, sp, NEG_INF), NEG_INF)
            elif w == 0:
                sp = jnp.where(kk > qq, sp, NEG_INF)
            elif w == n_piece - 1:
                sp = jnp.where(kk <= qq, sp, NEG_INF)
            if not steady or w in (0, n_piece - 1):
                swin_ref[e, rows, :] = sp
            top = jnp.max(sp, axis=0, keepdims=True)
            win["m"] = top if win["m"] is None else jnp.maximum(win["m"], top)

        def piece_absorb(w):
            rows = slice(w * Q_BLOCK, (w + 1) * Q_BLOCK)
            p = jnp.exp2(swin_ref[e, rows, :] - win["m"]).astype(BF16)
            k0 = pl.multiple_of(w0 + w * Q_BLOCK, Q_BLOCK)
            pv = _dot(_with_ones(vwinT_ref[0, :, pl.ds(k0, Q_BLOCK)]), p)
            win["acc"] = pv if win["acc"] is None else win["acc"] + pv

        for w in range(n_piece):
            piece_max(w)
        left = v0
        pending = list(range(n_piece))
        for r in range(n_top):
            left = jnp.where(left == jnp.max(left, axis=0, keepdims=True), -2.0, left)
            if pending and r % 2 == 0:
                piece_absorb(pending.pop(0))
        for w in pending:
            piece_absorb(w)
        taken = left != v0
        n_taken = jnp.sum(jnp.where(taken, jnp.where(v0 >= 0.0, 1.0, 0.0), 0.0), axis=0, keepdims=True)

        def store_bias(left):
            bias = jnp.where(forced, 0.0, jnp.where(left != v0, 0.0, NEG_INF)).astype(BF16)
            bias = jnp.concatenate([bias] * NSA_REP, axis=1)
            if nblk < LANES:
                bias = jnp.concatenate([bias, jnp.zeros((LANES - nblk, QLANES), BF16)], axis=0)
            qaug_ref[LANES:2 * LANES, lanes] = bias

        store_bias(left)
        bias0 = jnp.where(forced[0:CHUNK_BLOCKS], 0.0,
                          jnp.where(left[0:CHUNK_BLOCKS] != v0[0:CHUNK_BLOCKS], 0.0, NEG_INF))
        bias0 = jnp.concatenate([bias0] * NSA_REP, axis=1)
        top = None
        for n in range(CHUNK_BLOCKS):
            rows = slice(n * SEL_BLOCK, (n + 1) * SEL_BLOCK)
            sb = s0_ref[rows, lanes] + bias0[n:n + 1, :]
            s0_ref[rows, lanes] = sb
            bt = jnp.max(sb, axis=0, keepdims=True)
            top = bt if top is None else jnp.maximum(top, bt)
        top0_ref[:, lanes] = top

        a_w = win["acc"]
        gate = gate_ref[0, 0, e]
        part_ref[:, lanes] = gate[0:1] * o_cmp + gate[2:3] * (a_w[0:d] / a_w[d:d + 1])

        def select_exactly():
            def drop_first_max(_, val):
                top = jnp.max(val, axis=0, keepdims=True)
                first = jnp.min(jnp.where(val == top, blk, nblk), axis=0, keepdims=True)
                return jnp.where(blk == first, -2.0, val)

            store_bias(lax.fori_loop(0, n_top, drop_first_max, v0))

        return jnp.max(n_taken) > n_top, select_exactly

    def select_and_window(cmp_rows, steady):
        redo = [block_select_and_window(e, cmp_rows, steady) for e in range(STEP_BLOCKS)]
        any_overflowed = False
        for overflowed, select_exactly in redo:
            pl.when(overflowed)(select_exactly)
            any_overflowed = jnp.logical_or(any_overflowed, overflowed)
        pl.when(any_overflowed)(functools.partial(scores_into, s0_ref, top0_ref, 0))

    step_tokens = STEP_BLOCKS * Q_BLOCK

    def visible_cmp(step):
        return (step * step_tokens + step_tokens - CMP_LEN) // CMP_STRIDE + 1

    n_steps = ksel_ref.shape[1] // step_tokens
    steady_from = -(-WINDOW // step_tokens)
    first_rows = min(CMP_ROWS_STEP, ncmp)
    assert visible_cmp(steady_from - 1) <= first_rows
    pl.when(i < steady_from)(functools.partial(select_and_window, first_rows, False))
    lo = steady_from
    for rows in range(first_rows, ncmp + 1, CMP_ROWS_STEP):
        hi = next((step for step in range(lo, n_steps) if visible_cmp(step) > rows), n_steps)
        if hi > lo:
            pl.when((i >= lo) & (i < hi))(functools.partial(select_and_window, rows, True))
            lo = hi
    assert lo == n_steps

    def absorb(s_ref, top_ref, j, diagonal):
        k0 = chunk_start(j)
        s = s_ref[...]
        if diagonal:
            kpos = k0 + lax.broadcasted_iota(jnp.int32, (SEL_CHUNK, STEP_LANES), 0)
            s = jnp.where(kpos <= pos_step, s, NEG_INF)
            top = jnp.max(s, axis=0, keepdims=True)
        else:
            top = top_ref[...]
        m = m_ref[...]
        m_new = jnp.maximum(m, top)
        p = jnp.exp2(s - m_new).astype(BF16)
        acc_ref[...] = jnp.exp2(m - m_new) * acc_ref[...] + _dot(
            _with_ones(vselT_ref[0, :, pl.ds(k0, SEL_CHUNK)]), p)
        m_ref[...] = m_new

    last = t_step // SEL_CHUNK
    m_ref[...] = jnp.full(m_ref.shape, NEG_INF, F32)
    acc_ref[...] = jnp.zeros(acc_ref.shape, F32)

    def pair(t):
        scores_into(s1_ref, top1_ref, 2 * t + 1)
        absorb(s0_ref, top0_ref, 2 * t, False)
        scores_into(s0_ref, top0_ref, 2 * t + 2)
        absorb(s1_ref, top1_ref, 2 * t + 1, False)

    def pair_step(t, carry):
        pair(t)
        return carry

    lax.fori_loop(0, last // 2, pair_step, 0)

    @pl.when(last % 2 == 1)
    def _():
        scores_into(s1_ref, top1_ref, last)
        absorb(s0_ref, top0_ref, last - 1, False)
        absorb(s1_ref, top1_ref, last, True)

    @pl.when(last % 2 == 0)
    def _():
        absorb(s0_ref, top0_ref, last, True)

    a_s = acc_ref[...]
    gate_sel = jnp.concatenate([gate_ref[0, 0, e][1:2] for e in range(STEP_BLOCKS)], axis=1)
    out = part_ref[...] + gate_sel * (a_s[0:d] / a_s[d:d + 1])
    for e in range(STEP_BLOCKS):
        for r in range(NSA_REP):
            lane0 = e * QLANES + r * Q_BLOCK
            o_ref[0, r * HEAD_DIM:(r + 1) * HEAD_DIM, e * Q_BLOCK:(e + 1) * Q_BLOCK] = (
                out[:, lane0:lane0 + Q_BLOCK].astype(BF16))


def _nsa_attention(qT, qrT, kcmp, vcmpT, ksel, vselT, kwin, vwinT, gates):
    b, _, s = qT.shape
    nq = s // Q_BLOCK
    nblk = s // SEL_BLOCK
    ncmp = kcmp.shape[2]
    rows = NSA_REP * HEAD_DIM
    assert nblk <= LANES and s % SEL_CHUNK == 0 and s >= WIN_KEYS

    gt = gates[:, :, :NSA_HEADS * N_GATES].reshape(b, nq, Q_BLOCK, NSA_KV_GROUPS, NSA_REP, N_GATES)
    gt = gt.transpose(0, 3, 1, 5, 4, 2).reshape(b, NSA_KV_GROUPS, nq, N_GATES, QLANES)

    cstart = np.arange(ncmp) * CMP_STRIDE
    sstart = np.arange(nblk) * SEL_BLOCK
    ovT = ((cstart[None, :] < sstart[:, None] + SEL_BLOCK) & (cstart[None, :] + CMP_LEN > sstart[:, None]))
    ovT = jnp.asarray(ovT, BF16)
    onehot = jnp.asarray(np.arange(s)[:, None] // SEL_BLOCK == np.arange(LANES)[None, :], BF16)

    assert nq % STEP_BLOCKS == 0
    q_spec = pl.BlockSpec((1, rows, STEP_BLOCKS * Q_BLOCK), lambda bi, g, i: (bi, g, i))
    k_spec = pl.BlockSpec((1, s, LANES), lambda bi, g, i: (bi, 0, g))
    vT_spec = pl.BlockSpec((1, HEAD_DIM, s), lambda bi, g, i: (bi, g, 0))
    return pl.pallas_call(
        _attn_body,
        grid=(b, NSA_KV_GROUPS, nq // STEP_BLOCKS),
        in_specs=[q_spec, q_spec,
                  pl.BlockSpec((1, 1, ncmp, LANES), lambda bi, g, i: (bi, g, 0, 0)),
                  pl.BlockSpec((1, 1, HEAD_DIM, ncmp), lambda bi, g, i: (bi, g, 0, 0)),
                  k_spec, vT_spec, k_spec, vT_spec,
                  pl.BlockSpec((1, 1, STEP_BLOCKS, N_GATES, QLANES), lambda bi, g, i: (bi, g, i, 0, 0)),
                  _const_spec((nblk, ncmp)), _const_spec((s, LANES))],
        out_specs=q_spec,
        out_shape=jax.ShapeDtypeStruct((b, NSA_HEADS * HEAD_DIM, s), BF16),
        scratch_shapes=[pltpu.VMEM((2 * LANES, STEP_LANES), BF16),
                        pltpu.VMEM((SEL_CHUNK, STEP_LANES), F32), pltpu.VMEM((SEL_CHUNK, STEP_LANES), F32),
                        pltpu.VMEM((1, STEP_LANES), F32), pltpu.VMEM((HEAD_DIM + ONES_ROWS, STEP_LANES), F32),
                        pltpu.VMEM((HEAD_DIM, STEP_LANES), F32), pltpu.VMEM((STEP_BLOCKS, WIN_KEYS, QLANES), F32),
                        pltpu.VMEM((1, STEP_LANES), F32), pltpu.VMEM((1, STEP_LANES), F32)],
        compiler_params=_params(("arbitrary", "arbitrary", "arbitrary")),
        name="nsa_attention",
    )(qT, qrT, kcmp, vcmpT, ksel, vselT, kwin, vwinT, gt, ovT, onehot)


def _nsa_mixer(x, g_pre, w_in, b_gate, cmp_pos, cmp_w1, cmp_w2):
    qT, qrT, ksel, kwin, vselT, vwinT, kc_raw, vc_raw, gates = _nsa_project(x, g_pre, w_in, b_gate)
    kcmp, vcmpT = _nsa_compress(kc_raw, vc_raw, cmp_pos, cmp_w1, cmp_w2)
    return _nsa_attention(qT, qrT, kcmp, vcmpT, ksel, vselT, kwin, vwinT, gates)


def kernel(x, mix_norm_pre, mix_norm_post, ffn_norm_pre, ffn_norm_post, ffn_w_gate, ffn_w_up, ffn_w_down,
           conv_w_pw1, conv_b_pw1, conv_w_dw, conv_b_dw, conv_ln_g, conv_ln_b, conv_w_pw2, conv_b_pw2,
           nsa_w_in, nsa_b_gate, nsa_cmp_pos, nsa_cmp_w1, nsa_cmp_w2, nsa_w_out):
    b, s, d = x.shape
    depth = mix_norm_pre.shape[0]
    n_mixers = 2

    def ffn(x, i, half, attn_out=None):
        y = _ffn_half(x.reshape(b * s, d), ffn_norm_pre[i, half], ffn_norm_post[i, half],
                      ffn_w_gate[i, half], ffn_w_up[i, half], ffn_w_down[i, half], attn_out)
        return y.reshape(b, s, d)

    for i in range(depth):
        x = ffn(x, i, 0)
        j = i // n_mixers
        if i % n_mixers == 0:
            x = _conv_mixer(x, mix_norm_pre[i], mix_norm_post[i], conv_w_pw1[j], conv_b_pw1[j], conv_w_dw[j],
                            conv_b_dw[j], conv_ln_g[j], conv_ln_b[j], conv_w_pw2[j], conv_b_pw2[j])
            x = ffn(x, i, 1)
        else:
            attnT = _nsa_mixer(x, mix_norm_pre[i], nsa_w_in[j], nsa_b_gate[j], nsa_cmp_pos[j],
                               nsa_cmp_w1[j], nsa_cmp_w2[j])
            x = ffn(x, i, 1, (attnT, nsa_w_out[j], mix_norm_post[i]))
    return x
```

```python
import functools
import math

import jax
import jax.numpy as jnp
import numpy as np
from jax import lax
from jax.experimental import pallas as pl
from jax.experimental.pallas import tpu as pltpu

RMS_EPS = 1e-6
LN_EPS = 1e-5
FFN_RESIDUAL_WEIGHT = 0.5
CONV_WIDTH = 31
NSA_HEADS = 16
NSA_KV_GROUPS = 4
NSA_REP = NSA_HEADS // NSA_KV_GROUPS
HEAD_DIM = 64
ROT_HALF = HEAD_DIM // 8
ROPE_THETA = 500000.0
CMP_LEN = 32
CMP_STRIDE = 16
SEL_BLOCK = 64
N_SEL = 16
WINDOW = 512
Q_BLOCK = 128
N_GATES = 3
NEG_INF = -1e30

LANES = 128
SUBLANES = 8
V7X_VMEM_LIMIT = 56 * 1024 * 1024

BF16 = jnp.bfloat16
F32 = jnp.float32


def _params(sem):
    return pltpu.CompilerParams(dimension_semantics=sem, vmem_limit_bytes=V7X_VMEM_LIMIT)


def _const_spec(shape):
    nd = len(shape)
    return pl.BlockSpec(shape, lambda *_: (0,) * nd, pipeline_mode=pl.Buffered(1))


def _rms(x, gain):
    return x * lax.rsqrt(jnp.mean(x * x, axis=-1, keepdims=True) + RMS_EPS) * gain


def _sigmoid(x):
    return 1.0 / (1.0 + jnp.exp(-x))


def _dot(a, b):
    return jnp.dot(a, b, preferred_element_type=F32)


def _dot_nt(a, b):
    return lax.dot_general(a, b, (((1,), (1,)), ((), ())), preferred_element_type=F32)


def _dot_tn(a, b):
    return lax.dot_general(a, b, (((0,), (0,)), ((), ())), preferred_element_type=F32)


FFN_ROWS = 512
FFN_CHUNK = 512


def _swiglu_half_step(x, gpre_ref, gpost_ref, wg_ref, wu_ref, wd_ref, o_ref):
    h = _rms(x, gpre_ref[...]).astype(BF16)
    d_ff = wd_ref.shape[0]
    acc = jnp.zeros(x.shape, F32)
    for c0 in range(0, d_ff, FFN_CHUNK):
        c1 = min(c0 + FFN_CHUNK, d_ff)
        g = _dot(h, wg_ref[:, c0:c1])
        u = _dot(h, wu_ref[:, c0:c1])
        a = (g * _sigmoid(g) * u).astype(BF16)
        acc = acc + _dot(a, wd_ref[c0:c1, :])
    o_ref[...] = x + FFN_RESIDUAL_WEIGHT * _rms(acc, gpost_ref[...])


def _ffn_body(x_ref, *refs):
    _swiglu_half_step(x_ref[...], *refs)


def _attn_out_ffn_body(x_ref, aT_ref, wo_ref, gmix_ref, *refs):
    x = x_ref[...] + _rms(_dot_tn(aT_ref[0], wo_ref[...]), gmix_ref[...])
    _swiglu_half_step(x, *refs)


def _ffn_half(x2, g_pre, g_post, w_gate, w_up, w_down, layer, attn_out=None):
    t, d = x2.shape
    f = w_gate.shape[-1]
    tm = min(FFN_ROWS, t)
    row = pl.BlockSpec((tm, d), lambda i: (i, 0))

    def stacked(rows, cols):
        return pl.BlockSpec((None, None, rows, cols), lambda *_: (*layer, 0, 0), pipeline_mode=pl.Buffered(1))

    ffn_specs = [_const_spec((1, d)), _const_spec((1, d)), stacked(d, f), stacked(d, f), stacked(f, d)]
    ffn_args = (g_pre.reshape(1, d), g_post.reshape(1, d), w_gate, w_up, w_down)
    if attn_out is None:
        body, specs, args = _ffn_body, [row], (x2,)
    else:
        attnT, w_out, g_mix = attn_out
        tiles = attnT.shape[2] // tm
        body = _attn_out_ffn_body
        specs = [row, pl.BlockSpec((1, attnT.shape[1], tm), lambda i: (i // tiles, 0, i % tiles)),
                 _const_spec(w_out.shape), _const_spec((1, d))]
        args = (x2, attnT, w_out.astype(BF16), g_mix.reshape(1, d))
    return pl.pallas_call(
        body,
        grid=(t // tm,),
        in_specs=specs + ffn_specs,
        out_specs=row,
        out_shape=jax.ShapeDtypeStruct((t, d), F32),
        compiler_params=_params(("arbitrary",)),
        name="ffn_half",
    )(*args, *ffn_args)


CONV_ROWS = 256
CONV_HALO = 32
CONV_STRIP = 32


def _conv_body(x_ref, gpre_ref, w1_ref, b1_ref, wdw_ref, bdw_ref, lng_ref, lnb_ref,
               w2_ref, b2_ref, gpost_ref, o_ref, buf_ref, dw_ref, shift_ref):
    ts, d = x_ref.shape[1], x_ref.shape[2]

    @pl.when(pl.program_id(1) == 0)
    def _():
        buf_ref[0:CONV_HALO, :] = jnp.zeros((CONV_HALO, d), F32)

    x = x_ref[0]
    h = _rms(x, gpre_ref[...]).astype(BF16)
    p = _dot(h, w1_ref[...]) + b1_ref[...]
    buf_ref[CONV_HALO:CONV_HALO + ts, :] = p[:, :d] * _sigmoid(p[:, d:])

    base = CONV_HALO - (CONV_WIDTH - 1)
    span = shift_ref.shape[1]
    for b in range(1, SUBLANES):
        shift_ref[b - 1] = buf_ref[b:b + span, :]

    for r0 in range(0, ts, CONV_STRIP):
        acc = jnp.zeros((CONV_STRIP, d), F32)
        for k in range(CONV_WIDTH):
            a, b = divmod(base + k, SUBLANES)
            lo = r0 + a * SUBLANES
            win = buf_ref[lo:lo + CONV_STRIP, :] if b == 0 else shift_ref[b - 1, lo:lo + CONV_STRIP, :]
            acc = acc + wdw_ref[k:k + 1, :] * win
        dw_ref[r0:r0 + CONV_STRIP, :] = acc
    buf_ref[0:CONV_HALO, :] = buf_ref[ts:ts + CONV_HALO, :]

    c = dw_ref[...] + bdw_ref[...]
    mu = jnp.mean(c, axis=-1, keepdims=True)
    cc = c - mu
    var = jnp.mean(cc * cc, axis=-1, keepdims=True)
    y = cc * lax.rsqrt(var + LN_EPS) * lng_ref[...] + lnb_ref[...]
    y = (y * _sigmoid(y)).astype(BF16)
    out = _dot(y, w2_ref[...]) + b2_ref[...]
    o_ref[0] = x + _rms(out, gpost_ref[...])


def _conv_mixer(x, g_pre, g_post, w_pw1, b_pw1, w_dw, b_dw, ln_g, ln_b, w_pw2, b_pw2):
    b, s, d = x.shape
    ts = min(CONV_ROWS, s)
    tile = pl.BlockSpec((1, ts, d), lambda bi, j: (bi, j, 0))
    vec = lambda n: _const_spec((1, n))
    return pl.pallas_call(
        _conv_body,
        grid=(b, s // ts),
        in_specs=[tile, vec(d), _const_spec((d, 2 * d)), vec(2 * d), _const_spec((CONV_WIDTH, d)),
                  vec(d), vec(d), vec(d), _const_spec((d, d)), vec(d), vec(d)],
        out_specs=tile,
        out_shape=jax.ShapeDtypeStruct((b, s, d), F32),
        scratch_shapes=[pltpu.VMEM((ts + CONV_HALO, d), F32), pltpu.VMEM((ts, d), F32),
                        pltpu.VMEM((SUBLANES - 1, ts + (CONV_HALO - 1) // SUBLANES * SUBLANES, d), F32)],
        compiler_params=_params(("arbitrary", "arbitrary")),
        name="conv_mixer",
    )(x, g_pre.reshape(1, d), w_pw1.astype(BF16), b_pw1.reshape(1, 2 * d), w_dw,
      b_dw.reshape(1, d), ln_g.reshape(1, d), ln_b.reshape(1, d), w_pw2.astype(BF16),
      b_pw2.reshape(1, d), g_post.reshape(1, d))


PROJ_ROWS = 512
Q_WIDTH = NSA_HEADS * HEAD_DIM
KV_WIDTH = NSA_KV_GROUPS * HEAD_DIM
KPAD_WIDTH = NSA_KV_GROUPS * LANES
GATE_PAD = LANES
Q_SCALE = HEAD_DIM ** -0.5 * math.log2(math.e)


def _rope_tables(s):
    pos = jnp.arange(s, dtype=F32)
    inv_freq = ROPE_THETA ** (-jnp.arange(0, 2 * ROT_HALF, 2, dtype=F32) / (2 * ROT_HALF))
    ang = pos[:, None] * inv_freq[None, :]
    return jnp.cos(ang), jnp.sin(ang)


def _proj_body(x_ref, gpre_ref, wtok_ref, wtr_ref, bg_ref, cosT_ref, sinT_ref, ck_ref, s1_ref, s2_ref,
               qT_ref, qrT_ref, ksel_ref, kwin_ref, vselT_ref, vwinT_ref, kc_ref, vc_ref, gate_ref, raw_ref):
    tm = x_ref.shape[1]
    h = _rms(x_ref[0], gpre_ref[...]).astype(BF16)
    tok = _dot(h, wtok_ref[...])
    tr = _dot_nt(wtr_ref[...], h)

    q = tr[0:Q_WIDTH] * Q_SCALE
    qT_ref[0] = q.astype(BF16)
    q3 = q.reshape(NSA_HEADS, HEAD_DIM, tm)
    cos, sin = cosT_ref[...], sinT_ref[...]
    x1, x2 = q3[:, 0:ROT_HALF], q3[:, ROT_HALF:2 * ROT_HALF]
    qr = jnp.concatenate([x1 * cos - x2 * sin, x2 * cos + x1 * sin, q3[:, 2 * ROT_HALF:]], axis=1)
    qrT_ref[0] = qr.reshape(Q_WIDTH, tm).astype(BF16)
    vselT_ref[0] = tr[Q_WIDTH:Q_WIDTH + KV_WIDTH].astype(BF16)
    vwinT_ref[0] = tr[Q_WIDTH + KV_WIDTH:Q_WIDTH + 2 * KV_WIDTH].astype(BF16)

    ck, s1, s2 = ck_ref[...], s1_ref[...], s2_ref[...]
    for out_ref, base in ((ksel_ref, 0), (kwin_ref, KPAD_WIDTH)):
        for g in range(NSA_KV_GROUPS):
            k = tok[:, base + g * LANES:base + (g + 1) * LANES]
            kr = k * ck + pltpu.roll(k, ROT_HALF, 1) * s1 + pltpu.roll(k, LANES - ROT_HALF, 1) * s2
            out_ref[0, :, g * LANES:(g + 1) * LANES] = kr.astype(BF16)
    c0 = 2 * KPAD_WIDTH
    gate_ref[0] = _sigmoid(tok[:, c0 + 2 * KV_WIDTH:] + bg_ref[...])

    tiles = KV_WIDTH // LANES
    for t in range(2 * tiles):
        raw_ref[t] = tok[:, c0 + t * LANES:c0 + (t + 1) * LANES]
    nrow = tm // CMP_STRIDE
    low_half = lax.broadcasted_iota(jnp.int32, (nrow, LANES), 1) < HEAD_DIM
    for out_ref, base in ((kc_ref, 0), (vc_ref, tiles)):
        for h in range(tiles):
            for i in range(CMP_STRIDE // 2):
                a0 = raw_ref[base + h, pl.ds(2 * i, nrow, stride=CMP_STRIDE), :]
                a1 = raw_ref[base + h, pl.ds(2 * i + 1, nrow, stride=CMP_STRIDE), :]
                cols = slice(i * LANES, (i + 1) * LANES)
                out_ref[0, 2 * h, :, cols] = jnp.where(low_half, a0, pltpu.roll(a1, HEAD_DIM, 1))
                out_ref[0, 2 * h + 1, :, cols] = jnp.where(low_half, pltpu.roll(a0, HEAD_DIM, 1), a1)


def _nsa_project(x, g_pre, w_in, b_gate):
    b, s, d = x.shape
    tm = min(PROJ_ROWS, s)
    n_gate = NSA_HEADS * N_GATES
    w_q = w_in[:, :Q_WIDTH]
    w_kv = w_in[:, Q_WIDTH:Q_WIDTH + 6 * KV_WIDTH].reshape(d, 6, NSA_KV_GROUPS, HEAD_DIM)
    w_gate = w_in[:, Q_WIDTH + 6 * KV_WIDTH:]

    def padded(w):
        return jnp.pad(w, ((0, 0), (0, 0), (0, LANES - HEAD_DIM))).reshape(d, KPAD_WIDTH)

    w_tok = jnp.concatenate(
        [padded(w_kv[:, 2]), padded(w_kv[:, 4]), w_kv[:, 0].reshape(d, KV_WIDTH),
         w_kv[:, 1].reshape(d, KV_WIDTH), jnp.pad(w_gate, ((0, 0), (0, GATE_PAD - n_gate)))],
        axis=1).astype(BF16)
    w_tr = jnp.concatenate(
        [w_q, w_kv[:, 3].reshape(d, KV_WIDTH), w_kv[:, 5].reshape(d, KV_WIDTH)], axis=1).T.astype(BF16)
    bg = jnp.pad(b_gate, (0, GATE_PAD - n_gate)).reshape(1, GATE_PAD)

    cos, sin = _rope_tables(s)
    zeros = jnp.zeros_like(sin)
    pad_to = lambda parts, fill: jnp.concatenate(
        parts + [jnp.full((s, LANES - 2 * ROT_HALF), fill, F32)], axis=1)
    ck = pad_to([cos, cos], 1.0)
    s1 = pad_to([zeros, sin], 0.0)
    s2 = pad_to([-sin, zeros], 0.0)

    ntok, ntr = w_tok.shape[1], w_tr.shape[0]
    tok_major = lambda w: pl.BlockSpec((1, tm, w), lambda bi, j: (bi, j, 0))
    tr_major = lambda r: pl.BlockSpec((1, r, tm), lambda bi, j: (bi, 0, j))
    cmp_shape = (b, NSA_KV_GROUPS, s // CMP_STRIDE, CMP_STRIDE * HEAD_DIM)
    cmp_rows = pl.BlockSpec((1, NSA_KV_GROUPS, tm // CMP_STRIDE, CMP_STRIDE * HEAD_DIM),
                            lambda bi, j: (bi, 0, j, 0))
    return pl.pallas_call(
        _proj_body,
        grid=(b, s // tm),
        in_specs=[tok_major(d), _const_spec((1, d)), _const_spec((d, ntok)), _const_spec((ntr, d)),
                  _const_spec((1, GATE_PAD)),
                  pl.BlockSpec((ROT_HALF, tm), lambda bi, j: (0, j)),
                  pl.BlockSpec((ROT_HALF, tm), lambda bi, j: (0, j)),
                  pl.BlockSpec((tm, LANES), lambda bi, j: (j, 0)),
                  pl.BlockSpec((tm, LANES), lambda bi, j: (j, 0)),
                  pl.BlockSpec((tm, LANES), lambda bi, j: (j, 0))],
        out_specs=[tr_major(Q_WIDTH), tr_major(Q_WIDTH), tok_major(KPAD_WIDTH), tok_major(KPAD_WIDTH),
                   tr_major(KV_WIDTH), tr_major(KV_WIDTH), cmp_rows, cmp_rows,
                   tok_major(GATE_PAD)],
        out_shape=[jax.ShapeDtypeStruct((b, Q_WIDTH, s), BF16), jax.ShapeDtypeStruct((b, Q_WIDTH, s), BF16),
                   jax.ShapeDtypeStruct((b, s, KPAD_WIDTH), BF16), jax.ShapeDtypeStruct((b, s, KPAD_WIDTH), BF16),
                   jax.ShapeDtypeStruct((b, KV_WIDTH, s), BF16), jax.ShapeDtypeStruct((b, KV_WIDTH, s), BF16),
                   jax.ShapeDtypeStruct(cmp_shape, F32), jax.ShapeDtypeStruct(cmp_shape, F32),
                   jax.ShapeDtypeStruct((b, s, GATE_PAD), F32)],
        scratch_shapes=[pltpu.VMEM((2 * KV_WIDTH // LANES, tm, LANES), F32)],
        compiler_params=_params(("arbitrary", "arbitrary")),
        name="nsa_project",
    )(x, g_pre.reshape(1, d), w_tok, w_tr, bg, cos.T, sin.T, ck, s1, s2)


def _compress_body(rk_ref, rv_ref, pos_ref, w1_ref, w2k_ref, w2vT_ref, kc_ref, vcT_ref):
    nrow = rk_ref.shape[2]
    half = rk_ref.shape[3]

    def hidden(r, t):
        top = _dot((r + pos_ref[t, 0:1, :]).astype(BF16), w1_ref[t, 0:half, :])
        bot = _dot((r + pos_ref[t, 1:2, :]).astype(BF16), w1_ref[t, half:2 * half, :])
        hid = top + pltpu.roll(bot, nrow - 1, 0)
        return (hid * _sigmoid(hid)).astype(BF16)

    kc_ref[0, 0] = _dot(hidden(rk_ref[0, 0], 0), w2k_ref[...]).astype(BF16)
    vcT_ref[0, 0] = _dot_nt(w2vT_ref[...], hidden(rv_ref[0, 0], 1)).astype(BF16)


def _nsa_compress(kc_raw, vc_raw, cmp_pos, cmp_w1, cmp_w2):
    b, _, nrow, half = kc_raw.shape
    hid = cmp_w1.shape[-1]
    pos = cmp_pos.reshape(2, 2, half)
    w2k = jnp.pad(cmp_w2[0], ((0, 0), (0, LANES - HEAD_DIM))).astype(BF16)
    w2vT = cmp_w2[1].T.astype(BF16)
    blk = pl.BlockSpec((1, 1, nrow, half), lambda bi, g: (bi, g, 0, 0))
    return pl.pallas_call(
        _compress_body,
        grid=(b, NSA_KV_GROUPS),
        in_specs=[blk, blk, _const_spec((2, 2, half)), _const_spec((2, 2 * half, hid)),
                  _const_spec((hid, LANES)), _const_spec((HEAD_DIM, hid))],
        out_specs=[pl.BlockSpec((1, 1, nrow, LANES), lambda bi, g: (bi, g, 0, 0)),
                   pl.BlockSpec((1, 1, HEAD_DIM, nrow), lambda bi, g: (bi, g, 0, 0))],
        out_shape=[jax.ShapeDtypeStruct((b, NSA_KV_GROUPS, nrow, LANES), BF16),
                   jax.ShapeDtypeStruct((b, NSA_KV_GROUPS, HEAD_DIM, nrow), BF16)],
        compiler_params=_params(("arbitrary", "arbitrary")),
        name="nsa_compress",
    )(kc_raw, vc_raw, pos, cmp_w1.astype(BF16), w2k, w2vT)


SEL_CHUNK = 512
CHUNK_BLOCKS = SEL_CHUNK // SEL_BLOCK
WIN_KEYS = WINDOW + Q_BLOCK
QLANES = NSA_REP * Q_BLOCK
STEP_BLOCKS = 4
STEP_LANES = STEP_BLOCKS * QLANES
N_FORCED = 3
ONES_ROWS = 16
CMP_ROWS_STEP = 256
MASKED_MAX_FLOOR = -1e29


def _with_ones(vT):
    return jnp.concatenate([vT, jnp.ones((ONES_ROWS, vT.shape[1]), BF16)], axis=0)


def _attn_body(qT_ref, qrT_ref, kc_ref, vcT_ref, ksel_ref, vselT_ref, kwin_ref, vwinT_ref, gate_ref,
               ovT_ref, oh_ref, o_ref, qaug_ref, s0_ref, s1_ref, m_ref, acc_ref, part_ref, swin_ref,
               top0_ref, top1_ref):
    i = pl.program_id(2)
    t_step = i * (STEP_BLOCKS * Q_BLOCK)
    nblk = ovT_ref.shape[0]
    ncmp = kc_ref.shape[2]
    n_top = min(N_SEL, nblk) - N_FORCED

    def lanes_of_heads(ref, e):
        cols = slice(e * Q_BLOCK, (e + 1) * Q_BLOCK)
        parts = [ref[0, r * HEAD_DIM:(r + 1) * HEAD_DIM, cols] for r in range(NSA_REP)]
        return jnp.concatenate(
            [jnp.concatenate(parts, axis=1), jnp.zeros((LANES - HEAD_DIM, QLANES), BF16)], axis=0)

    def positions(e):
        return t_step + e * Q_BLOCK + lax.broadcasted_iota(jnp.int32, (1, Q_BLOCK), 1)

    pos_step = jnp.concatenate([positions(e) for e in range(STEP_BLOCKS) for _ in range(NSA_REP)], axis=1)
    d = HEAD_DIM

    def chunk_start(j):
        return pl.multiple_of(j * SEL_CHUNK, SEL_CHUNK)

    def scores_into(s_ref, top_ref, j):
        k0 = chunk_start(j)
        ka = jnp.concatenate([ksel_ref[0, pl.ds(k0, SEL_CHUNK), :], oh_ref[pl.ds(k0, SEL_CHUNK), :]], axis=1)
        s = _dot(ka, qaug_ref[...])
        s_ref[...] = s
        top_ref[...] = jnp.max(s, axis=0, keepdims=True)

    def block_select_and_window(e, cmp_rows, steady):
        t0 = t_step + e * Q_BLOCK
        lanes = slice(e * QLANES, (e + 1) * QLANES)
        qT = lanes_of_heads(qT_ref, e)
        qrT = lanes_of_heads(qrT_ref, e)
        pos1 = positions(e)
        pos = jnp.concatenate([pos1] * NSA_REP, axis=1)
        qaug_ref[0:LANES, lanes] = qrT
        s = _dot(kc_ref[0, 0, 0:cmp_rows, :], qT)
        w0 = pl.multiple_of(t0 - WINDOW if steady else jnp.maximum(t0 - WINDOW, 0), Q_BLOCK)
        swin_ref[e] = _dot(kwin_ref[0, pl.ds(w0, WIN_KEYS), :], qrT)
        s0_ref[:, lanes] = _dot(ksel_ref[0, 0:SEL_CHUNK, :], qrT)
        cend = lax.broadcasted_iota(jnp.int32, (cmp_rows, QLANES), 0) * CMP_STRIDE + (CMP_LEN - 1)
        s = jnp.where(cend <= pos, s, NEG_INF)
        m = jnp.maximum(jnp.max(s, axis=0, keepdims=True), MASKED_MAX_FLOOR)
        p = jnp.exp2(s - m)
        inv_l = 1.0 / jnp.maximum(jnp.sum(p, axis=0, keepdims=True), 1e-30)
        o_cmp = _dot(vcT_ref[0, 0, :, 0:cmp_rows], p.astype(BF16)) * inv_l

        pn = p * inv_l
        ph = pn[:, 0:Q_BLOCK]
        for r in range(1, NSA_REP):
            ph = ph + pn[:, r * Q_BLOCK:(r + 1) * Q_BLOCK]
        hi = ph.astype(BF16)
        lo = (ph - hi.astype(F32)).astype(BF16)
        ovT = ovT_ref[:, 0:cmp_rows]
        imp = _dot(ovT, hi) + _dot(ovT, lo)

        blk = lax.broadcasted_iota(jnp.int32, (nblk, Q_BLOCK), 0)
        cur = pos1 // SEL_BLOCK
        forced = (blk == 0) | (blk == cur) | (blk == cur - 1)
        v0 = jnp.where(forced, -1.0, jnp.where(blk * SEL_BLOCK <= pos1, imp, -1.0))
        n_piece = WIN_KEYS // Q_BLOCK
        kk = lax.broadcasted_iota(jnp.int32, (Q_BLOCK, QLANES), 0)
        qq = lax.broadcasted_iota(jnp.int32, (Q_BLOCK, QLANES), 1) % Q_BLOCK
        win = {"m": None, "acc": None}

        def piece_max(w):
            rows = slice(w * Q_BLOCK, (w + 1) * Q_BLOCK)
            sp = swin_ref[e, rows, :]
            if not steady:
                kpos = w0 + w * Q_BLOCK + kk
                sp = jnp.where(kpos <= pos, jnp.where(kpos > pos - WINDOW, sp, NEG_INF), NEG_INF)
            elif w == 0:
                sp = jnp.where(kk > qq, sp, NEG_INF)
            elif w == n_piece - 1:
                sp = jnp.where(kk <= qq, sp, NEG_INF)
            if not steady or w in (0, n_piece - 1):
                swin_ref[e, rows, :] = sp
            top = jnp.max(sp, axis=0, keepdims=True)
            win["m"] = top if win["m"] is None else jnp.maximum(win["m"], top)

        def piece_absorb(w):
            rows = slice(w * Q_BLOCK, (w + 1) * Q_BLOCK)
            p = jnp.exp2(swin_ref[e, rows, :] - win["m"]).astype(BF16)
            k0 = pl.multiple_of(w0 + w * Q_BLOCK, Q_BLOCK)
            pv = _dot(_with_ones(vwinT_ref[0, :, pl.ds(k0, Q_BLOCK)]), p)
            win["acc"] = pv if win["acc"] is None else win["acc"] + pv

        for w in range(n_piece):
            piece_max(w)
        left = v0
        pending = list(range(n_piece))
        for r in range(n_top):
            left = jnp.where(left == jnp.max(left, axis=0, keepdims=True), -2.0, left)
            if pending and r % 2 == 0:
                piece_absorb(pending.pop(0))
        for w in pending:
            piece_absorb(w)
        taken = left != v0
        n_taken = jnp.sum(jnp.where(taken, jnp.where(v0 >= 0.0, 1.0, 0.0), 0.0), axis=0, keepdims=True)

        def store_bias(left):
            bias = jnp.where(forced, 0.0, jnp.where(left != v0, 0.0, NEG_INF)).astype(BF16)
            bias = jnp.concatenate([bias] * NSA_REP, axis=1)
            if nblk < LANES:
                bias = jnp.concatenate([bias, jnp.zeros((LANES - nblk, QLANES), BF16)], axis=0)
            qaug_ref[LANES:2 * LANES, lanes] = bias

        store_bias(left)
        bias0 = jnp.where(forced[0:CHUNK_BLOCKS], 0.0,
                          jnp.where(left[0:CHUNK_BLOCKS] != v0[0:CHUNK_BLOCKS], 0.0, NEG_INF))
        bias0 = jnp.concatenate([bias0] * NSA_REP, axis=1)
        top = None
        for n in range(CHUNK_BLOCKS):
            rows = slice(n * SEL_BLOCK, (n + 1) * SEL_BLOCK)
            sb = s0_ref[rows, lanes] + bias0[n:n + 1, :]
            s0_ref[rows, lanes] = sb
            bt = jnp.max(sb, axis=0, keepdims=True)
            top = bt if top is None else jnp.maximum(top, bt)
        top0_ref[:, lanes] = top

        a_w = win["acc"]
        gate = gate_ref[0, 0, e]
        part_ref[:, lanes] = gate[0:1] * o_cmp + gate[2:3] * (a_w[0:d] / a_w[d:d + 1])

        def select_exactly():
            def drop_first_max(_, val):
                top = jnp.max(val, axis=0, keepdims=True)
                first = jnp.min(jnp.where(val == top, blk, nblk), axis=0, keepdims=True)
                return jnp.where(blk == first, -2.0, val)

            store_bias(lax.fori_loop(0, n_top, drop_first_max, v0))

        return jnp.max(n_taken) > n_top, select_exactly

    def select_and_window(cmp_rows, steady):
        redo = [block_select_and_window(e, cmp_rows, steady) for e in range(STEP_BLOCKS)]
        any_overflowed = False
        for overflowed, select_exactly in redo:
            pl.when(overflowed)(select_exactly)
            any_overflowed = jnp.logical_or(any_overflowed, overflowed)
        pl.when(any_overflowed)(functools.partial(scores_into, s0_ref, top0_ref, 0))

    step_tokens = STEP_BLOCKS * Q_BLOCK

    def visible_cmp(step):
        return (step * step_tokens + step_tokens - CMP_LEN) // CMP_STRIDE + 1

    n_steps = ksel_ref.shape[1] // step_tokens
    steady_from = -(-WINDOW // step_tokens)
    first_rows = min(CMP_ROWS_STEP, ncmp)
    assert visible_cmp(steady_from - 1) <= first_rows
    pl.when(i < steady_from)(functools.partial(select_and_window, first_rows, False))
    lo = steady_from
    for rows in range(first_rows, ncmp + 1, CMP_ROWS_STEP):
        hi = next((step for step in range(lo, n_steps) if visible_cmp(step) > rows), n_steps)
        if hi > lo:
            pl.when((i >= lo) & (i < hi))(functools.partial(select_and_window, rows, True))
            lo = hi
    assert lo == n_steps

    def absorb(s_ref, top_ref, j, diagonal):
        k0 = chunk_start(j)
        s = s_ref[...]
        if diagonal:
            kpos = k0 + lax.broadcasted_iota(jnp.int32, (SEL_CHUNK, STEP_LANES), 0)
            s = jnp.where(kpos <= pos_step, s, NEG_INF)
            top = jnp.max(s, axis=0, keepdims=True)
        else:
            top = top_ref[...]
        m = m_ref[...]
        m_new = jnp.maximum(m, top)
        p = jnp.exp2(s - m_new).astype(BF16)
        acc_ref[...] = jnp.exp2(m - m_new) * acc_ref[...] + _dot(
            _with_ones(vselT_ref[0, :, pl.ds(k0, SEL_CHUNK)]), p)
        m_ref[...] = m_new

    last = t_step // SEL_CHUNK
    m_ref[...] = jnp.full(m_ref.shape, NEG_INF, F32)
    acc_ref[...] = jnp.zeros(acc_ref.shape, F32)

    def pair(t):
        scores_into(s1_ref, top1_ref, 2 * t + 1)
        absorb(s0_ref, top0_ref, 2 * t, False)
        scores_into(s0_ref, top0_ref, 2 * t + 2)
        absorb(s1_ref, top1_ref, 2 * t + 1, False)

    def pair_step(t, carry):
        pair(t)
        return carry

    lax.fori_loop(0, last // 2, pair_step, 0)

    @pl.when(last % 2 == 1)
    def _():
        scores_into(s1_ref, top1_ref, last)
        absorb(s0_ref, top0_ref, last - 1, False)
        absorb(s1_ref, top1_ref, last, True)

    @pl.when(last % 2 == 0)
    def _():
        absorb(s0_ref, top0_ref, last, True)

    a_s = acc_ref[...]
    gate_sel = jnp.concatenate([gate_ref[0, 0, e][1:2] for e in range(STEP_BLOCKS)], axis=1)
    out = part_ref[...] + gate_sel * (a_s[0:d] / a_s[d:d + 1])
    for e in range(STEP_BLOCKS):
        for r in range(NSA_REP):
            lane0 = e * QLANES + r * Q_BLOCK
            o_ref[0, r * HEAD_DIM:(r + 1) * HEAD_DIM, e * Q_BLOCK:(e + 1) * Q_BLOCK] = (
                out[:, lane0:lane0 + Q_BLOCK].astype(BF16))


def _nsa_attention(qT, qrT, kcmp, vcmpT, ksel, vselT, kwin, vwinT, gates):
    b, _, s = qT.shape
    nq = s // Q_BLOCK
    nblk = s // SEL_BLOCK
    ncmp = kcmp.shape[2]
    rows = NSA_REP * HEAD_DIM
    assert nblk <= LANES and s % SEL_CHUNK == 0 and s >= WIN_KEYS

    gt = gates[:, :, :NSA_HEADS * N_GATES].reshape(b, nq, Q_BLOCK, NSA_KV_GROUPS, NSA_REP, N_GATES)
    gt = gt.transpose(0, 3, 1, 5, 4, 2).reshape(b, NSA_KV_GROUPS, nq, N_GATES, QLANES)

    cstart = np.arange(ncmp) * CMP_STRIDE
    sstart = np.arange(nblk) * SEL_BLOCK
    ovT = ((cstart[None, :] < sstart[:, None] + SEL_BLOCK) & (cstart[None, :] + CMP_LEN > sstart[:, None]))
    ovT = jnp.asarray(ovT, BF16)
    onehot = jnp.asarray(np.arange(s)[:, None] // SEL_BLOCK == np.arange(LANES)[None, :], BF16)

    assert nq % STEP_BLOCKS == 0
    q_spec = pl.BlockSpec((1, rows, STEP_BLOCKS * Q_BLOCK), lambda bi, g, i: (bi, g, i))
    k_spec = pl.BlockSpec((1, s, LANES), lambda bi, g, i: (bi, 0, g))
    vT_spec = pl.BlockSpec((1, HEAD_DIM, s), lambda bi, g, i: (bi, g, 0))
    return pl.pallas_call(
        _attn_body,
        grid=(b, NSA_KV_GROUPS, nq // STEP_BLOCKS),
        in_specs=[q_spec, q_spec,
                  pl.BlockSpec((1, 1, ncmp, LANES), lambda bi, g, i: (bi, g, 0, 0)),
                  pl.BlockSpec((1, 1, HEAD_DIM, ncmp), lambda bi, g, i: (bi, g, 0, 0)),
                  k_spec, vT_spec, k_spec, vT_spec,
                  pl.BlockSpec((1, 1, STEP_BLOCKS, N_GATES, QLANES), lambda bi, g, i: (bi, g, i, 0, 0)),
                  _const_spec((nblk, ncmp)), _const_spec((s, LANES))],
        out_specs=q_spec,
        out_shape=jax.ShapeDtypeStruct((b, NSA_HEADS * HEAD_DIM, s), BF16),
        scratch_shapes=[pltpu.VMEM((2 * LANES, STEP_LANES), BF16),
                        pltpu.VMEM((SEL_CHUNK, STEP_LANES), F32), pltpu.VMEM((SEL_CHUNK, STEP_LANES), F32),
                        pltpu.VMEM((1, STEP_LANES), F32), pltpu.VMEM((HEAD_DIM + ONES_ROWS, STEP_LANES), F32),
                        pltpu.VMEM((HEAD_DIM, STEP_LANES), F32), pltpu.VMEM((STEP_BLOCKS, WIN_KEYS, QLANES), F32),
                        pltpu.VMEM((1, STEP_LANES), F32), pltpu.VMEM((1, STEP_LANES), F32)],
        compiler_params=_params(("arbitrary", "arbitrary", "arbitrary")),
        name="nsa_attention",
    )(qT, qrT, kcmp, vcmpT, ksel, vselT, kwin, vwinT, gt, ovT, onehot)


def _nsa_mixer(x, g_pre, w_in, b_gate, cmp_pos, cmp_w1, cmp_w2):
    qT, qrT, ksel, kwin, vselT, vwinT, kc_raw, vc_raw, gates = _nsa_project(x, g_pre, w_in, b_gate)
    kcmp, vcmpT = _nsa_compress(kc_raw, vc_raw, cmp_pos, cmp_w1, cmp_w2)
    return _nsa_attention(qT, qrT, kcmp, vcmpT, ksel, vselT, kwin, vwinT, gates)


def kernel(x, mix_norm_pre, mix_norm_post, ffn_norm_pre, ffn_norm_post, ffn_w_gate, ffn_w_up, ffn_w_down,
           conv_w_pw1, conv_b_pw1, conv_w_dw, conv_b_dw, conv_ln_g, conv_ln_b, conv_w_pw2, conv_b_pw2,
           nsa_w_in, nsa_b_gate, nsa_cmp_pos, nsa_cmp_w1, nsa_cmp_w2, nsa_w_out):
    b, s, d = x.shape
    depth = mix_norm_pre.shape[0]
    n_mixers = 2

    w_gate, w_up, w_down = ffn_w_gate.astype(BF16), ffn_w_up.astype(BF16), ffn_w_down.astype(BF16)

    def ffn(x, i, half, attn_out=None):
        y = _ffn_half(x.reshape(b * s, d), ffn_norm_pre[i, half], ffn_norm_post[i, half],
                      w_gate, w_up, w_down, (i, half), attn_out)
        return y.reshape(b, s, d)

    for i in range(depth):
        x = ffn(x, i, 0)
        j = i // n_mixers
        if i % n_mixers == 0:
            x = _conv_mixer(x, mix_norm_pre[i], mix_norm_post[i], conv_w_pw1[j], conv_b_pw1[j], conv_w_dw[j],
                            conv_b_dw[j], conv_ln_g[j], conv_ln_b[j], conv_w_pw2[j], conv_b_pw2[j])
            x = ffn(x, i, 1)
        else:
            attnT = _nsa_mixer(x, mix_norm_pre[i], nsa_w_in[j], nsa_b_gate[j], nsa_cmp_pos[j],
                               nsa_cmp_w1[j], nsa_cmp_w2[j])
            x = ffn(x, i, 1, (attnT, nsa_w_out[j], mix_norm_post[i]))
    return x
```

```python
import functools
import math

import jax
import jax.numpy as jnp
import numpy as np
from jax import lax
from jax.experimental import pallas as pl
from jax.experimental.pallas import tpu as pltpu

RMS_EPS = 1e-6
LN_EPS = 1e-5
FFN_RESIDUAL_WEIGHT = 0.5
CONV_WIDTH = 31
NSA_HEADS = 16
NSA_KV_GROUPS = 4
NSA_REP = NSA_HEADS // NSA_KV_GROUPS
HEAD_DIM = 64
ROT_HALF = HEAD_DIM // 8
ROPE_THETA = 500000.0
CMP_LEN = 32
CMP_STRIDE = 16
SEL_BLOCK = 64
N_SEL = 16
WINDOW = 512
Q_BLOCK = 128
N_GATES = 3
NEG_INF = -1e30

LANES = 128
SUBLANES = 8
V7X_VMEM_LIMIT = 56 * 1024 * 1024

BF16 = jnp.bfloat16
F32 = jnp.float32


def _params(sem):
    return pltpu.CompilerParams(dimension_semantics=sem, vmem_limit_bytes=V7X_VMEM_LIMIT)


def _const_spec(shape):
    nd = len(shape)
    return pl.BlockSpec(shape, lambda *_: (0,) * nd, pipeline_mode=pl.Buffered(1))


def _rms(x, gain):
    return x * lax.rsqrt(jnp.mean(x * x, axis=-1, keepdims=True) + RMS_EPS) * gain


def _sigmoid(x):
    return 1.0 / (1.0 + jnp.exp(-x))


def _dot(a, b):
    return jnp.dot(a, b, preferred_element_type=F32)


def _dot_nt(a, b):
    return lax.dot_general(a, b, (((1,), (1,)), ((), ())), preferred_element_type=F32)


def _dot_tn(a, b):
    return lax.dot_general(a, b, (((0,), (0,)), ((), ())), preferred_element_type=F32)


FFN_ROWS = 512
FFN_CHUNK = 512


def _swiglu_half_step(x, gpre_ref, gpost_ref, wg_ref, wu_ref, wd_ref, o_ref):
    h = _rms(x, gpre_ref[...]).astype(BF16)
    d_ff = wd_ref.shape[0]
    acc = jnp.zeros(x.shape, F32)
    for c0 in range(0, d_ff, FFN_CHUNK):
        c1 = min(c0 + FFN_CHUNK, d_ff)
        g = _dot(h, wg_ref[:, c0:c1])
        u = _dot(h, wu_ref[:, c0:c1])
        a = (g * _sigmoid(g) * u).astype(BF16)
        acc = acc + _dot(a, wd_ref[c0:c1, :])
    o_ref[...] = x + FFN_RESIDUAL_WEIGHT * _rms(acc, gpost_ref[...])


def _ffn_body(x_ref, *refs):
    _swiglu_half_step(x_ref[...], *refs)


def _attn_out_ffn_body(x_ref, aT_ref, wo_ref, gmix_ref, *refs):
    x = x_ref[...] + _rms(_dot_tn(aT_ref[0], wo_ref[...]), gmix_ref[...])
    _swiglu_half_step(x, *refs)


def _ffn_half(x2, g_pre, g_post, w_gate, w_up, w_down, layer, attn_out=None):
    t, d = x2.shape
    f = w_gate.shape[-1]
    tm = min(FFN_ROWS, t)
    row = pl.BlockSpec((tm, d), lambda i: (i, 0))

    def stacked(rows, cols):
        return pl.BlockSpec((None, None, rows, cols), lambda *_: (*layer, 0, 0), pipeline_mode=pl.Buffered(1))

    ffn_specs = [_const_spec((1, d)), _const_spec((1, d)), stacked(d, f), stacked(d, f), stacked(f, d)]
    ffn_args = (g_pre.reshape(1, d), g_post.reshape(1, d), w_gate, w_up, w_down)
    if attn_out is None:
        body, specs, args = _ffn_body, [row], (x2,)
    else:
        attnT, w_out, g_mix = attn_out
        tiles = attnT.shape[2] // tm
        body = _attn_out_ffn_body
        specs = [row, pl.BlockSpec((1, attnT.shape[1], tm), lambda i: (i // tiles, 0, i % tiles)),
                 _const_spec(w_out.shape), _const_spec((1, d))]
        args = (x2, attnT, w_out.astype(BF16), g_mix.reshape(1, d))
    return pl.pallas_call(
        body,
        grid=(t // tm,),
        in_specs=specs + ffn_specs,
        out_specs=row,
        out_shape=jax.ShapeDtypeStruct((t, d), F32),
        compiler_params=_params(("arbitrary",)),
        name="ffn_half",
    )(*args, *ffn_args)


CONV_ROWS = 256
CONV_HALO = 32
CONV_STRIP = 32


def _conv_body(x_ref, gpre_ref, w1_ref, b1_ref, wdw_ref, bdw_ref, lng_ref, lnb_ref,
               w2_ref, b2_ref, gpost_ref, o_ref, buf_ref, dw_ref, shift_ref):
    ts, d = x_ref.shape[1], x_ref.shape[2]

    @pl.when(pl.program_id(1) == 0)
    def _():
        buf_ref[0:CONV_HALO, :] = jnp.zeros((CONV_HALO, d), F32)

    x = x_ref[0]
    h = _rms(x, gpre_ref[...]).astype(BF16)
    p = _dot(h, w1_ref[...]) + b1_ref[...]
    buf_ref[CONV_HALO:CONV_HALO + ts, :] = p[:, :d] * _sigmoid(p[:, d:])

    base = CONV_HALO - (CONV_WIDTH - 1)
    span = shift_ref.shape[1]
    for b in range(1, SUBLANES):
        shift_ref[b - 1] = buf_ref[b:b + span, :]

    for r0 in range(0, ts, CONV_STRIP):
        acc = jnp.zeros((CONV_STRIP, d), F32)
        for k in range(CONV_WIDTH):
            a, b = divmod(base + k, SUBLANES)
            lo = r0 + a * SUBLANES
            win = buf_ref[lo:lo + CONV_STRIP, :] if b == 0 else shift_ref[b - 1, lo:lo + CONV_STRIP, :]
            acc = acc + wdw_ref[k:k + 1, :] * win
        dw_ref[r0:r0 + CONV_STRIP, :] = acc
    buf_ref[0:CONV_HALO, :] = buf_ref[ts:ts + CONV_HALO, :]

    c = dw_ref[...] + bdw_ref[...]
    mu = jnp.mean(c, axis=-1, keepdims=True)
    cc = c - mu
    var = jnp.mean(cc * cc, axis=-1, keepdims=True)
    y = cc * lax.rsqrt(var + LN_EPS) * lng_ref[...] + lnb_ref[...]
    y = (y * _sigmoid(y)).astype(BF16)
    out = _dot(y, w2_ref[...]) + b2_ref[...]
    o_ref[0] = x + _rms(out, gpost_ref[...])


def _conv_mixer(x, g_pre, g_post, w_pw1, b_pw1, w_dw, b_dw, ln_g, ln_b, w_pw2, b_pw2):
    b, s, d = x.shape
    ts = min(CONV_ROWS, s)
    tile = pl.BlockSpec((1, ts, d), lambda bi, j: (bi, j, 0))
    vec = lambda n: _const_spec((1, n))
    return pl.pallas_call(
        _conv_body,
        grid=(b, s // ts),
        in_specs=[tile, vec(d), _const_spec((d, 2 * d)), vec(2 * d), _const_spec((CONV_WIDTH, d)),
                  vec(d), vec(d), vec(d), _const_spec((d, d)), vec(d), vec(d)],
        out_specs=tile,
        out_shape=jax.ShapeDtypeStruct((b, s, d), F32),
        scratch_shapes=[pltpu.VMEM((ts + CONV_HALO, d), F32), pltpu.VMEM((ts, d), F32),
                        pltpu.VMEM((SUBLANES - 1, ts + (CONV_HALO - 1) // SUBLANES * SUBLANES, d), F32)],
        compiler_params=_params(("arbitrary", "arbitrary")),
        name="conv_mixer",
    )(x, g_pre.reshape(1, d), w_pw1.astype(BF16), b_pw1.reshape(1, 2 * d), w_dw,
      b_dw.reshape(1, d), ln_g.reshape(1, d), ln_b.reshape(1, d), w_pw2.astype(BF16),
      b_pw2.reshape(1, d), g_post.reshape(1, d))


PROJ_ROWS = 512
Q_WIDTH = NSA_HEADS * HEAD_DIM
KV_WIDTH = NSA_KV_GROUPS * HEAD_DIM
KPAD_WIDTH = NSA_KV_GROUPS * LANES
GATE_PAD = LANES
Q_SCALE = HEAD_DIM ** -0.5 * math.log2(math.e)


def _rope_tables(s):
    pos = jnp.arange(s, dtype=F32)
    inv_freq = ROPE_THETA ** (-jnp.arange(0, 2 * ROT_HALF, 2, dtype=F32) / (2 * ROT_HALF))
    ang = pos[:, None] * inv_freq[None, :]
    return jnp.cos(ang), jnp.sin(ang)


def _proj_body(x_ref, gpre_ref, wtok_ref, wtr_ref, bg_ref, cosT_ref, sinT_ref, ck_ref, s1_ref, s2_ref,
               qT_ref, qrT_ref, ksel_ref, kwin_ref, vselT_ref, vwinT_ref, kc_ref, vc_ref, gate_ref, raw_ref):
    tm = x_ref.shape[1]
    h = _rms(x_ref[0], gpre_ref[...]).astype(BF16)
    tok = _dot(h, wtok_ref[...])
    tr = _dot_nt(wtr_ref[...], h)

    q = tr[0:Q_WIDTH] * Q_SCALE
    qT_ref[0] = q.astype(BF16)
    q3 = q.reshape(NSA_HEADS, HEAD_DIM, tm)
    cos, sin = cosT_ref[...], sinT_ref[...]
    x1, x2 = q3[:, 0:ROT_HALF], q3[:, ROT_HALF:2 * ROT_HALF]
    qr = jnp.concatenate([x1 * cos - x2 * sin, x2 * cos + x1 * sin, q3[:, 2 * ROT_HALF:]], axis=1)
    qrT_ref[0] = qr.reshape(Q_WIDTH, tm).astype(BF16)
    vselT_ref[0] = tr[Q_WIDTH:Q_WIDTH + KV_WIDTH].astype(BF16)
    vwinT_ref[0] = tr[Q_WIDTH + KV_WIDTH:Q_WIDTH + 2 * KV_WIDTH].astype(BF16)

    ck, s1, s2 = ck_ref[...], s1_ref[...], s2_ref[...]
    for out_ref, base in ((ksel_ref, 0), (kwin_ref, KPAD_WIDTH)):
        for g in range(NSA_KV_GROUPS):
            k = tok[:, base + g * LANES:base + (g + 1) * LANES]
            kr = k * ck + pltpu.roll(k, ROT_HALF, 1) * s1 + pltpu.roll(k, LANES - ROT_HALF, 1) * s2
            out_ref[0, :, g * LANES:(g + 1) * LANES] = kr.astype(BF16)
    c0 = 2 * KPAD_WIDTH
    gate_ref[0] = _sigmoid(tok[:, c0 + 2 * KV_WIDTH:] + bg_ref[...])

    tiles = KV_WIDTH // LANES
    for t in range(2 * tiles):
        raw_ref[t] = tok[:, c0 + t * LANES:c0 + (t + 1) * LANES]
    nrow = tm // CMP_STRIDE
    low_half = lax.broadcasted_iota(jnp.int32, (nrow, LANES), 1) < HEAD_DIM
    for out_ref, base in ((kc_ref, 0), (vc_ref, tiles)):
        for h in range(tiles):
            for i in range(CMP_STRIDE // 2):
                a0 = raw_ref[base + h, pl.ds(2 * i, nrow, stride=CMP_STRIDE), :]
                a1 = raw_ref[base + h, pl.ds(2 * i + 1, nrow, stride=CMP_STRIDE), :]
                cols = slice(i * LANES, (i + 1) * LANES)
                out_ref[0, 2 * h, :, cols] = jnp.where(low_half, a0, pltpu.roll(a1, HEAD_DIM, 1))
                out_ref[0, 2 * h + 1, :, cols] = jnp.where(low_half, pltpu.roll(a0, HEAD_DIM, 1), a1)


def _nsa_project(x, g_pre, w_in, b_gate):
    b, s, d = x.shape
    tm = min(PROJ_ROWS, s)
    n_gate = NSA_HEADS * N_GATES
    w_q = w_in[:, :Q_WIDTH]
    w_kv = w_in[:, Q_WIDTH:Q_WIDTH + 6 * KV_WIDTH].reshape(d, 6, NSA_KV_GROUPS, HEAD_DIM)
    w_gate = w_in[:, Q_WIDTH + 6 * KV_WIDTH:]

    def padded(w):
        return jnp.pad(w, ((0, 0), (0, 0), (0, LANES - HEAD_DIM))).reshape(d, KPAD_WIDTH)

    w_tok = jnp.concatenate(
        [padded(w_kv[:, 2]), padded(w_kv[:, 4]), w_kv[:, 0].reshape(d, KV_WIDTH),
         w_kv[:, 1].reshape(d, KV_WIDTH), jnp.pad(w_gate, ((0, 0), (0, GATE_PAD - n_gate)))],
        axis=1).astype(BF16)
    w_tr = jnp.concatenate(
        [w_q, w_kv[:, 3].reshape(d, KV_WIDTH), w_kv[:, 5].reshape(d, KV_WIDTH)], axis=1).T.astype(BF16)
    bg = jnp.pad(b_gate, (0, GATE_PAD - n_gate)).reshape(1, GATE_PAD)

    cos, sin = _rope_tables(s)
    zeros = jnp.zeros_like(sin)
    pad_to = lambda parts, fill: jnp.concatenate(
        parts + [jnp.full((s, LANES - 2 * ROT_HALF), fill, F32)], axis=1)
    ck = pad_to([cos, cos], 1.0)
    s1 = pad_to([zeros, sin], 0.0)
    s2 = pad_to([-sin, zeros], 0.0)

    ntok, ntr = w_tok.shape[1], w_tr.shape[0]
    tok_major = lambda w: pl.BlockSpec((1, tm, w), lambda bi, j: (bi, j, 0))
    tr_major = lambda r: pl.BlockSpec((1, r, tm), lambda bi, j: (bi, 0, j))
    cmp_shape = (b, NSA_KV_GROUPS, s // CMP_STRIDE, CMP_STRIDE * HEAD_DIM)
    cmp_rows = pl.BlockSpec((1, NSA_KV_GROUPS, tm // CMP_STRIDE, CMP_STRIDE * HEAD_DIM),
                            lambda bi, j: (bi, 0, j, 0))
    return pl.pallas_call(
        _proj_body,
        grid=(b, s // tm),
        in_specs=[tok_major(d), _const_spec((1, d)), _const_spec((d, ntok)), _const_spec((ntr, d)),
                  _const_spec((1, GATE_PAD)),
                  pl.BlockSpec((ROT_HALF, tm), lambda bi, j: (0, j)),
                  pl.BlockSpec((ROT_HALF, tm), lambda bi, j: (0, j)),
                  pl.BlockSpec((tm, LANES), lambda bi, j: (j, 0)),
                  pl.BlockSpec((tm, LANES), lambda bi, j: (j, 0)),
                  pl.BlockSpec((tm, LANES), lambda bi, j: (j, 0))],
        out_specs=[tr_major(Q_WIDTH), tr_major(Q_WIDTH), tok_major(KPAD_WIDTH), tok_major(KPAD_WIDTH),
                   tr_major(KV_WIDTH), tr_major(KV_WIDTH), cmp_rows, cmp_rows,
                   tok_major(GATE_PAD)],
        out_shape=[jax.ShapeDtypeStruct((b, Q_WIDTH, s), BF16), jax.ShapeDtypeStruct((b, Q_WIDTH, s), BF16),
                   jax.ShapeDtypeStruct((b, s, KPAD_WIDTH), BF16), jax.ShapeDtypeStruct((b, s, KPAD_WIDTH), BF16),
                   jax.ShapeDtypeStruct((b, KV_WIDTH, s), BF16), jax.ShapeDtypeStruct((b, KV_WIDTH, s), BF16),
                   jax.ShapeDtypeStruct(cmp_shape, F32), jax.ShapeDtypeStruct(cmp_shape, F32),
                   jax.ShapeDtypeStruct((b, s, GATE_PAD), F32)],
        scratch_shapes=[pltpu.VMEM((2 * KV_WIDTH // LANES, tm, LANES), F32)],
        compiler_params=_params(("arbitrary", "arbitrary")),
        name="nsa_project",
    )(x, g_pre.reshape(1, d), w_tok, w_tr, bg, cos.T, sin.T, ck, s1, s2)


def _compress_body(rk_ref, rv_ref, pos_ref, w1_ref, w2k_ref, w2vT_ref, kc_ref, vcT_ref):
    nrow = rk_ref.shape[2]
    half = rk_ref.shape[3]

    def hidden(r, t):
        top = _dot((r + pos_ref[t, 0:1, :]).astype(BF16), w1_ref[t, 0:half, :])
        bot = _dot((r + pos_ref[t, 1:2, :]).astype(BF16), w1_ref[t, half:2 * half, :])
        hid = top + pltpu.roll(bot, nrow - 1, 0)
        return (hid * _sigmoid(hid)).astype(BF16)

    kc_ref[0, 0] = _dot(hidden(rk_ref[0, 0], 0), w2k_ref[...]).astype(BF16)
    vcT_ref[0, 0] = _dot_nt(w2vT_ref[...], hidden(rv_ref[0, 0], 1)).astype(BF16)


def _nsa_compress(kc_raw, vc_raw, cmp_pos, cmp_w1, cmp_w2):
    b, _, nrow, half = kc_raw.shape
    hid = cmp_w1.shape[-1]
    pos = cmp_pos.reshape(2, 2, half)
    w2k = jnp.pad(cmp_w2[0], ((0, 0), (0, LANES - HEAD_DIM))).astype(BF16)
    w2vT = cmp_w2[1].T.astype(BF16)
    blk = pl.BlockSpec((1, 1, nrow, half), lambda bi, g: (bi, g, 0, 0))
    return pl.pallas_call(
        _compress_body,
        grid=(b, NSA_KV_GROUPS),
        in_specs=[blk, blk, _const_spec((2, 2, half)), _const_spec((2, 2 * half, hid)),
                  _const_spec((hid, LANES)), _const_spec((HEAD_DIM, hid))],
        out_specs=[pl.BlockSpec((1, 1, nrow, LANES), lambda bi, g: (bi, g, 0, 0)),
                   pl.BlockSpec((1, 1, HEAD_DIM, nrow), lambda bi, g: (bi, g, 0, 0))],
        out_shape=[jax.ShapeDtypeStruct((b, NSA_KV_GROUPS, nrow, LANES), BF16),
                   jax.ShapeDtypeStruct((b, NSA_KV_GROUPS, HEAD_DIM, nrow), BF16)],
        compiler_params=_params(("arbitrary", "arbitrary")),
        name="nsa_compress",
    )(kc_raw, vc_raw, pos, cmp_w1.astype(BF16), w2k, w2vT)


SEL_CHUNK = 512
CHUNK_BLOCKS = SEL_CHUNK // SEL_BLOCK
WIN_KEYS = WINDOW + Q_BLOCK
QLANES = NSA_REP * Q_BLOCK
STEP_BLOCKS = 4
STEP_LANES = STEP_BLOCKS * QLANES
N_FORCED = 3
ONES_ROWS = 16
CMP_ROWS_STEP = 256
MASKED_MAX_FLOOR = -1e29


def _with_ones(vT):
    return jnp.concatenate([vT, jnp.ones((ONES_ROWS, vT.shape[1]), BF16)], axis=0)


def _attn_body(qT_ref, qrT_ref, kc_ref, vcT_ref, ksel_ref, vselT_ref, kwin_ref, vwinT_ref, gate_ref,
               ovT_ref, oh_ref, o_ref, qaug_ref, s0_ref, s1_ref, m_ref, acc_ref, part_ref, swin_ref,
               top0_ref, top1_ref):
    i = pl.program_id(2)
    t_step = i * (STEP_BLOCKS * Q_BLOCK)
    nblk = ovT_ref.shape[0]
    ncmp = kc_ref.shape[2]
    n_top = min(N_SEL, nblk) - N_FORCED

    def lanes_of_heads(ref, e):
        cols = slice(e * Q_BLOCK, (e + 1) * Q_BLOCK)
        parts = [ref[0, r * HEAD_DIM:(r + 1) * HEAD_DIM, cols] for r in range(NSA_REP)]
        return jnp.concatenate(
            [jnp.concatenate(parts, axis=1), jnp.zeros((LANES - HEAD_DIM, QLANES), BF16)], axis=0)

    def positions(e):
        return t_step + e * Q_BLOCK + lax.broadcasted_iota(jnp.int32, (1, Q_BLOCK), 1)

    pos_step = jnp.concatenate([positions(e) for e in range(STEP_BLOCKS) for _ in range(NSA_REP)], axis=1)
    d = HEAD_DIM

    def chunk_start(j):
        return pl.multiple_of(j * SEL_CHUNK, SEL_CHUNK)

    def scores_into(s_ref, top_ref, j):
        k0 = chunk_start(j)
        ka = jnp.concatenate([ksel_ref[0, pl.ds(k0, SEL_CHUNK), :], oh_ref[pl.ds(k0, SEL_CHUNK), :]], axis=1)
        s = _dot(ka, qaug_ref[...])
        s_ref[...] = s
        top_ref[...] = jnp.max(s, axis=0, keepdims=True)

    def block_select_and_window(e, cmp_rows, steady):
        t0 = t_step + e * Q_BLOCK
        lanes = slice(e * QLANES, (e + 1) * QLANES)
        qT = lanes_of_heads(qT_ref, e)
        qrT = lanes_of_heads(qrT_ref, e)
        pos1 = positions(e)
        pos = jnp.concatenate([pos1] * NSA_REP, axis=1)
        qaug_ref[0:LANES, lanes] = qrT
        s = _dot(kc_ref[0, 0, 0:cmp_rows, :], qT)
        w0 = pl.multiple_of(t0 - WINDOW if steady else jnp.maximum(t0 - WINDOW, 0), Q_BLOCK)
        swin_ref[e] = _dot(kwin_ref[0, pl.ds(w0, WIN_KEYS), :], qrT)
        s0_ref[:, lanes] = _dot(ksel_ref[0, 0:SEL_CHUNK, :], qrT)
        cend = lax.broadcasted_iota(jnp.int32, (cmp_rows, QLANES), 0) * CMP_STRIDE + (CMP_LEN - 1)
        s = jnp.where(cend <= pos, s, NEG_INF)
        m = jnp.maximum(jnp.max(s, axis=0, keepdims=True), MASKED_MAX_FLOOR)
        p = jnp.exp2(s - m)
        inv_l = 1.0 / jnp.maximum(jnp.sum(p, axis=0, keepdims=True), 1e-30)
        o_cmp = _dot(vcT_ref[0, 0, :, 0:cmp_rows], p.astype(BF16)) * inv_l

        pn = p * inv_l
        ph = pn[:, 0:Q_BLOCK]
        for r in range(1, NSA_REP):
            ph = ph + pn[:, r * Q_BLOCK:(r + 1) * Q_BLOCK]
        hi = ph.astype(BF16)
        lo = (ph - hi.astype(F32)).astype(BF16)
        ovT = ovT_ref[:, 0:cmp_rows]
        imp = _dot(ovT, hi) + _dot(ovT, lo)

        blk = lax.broadcasted_iota(jnp.int32, (nblk, Q_BLOCK), 0)
        cur = pos1 // SEL_BLOCK
        forced = (blk == 0) | (blk == cur) | (blk == cur - 1)
        v0 = jnp.where(forced, -1.0, jnp.where(blk * SEL_BLOCK <= pos1, imp, -1.0))
        n_piece = WIN_KEYS // Q_BLOCK
        kk = lax.broadcasted_iota(jnp.int32, (Q_BLOCK, QLANES), 0)
        qq = lax.broadcasted_iota(jnp.int32, (Q_BLOCK, QLANES), 1) % Q_BLOCK
        win = {"m": None, "acc": None}

        def piece_max(w):
            rows = slice(w * Q_BLOCK, (w + 1) * Q_BLOCK)
            sp = swin_ref[e, rows, :]
            if not steady:
                kpos = w0 + w * Q_BLOCK + kk
                sp = jnp.where(kpos <= pos, jnp.where(kpos > pos - WINDOW, sp, NEG_INF), NEG_INF)
            elif w == 0:
                sp = jnp.where(kk > qq, sp, NEG_INF)
            elif w == n_piece - 1:
                sp = jnp.where(kk <= qq, sp, NEG_INF)
            if not steady or w in (0, n_piece - 1):
                swin_ref[e, rows, :] = sp
            top = jnp.max(sp, axis=0, keepdims=True)
            win["m"] = top if win["m"] is None else jnp.maximum(win["m"], top)

        def piece_absorb(w):
            rows = slice(w * Q_BLOCK, (w + 1) * Q_BLOCK)
            p = jnp.exp2(swin_ref[e, rows, :] - win["m"]).astype(BF16)
            k0 = pl.multiple_of(w0 + w * Q_BLOCK, Q_BLOCK)
            pv = _dot(_with_ones(vwinT_ref[0, :, pl.ds(k0, Q_BLOCK)]), p)
            win["acc"] = pv if win["acc"] is None else win["acc"] + pv

        for w in range(n_piece):
            piece_max(w)
        left = v0
        pending = list(range(n_piece))
        for r in range(n_top):
            left = jnp.where(left == jnp.max(left, axis=0, keepdims=True), -2.0, left)
            if pending and r % 2 == 0:
                piece_absorb(pending.pop(0))
        for w in pending:
            piece_absorb(w)
        taken = left != v0
        n_taken = jnp.sum(jnp.where(taken, jnp.where(v0 >= 0.0, 1.0, 0.0), 0.0), axis=0, keepdims=True)

        def store_bias(left):
            bias = jnp.where(forced, 0.0, jnp.where(left != v0, 0.0, NEG_INF)).astype(BF16)
            bias = jnp.concatenate([bias] * NSA_REP, axis=1)
            if nblk < LANES:
                bias = jnp.concatenate([bias, jnp.zeros((LANES - nblk, QLANES), BF16)], axis=0)
            qaug_ref[LANES:2 * LANES, lanes] = bias

        store_bias(left)
        bias0 = jnp.where(forced[0:CHUNK_BLOCKS], 0.0,
                          jnp.where(left[0:CHUNK_BLOCKS] != v0[0:CHUNK_BLOCKS], 0.0, NEG_INF))
        bias0 = jnp.concatenate([bias0] * NSA_REP, axis=1)
        top = None
        for n in range(CHUNK_BLOCKS):
            rows = slice(n * SEL_BLOCK, (n + 1) * SEL_BLOCK)
            sb = s0_ref[rows, lanes] + bias0[n:n + 1, :]
            s0_ref[rows, lanes] = sb
            bt = jnp.max(sb, axis=0, keepdims=True)
            top = bt if top is None else jnp.maximum(top, bt)
        top0_ref[:, lanes] = top

        a_w = win["acc"]
        gate = gate_ref[0, 0, e]
        part_ref[:, lanes] = gate[0:1] * o_cmp + gate[2:3] * (a_w[0:d] / a_w[d:d + 1])

        def select_exactly():
            def drop_first_max(_, val):
                top = jnp.max(val, axis=0, keepdims=True)
                first = jnp.min(jnp.where(val == top, blk, nblk), axis=0, keepdims=True)
                return jnp.where(blk == first, -2.0, val)

            store_bias(lax.fori_loop(0, n_top, drop_first_max, v0))

        return jnp.max(n_taken) > n_top, select_exactly

    def select_and_window(cmp_rows, steady):
        redo = [block_select_and_window(e, cmp_rows, steady) for e in range(STEP_BLOCKS)]
        any_overflowed = False
        for overflowed, select_exactly in redo:
            pl.when(overflowed)(select_exactly)
            any_overflowed = jnp.logical_or(any_overflowed, overflowed)
        pl.when(any_overflowed)(functools.partial(scores_into, s0_ref, top0_ref, 0))

    step_tokens = STEP_BLOCKS * Q_BLOCK

    def visible_cmp(step):
        return (step * step_tokens + step_tokens - CMP_LEN) // CMP_STRIDE + 1

    n_steps = ksel_ref.shape[1] // step_tokens
    steady_from = -(-WINDOW // step_tokens)
    first_rows = min(CMP_ROWS_STEP, ncmp)
    assert visible_cmp(steady_from - 1) <= first_rows
    pl.when(i < steady_from)(functools.partial(select_and_window, first_rows, False))
    lo = steady_from
    for rows in range(first_rows, ncmp + 1, CMP_ROWS_STEP):
        hi = next((step for step in range(lo, n_steps) if visible_cmp(step) > rows), n_steps)
        if hi > lo:
            pl.when((i >= lo) & (i < hi))(functools.partial(select_and_window, rows, True))
            lo = hi
    assert lo == n_steps

    def absorb(s_ref, top_ref, j, diagonal):
        k0 = chunk_start(j)
        s = s_ref[...]
        if diagonal:
            kpos = k0 + lax.broadcasted_iota(jnp.int32, (SEL_CHUNK, STEP_LANES), 0)
            s = jnp.where(kpos <= pos_step, s, NEG_INF)
            top = jnp.max(s, axis=0, keepdims=True)
        else:
            top = top_ref[...]
        m = m_ref[...]
        m_new = jnp.maximum(m, top)
        p = jnp.exp2(s - m_new).astype(BF16)
        acc_ref[...] = jnp.exp2(m - m_new) * acc_ref[...] + _dot(
            _with_ones(vselT_ref[0, :, pl.ds(k0, SEL_CHUNK)]), p)
        m_ref[...] = m_new

    def absorb_own_positions(s_ref, j):
        k0 = chunk_start(j)
        kk = lax.broadcasted_iota(jnp.int32, (Q_BLOCK, QLANES), 0)
        qq = lax.broadcasted_iota(jnp.int32, (Q_BLOCK, QLANES), 1) % Q_BLOCK
        for e in range(STEP_BLOCKS):
            lanes = slice(e * QLANES, (e + 1) * QLANES)
            rows = (e + 1) * Q_BLOCK
            s = jnp.where(kk <= qq, s_ref[e * Q_BLOCK:rows, lanes], NEG_INF)
            if e > 0:
                s = jnp.concatenate([s_ref[0:e * Q_BLOCK, lanes], s], axis=0)
            m = m_ref[:, lanes]
            m_new = jnp.maximum(m, jnp.max(s, axis=0, keepdims=True))
            p = jnp.exp2(s - m_new).astype(BF16)
            acc_ref[:, lanes] = jnp.exp2(m - m_new) * acc_ref[:, lanes] + _dot(
                _with_ones(vselT_ref[0, :, pl.ds(k0, rows)]), p)
            m_ref[:, lanes] = m_new

    def absorb_last(s_ref, top_ref, j):
        if STEP_BLOCKS * Q_BLOCK == SEL_CHUNK:
            absorb_own_positions(s_ref, j)
        else:
            absorb(s_ref, top_ref, j, True)

    last = t_step // SEL_CHUNK
    m_ref[...] = jnp.full(m_ref.shape, NEG_INF, F32)
    acc_ref[...] = jnp.zeros(acc_ref.shape, F32)

    def pair(t):
        scores_into(s1_ref, top1_ref, 2 * t + 1)
        absorb(s0_ref, top0_ref, 2 * t, False)
        scores_into(s0_ref, top0_ref, 2 * t + 2)
        absorb(s1_ref, top1_ref, 2 * t + 1, False)

    def pair_step(t, carry):
        pair(t)
        return carry

    lax.fori_loop(0, last // 2, pair_step, 0)

    @pl.when(last % 2 == 1)
    def _():
        scores_into(s1_ref, top1_ref, last)
        absorb(s0_ref, top0_ref, last - 1, False)
        absorb_last(s1_ref, top1_ref, last)

    @pl.when(last % 2 == 0)
    def _():
        absorb_last(s0_ref, top0_ref, last)

    a_s = acc_ref[...]
    gate_sel = jnp.concatenate([gate_ref[0, 0, e][1:2] for e in range(STEP_BLOCKS)], axis=1)
    out = part_ref[...] + gate_sel * (a_s[0:d] / a_s[d:d + 1])
    for e in range(STEP_BLOCKS):
        for r in range(NSA_REP):
            lane0 = e * QLANES + r * Q_BLOCK
            o_ref[0, r * HEAD_DIM:(r + 1) * HEAD_DIM, e * Q_BLOCK:(e + 1) * Q_BLOCK] = (
                out[:, lane0:lane0 + Q_BLOCK].astype(BF16))


def _nsa_attention(qT, qrT, kcmp, vcmpT, ksel, vselT, kwin, vwinT, gates):
    b, _, s = qT.shape
    nq = s // Q_BLOCK
    nblk = s // SEL_BLOCK
    ncmp = kcmp.shape[2]
    rows = NSA_REP * HEAD_DIM
    assert nblk <= LANES and s % SEL_CHUNK == 0 and s >= WIN_KEYS

    gt = gates[:, :, :NSA_HEADS * N_GATES].reshape(b, nq, Q_BLOCK, NSA_KV_GROUPS, NSA_REP, N_GATES)
    gt = gt.transpose(0, 3, 1, 5, 4, 2).reshape(b, NSA_KV_GROUPS, nq, N_GATES, QLANES)

    cstart = np.arange(ncmp) * CMP_STRIDE
    sstart = np.arange(nblk) * SEL_BLOCK
    ovT = ((cstart[None, :] < sstart[:, None] + SEL_BLOCK) & (cstart[None, :] + CMP_LEN > sstart[:, None]))
    ovT = jnp.asarray(ovT, BF16)
    onehot = jnp.asarray(np.arange(s)[:, None] // SEL_BLOCK == np.arange(LANES)[None, :], BF16)

    assert nq % STEP_BLOCKS == 0
    q_spec = pl.BlockSpec((1, rows, STEP_BLOCKS * Q_BLOCK), lambda bi, g, i: (bi, g, i))
    k_spec = pl.BlockSpec((1, s, LANES), lambda bi, g, i: (bi, 0, g))
    vT_spec = pl.BlockSpec((1, HEAD_DIM, s), lambda bi, g, i: (bi, g, 0))
    return pl.pallas_call(
        _attn_body,
        grid=(b, NSA_KV_GROUPS, nq // STEP_BLOCKS),
        in_specs=[q_spec, q_spec,
                  pl.BlockSpec((1, 1, ncmp, LANES), lambda bi, g, i: (bi, g, 0, 0)),
                  pl.BlockSpec((1, 1, HEAD_DIM, ncmp), lambda bi, g, i: (bi, g, 0, 0)),
                  k_spec, vT_spec, k_spec, vT_spec,
                  pl.BlockSpec((1, 1, STEP_BLOCKS, N_GATES, QLANES), lambda bi, g, i: (bi, g, i, 0, 0)),
                  _const_spec((nblk, ncmp)), _const_spec((s, LANES))],
        out_specs=q_spec,
        out_shape=jax.ShapeDtypeStruct((b, NSA_HEADS * HEAD_DIM, s), BF16),
        scratch_shapes=[pltpu.VMEM((2 * LANES, STEP_LANES), BF16),
                        pltpu.VMEM((SEL_CHUNK, STEP_LANES), F32), pltpu.VMEM((SEL_CHUNK, STEP_LANES), F32),
                        pltpu.VMEM((1, STEP_LANES), F32), pltpu.VMEM((HEAD_DIM + ONES_ROWS, STEP_LANES), F32),
                        pltpu.VMEM((HEAD_DIM, STEP_LANES), F32), pltpu.VMEM((STEP_BLOCKS, WIN_KEYS, QLANES), F32),
                        pltpu.VMEM((1, STEP_LANES), F32), pltpu.VMEM((1, STEP_LANES), F32)],
        compiler_params=_params(("arbitrary", "arbitrary", "arbitrary")),
        name="nsa_attention",
    )(qT, qrT, kcmp, vcmpT, ksel, vselT, kwin, vwinT, gt, ovT, onehot)


def _nsa_mixer(x, g_pre, w_in, b_gate, cmp_pos, cmp_w1, cmp_w2):
    qT, qrT, ksel, kwin, vselT, vwinT, kc_raw, vc_raw, gates = _nsa_project(x, g_pre, w_in, b_gate)
    kcmp, vcmpT = _nsa_compress(kc_raw, vc_raw, cmp_pos, cmp_w1, cmp_w2)
    return _nsa_attention(qT, qrT, kcmp, vcmpT, ksel, vselT, kwin, vwinT, gates)


def kernel(x, mix_norm_pre, mix_norm_post, ffn_norm_pre, ffn_norm_post, ffn_w_gate, ffn_w_up, ffn_w_down,
           conv_w_pw1, conv_b_pw1, conv_w_dw, conv_b_dw, conv_ln_g, conv_ln_b, conv_w_pw2, conv_b_pw2,
           nsa_w_in, nsa_b_gate, nsa_cmp_pos, nsa_cmp_w1, nsa_cmp_w2, nsa_w_out):
    b, s, d = x.shape
    depth = mix_norm_pre.shape[0]
    n_mixers = 2

    w_gate, w_up, w_down = ffn_w_gate.astype(BF16), ffn_w_up.astype(BF16), ffn_w_down.astype(BF16)

    def ffn(x, i, half, attn_out=None):
        y = _ffn_half(x.reshape(b * s, d), ffn_norm_pre[i, half], ffn_norm_post[i, half],
                      w_gate, w_up, w_down, (i, half), attn_out)
        return y.reshape(b, s, d)

    for i in range(depth):
        x = ffn(x, i, 0)
        j = i // n_mixers
        if i % n_mixers == 0:
            x = _conv_mixer(x, mix_norm_pre[i], mix_norm_post[i], conv_w_pw1[j], conv_b_pw1[j], conv_w_dw[j],
                            conv_b_dw[j], conv_ln_g[j], conv_ln_b[j], conv_w_pw2[j], conv_b_pw2[j])
            x = ffn(x, i, 1)
        else:
            attnT = _nsa_mixer(x, mix_norm_pre[i], nsa_w_in[j], nsa_b_gate[j], nsa_cmp_pos[j],
                               nsa_cmp_w1[j], nsa_cmp_w2[j])
            x = ffn(x, i, 1, (attnT, nsa_w_out[j], mix_norm_post[i]))
    return x
```

```python
import functools
import math

import jax
import jax.numpy as jnp
import numpy as np
from jax import lax
from jax.experimental import pallas as pl
from jax.experimental.pallas import tpu as pltpu

RMS_EPS = 1e-6
LN_EPS = 1e-5
FFN_RESIDUAL_WEIGHT = 0.5
CONV_WIDTH = 31
NSA_HEADS = 16
NSA_KV_GROUPS = 4
NSA_REP = NSA_HEADS // NSA_KV_GROUPS
HEAD_DIM = 64
ROT_HALF = HEAD_DIM // 8
ROPE_THETA = 500000.0
CMP_LEN = 32
CMP_STRIDE = 16
SEL_BLOCK = 64
N_SEL = 16
WINDOW = 512
Q_BLOCK = 128
N_GATES = 3
NEG_INF = -1e30

LANES = 128
SUBLANES = 8
V7X_VMEM_LIMIT = 56 * 1024 * 1024

BF16 = jnp.bfloat16
F32 = jnp.float32


def _params(sem):
    return pltpu.CompilerParams(dimension_semantics=sem, vmem_limit_bytes=V7X_VMEM_LIMIT)


def _const_spec(shape):
    nd = len(shape)
    return pl.BlockSpec(shape, lambda *_: (0,) * nd, pipeline_mode=pl.Buffered(1))


def _rms(x, gain):
    return x * lax.rsqrt(jnp.mean(x * x, axis=-1, keepdims=True) + RMS_EPS) * gain


def _sigmoid(x):
    return 1.0 / (1.0 + jnp.exp(-x))


def _dot(a, b):
    return jnp.dot(a, b, preferred_element_type=F32)


def _dot_nt(a, b):
    return lax.dot_general(a, b, (((1,), (1,)), ((), ())), preferred_element_type=F32)


def _dot_tn(a, b):
    return lax.dot_general(a, b, (((0,), (0,)), ((), ())), preferred_element_type=F32)


FFN_ROWS = 512
FFN_CHUNK = 512


def _swiglu_half_step(x, gpre_ref, gpost_ref, wg_ref, wu_ref, wd_ref, o_ref):
    h = _rms(x, gpre_ref[...]).astype(BF16)
    d_ff = wd_ref.shape[0]
    acc = jnp.zeros(x.shape, F32)
    for c0 in range(0, d_ff, FFN_CHUNK):
        c1 = min(c0 + FFN_CHUNK, d_ff)
        g = _dot(h, wg_ref[:, c0:c1])
        u = _dot(h, wu_ref[:, c0:c1])
        a = (g * _sigmoid(g) * u).astype(BF16)
        acc = acc + _dot(a, wd_ref[c0:c1, :])
    o_ref[...] = x + FFN_RESIDUAL_WEIGHT * _rms(acc, gpost_ref[...])


def _ffn_body(x_ref, *refs):
    _swiglu_half_step(x_ref[...], *refs)


def _attn_out_ffn_body(x_ref, aT_ref, wo_ref, gmix_ref, *refs):
    x = x_ref[...] + _rms(_dot_tn(aT_ref[0], wo_ref[...]), gmix_ref[...])
    _swiglu_half_step(x, *refs)


def _ffn_half(x2, g_pre, g_post, w_gate, w_up, w_down, layer, attn_out=None):
    t, d = x2.shape
    f = w_gate.shape[-1]
    tm = min(FFN_ROWS, t)
    row = pl.BlockSpec((tm, d), lambda i: (i, 0))

    def stacked(rows, cols):
        return pl.BlockSpec((None, None, rows, cols), lambda *_: (*layer, 0, 0), pipeline_mode=pl.Buffered(1))

    ffn_specs = [_const_spec((1, d)), _const_spec((1, d)), stacked(d, f), stacked(d, f), stacked(f, d)]
    ffn_args = (g_pre.reshape(1, d), g_post.reshape(1, d), w_gate, w_up, w_down)
    if attn_out is None:
        body, specs, args = _ffn_body, [row], (x2,)
    else:
        attnT, w_out, g_mix = attn_out
        tiles = attnT.shape[2] // tm
        body = _attn_out_ffn_body
        specs = [row, pl.BlockSpec((1, attnT.shape[1], tm), lambda i: (i // tiles, 0, i % tiles)),
                 _const_spec(w_out.shape), _const_spec((1, d))]
        args = (x2, attnT, w_out.astype(BF16), g_mix.reshape(1, d))
    return pl.pallas_call(
        body,
        grid=(t // tm,),
        in_specs=specs + ffn_specs,
        out_specs=row,
        out_shape=jax.ShapeDtypeStruct((t, d), F32),
        compiler_params=_params(("arbitrary",)),
        name="ffn_half",
    )(*args, *ffn_args)


CONV_ROWS = 512
CONV_HALO = 32
CONV_STRIP = 32


def _conv_body(x_ref, gpre_ref, w1_ref, b1_ref, wdw_ref, bdw_ref, lng_ref, lnb_ref,
               w2_ref, b2_ref, gpost_ref, o_ref, buf_ref, dw_ref, shift_ref):
    ts, d = x_ref.shape[1], x_ref.shape[2]

    @pl.when(pl.program_id(1) == 0)
    def _():
        buf_ref[0:CONV_HALO, :] = jnp.zeros((CONV_HALO, d), F32)

    x = x_ref[0]
    h = _rms(x, gpre_ref[...]).astype(BF16)
    p = _dot(h, w1_ref[...]) + b1_ref[...]
    buf_ref[CONV_HALO:CONV_HALO + ts, :] = p[:, :d] * _sigmoid(p[:, d:])

    base = CONV_HALO - (CONV_WIDTH - 1)
    span = shift_ref.shape[1]
    for b in range(1, SUBLANES):
        shift_ref[b - 1] = buf_ref[b:b + span, :]

    for r0 in range(0, ts, CONV_STRIP):
        acc = jnp.zeros((CONV_STRIP, d), F32)
        for k in range(CONV_WIDTH):
            a, b = divmod(base + k, SUBLANES)
            lo = r0 + a * SUBLANES
            win = buf_ref[lo:lo + CONV_STRIP, :] if b == 0 else shift_ref[b - 1, lo:lo + CONV_STRIP, :]
            acc = acc + wdw_ref[k:k + 1, :] * win
        dw_ref[r0:r0 + CONV_STRIP, :] = acc
    buf_ref[0:CONV_HALO, :] = buf_ref[ts:ts + CONV_HALO, :]

    c = dw_ref[...] + bdw_ref[...]
    mu = jnp.mean(c, axis=-1, keepdims=True)
    cc = c - mu
    var = jnp.mean(cc * cc, axis=-1, keepdims=True)
    y = cc * lax.rsqrt(var + LN_EPS) * lng_ref[...] + lnb_ref[...]
    y = (y * _sigmoid(y)).astype(BF16)
    out = _dot(y, w2_ref[...]) + b2_ref[...]
    o_ref[0] = x + _rms(out, gpost_ref[...])


def _conv_mixer(x, g_pre, g_post, w_pw1, b_pw1, w_dw, b_dw, ln_g, ln_b, w_pw2, b_pw2):
    b, s, d = x.shape
    ts = min(CONV_ROWS, s)
    tile = pl.BlockSpec((1, ts, d), lambda bi, j: (bi, j, 0))
    vec = lambda n: _const_spec((1, n))
    return pl.pallas_call(
        _conv_body,
        grid=(b, s // ts),
        in_specs=[tile, vec(d), _const_spec((d, 2 * d)), vec(2 * d), _const_spec((CONV_WIDTH, d)),
                  vec(d), vec(d), vec(d), _const_spec((d, d)), vec(d), vec(d)],
        out_specs=tile,
        out_shape=jax.ShapeDtypeStruct((b, s, d), F32),
        scratch_shapes=[pltpu.VMEM((ts + CONV_HALO, d), F32), pltpu.VMEM((ts, d), F32),
                        pltpu.VMEM((SUBLANES - 1, ts + (CONV_HALO - 1) // SUBLANES * SUBLANES, d), F32)],
        compiler_params=_params(("arbitrary", "arbitrary")),
        name="conv_mixer",
    )(x, g_pre.reshape(1, d), w_pw1.astype(BF16), b_pw1.reshape(1, 2 * d), w_dw,
      b_dw.reshape(1, d), ln_g.reshape(1, d), ln_b.reshape(1, d), w_pw2.astype(BF16),
      b_pw2.reshape(1, d), g_post.reshape(1, d))


PROJ_ROWS = 512
Q_WIDTH = NSA_HEADS * HEAD_DIM
KV_WIDTH = NSA_KV_GROUPS * HEAD_DIM
KPAD_WIDTH = NSA_KV_GROUPS * LANES
GATE_PAD = LANES
Q_SCALE = HEAD_DIM ** -0.5 * math.log2(math.e)


def _rope_tables(s):
    pos = jnp.arange(s, dtype=F32)
    inv_freq = ROPE_THETA ** (-jnp.arange(0, 2 * ROT_HALF, 2, dtype=F32) / (2 * ROT_HALF))
    ang = pos[:, None] * inv_freq[None, :]
    return jnp.cos(ang), jnp.sin(ang)


def _proj_body(x_ref, gpre_ref, wtok_ref, wtr_ref, bg_ref, cosT_ref, sinT_ref, ck_ref, s1_ref, s2_ref,
               qT_ref, qrT_ref, ksel_ref, kwin_ref, vselT_ref, vwinT_ref, kc_ref, vc_ref, gate_ref, raw_ref):
    tm = x_ref.shape[1]
    h = _rms(x_ref[0], gpre_ref[...]).astype(BF16)
    tok = _dot(h, wtok_ref[...])
    tr = _dot_nt(wtr_ref[...], h)

    q = tr[0:Q_WIDTH] * Q_SCALE
    qT_ref[0] = q.astype(BF16)
    q3 = q.reshape(NSA_HEADS, HEAD_DIM, tm)
    cos, sin = cosT_ref[...], sinT_ref[...]
    x1, x2 = q3[:, 0:ROT_HALF], q3[:, ROT_HALF:2 * ROT_HALF]
    qr = jnp.concatenate([x1 * cos - x2 * sin, x2 * cos + x1 * sin, q3[:, 2 * ROT_HALF:]], axis=1)
    qrT_ref[0] = qr.reshape(Q_WIDTH, tm).astype(BF16)
    vselT_ref[0] = tr[Q_WIDTH:Q_WIDTH + KV_WIDTH].astype(BF16)
    vwinT_ref[0] = tr[Q_WIDTH + KV_WIDTH:Q_WIDTH + 2 * KV_WIDTH].astype(BF16)

    ck, s1, s2 = ck_ref[...], s1_ref[...], s2_ref[...]
    for out_ref, base in ((ksel_ref, 0), (kwin_ref, KPAD_WIDTH)):
        for g in range(NSA_KV_GROUPS):
            k = tok[:, base + g * LANES:base + (g + 1) * LANES]
            kr = k * ck + pltpu.roll(k, ROT_HALF, 1) * s1 + pltpu.roll(k, LANES - ROT_HALF, 1) * s2
            out_ref[0, :, g * LANES:(g + 1) * LANES] = kr.astype(BF16)
    c0 = 2 * KPAD_WIDTH
    gate_ref[0] = _sigmoid(tok[:, c0 + 2 * KV_WIDTH:] + bg_ref[...])

    tiles = KV_WIDTH // LANES
    for t in range(2 * tiles):
        raw_ref[t] = tok[:, c0 + t * LANES:c0 + (t + 1) * LANES]
    nrow = tm // CMP_STRIDE
    low_half = lax.broadcasted_iota(jnp.int32, (nrow, LANES), 1) < HEAD_DIM
    for out_ref, base in ((kc_ref, 0), (vc_ref, tiles)):
        for h in range(tiles):
            for i in range(CMP_STRIDE // 2):
                a0 = raw_ref[base + h, pl.ds(2 * i, nrow, stride=CMP_STRIDE), :]
                a1 = raw_ref[base + h, pl.ds(2 * i + 1, nrow, stride=CMP_STRIDE), :]
                cols = slice(i * LANES, (i + 1) * LANES)
                out_ref[0, 2 * h, :, cols] = jnp.where(low_half, a0, pltpu.roll(a1, HEAD_DIM, 1))
                out_ref[0, 2 * h + 1, :, cols] = jnp.where(low_half, pltpu.roll(a0, HEAD_DIM, 1), a1)


def _nsa_project(x, g_pre, w_in, b_gate):
    b, s, d = x.shape
    tm = min(PROJ_ROWS, s)
    n_gate = NSA_HEADS * N_GATES
    w_q = w_in[:, :Q_WIDTH]
    w_kv = w_in[:, Q_WIDTH:Q_WIDTH + 6 * KV_WIDTH].reshape(d, 6, NSA_KV_GROUPS, HEAD_DIM)
    w_gate = w_in[:, Q_WIDTH + 6 * KV_WIDTH:]

    def padded(w):
        return jnp.pad(w, ((0, 0), (0, 0), (0, LANES - HEAD_DIM))).reshape(d, KPAD_WIDTH)

    w_tok = jnp.concatenate(
        [padded(w_kv[:, 2]), padded(w_kv[:, 4]), w_kv[:, 0].reshape(d, KV_WIDTH),
         w_kv[:, 1].reshape(d, KV_WIDTH), jnp.pad(w_gate, ((0, 0), (0, GATE_PAD - n_gate)))],
        axis=1).astype(BF16)
    w_tr = jnp.concatenate(
        [w_q, w_kv[:, 3].reshape(d, KV_WIDTH), w_kv[:, 5].reshape(d, KV_WIDTH)], axis=1).T.astype(BF16)
    bg = jnp.pad(b_gate, (0, GATE_PAD - n_gate)).reshape(1, GATE_PAD)

    cos, sin = _rope_tables(s)
    zeros = jnp.zeros_like(sin)
    pad_to = lambda parts, fill: jnp.concatenate(
        parts + [jnp.full((s, LANES - 2 * ROT_HALF), fill, F32)], axis=1)
    ck = pad_to([cos, cos], 1.0)
    s1 = pad_to([zeros, sin], 0.0)
    s2 = pad_to([-sin, zeros], 0.0)

    ntok, ntr = w_tok.shape[1], w_tr.shape[0]
    tok_major = lambda w: pl.BlockSpec((1, tm, w), lambda bi, j: (bi, j, 0))
    tr_major = lambda r: pl.BlockSpec((1, r, tm), lambda bi, j: (bi, 0, j))
    cmp_shape = (b, NSA_KV_GROUPS, s // CMP_STRIDE, CMP_STRIDE * HEAD_DIM)
    cmp_rows = pl.BlockSpec((1, NSA_KV_GROUPS, tm // CMP_STRIDE, CMP_STRIDE * HEAD_DIM),
                            lambda bi, j: (bi, 0, j, 0))
    return pl.pallas_call(
        _proj_body,
        grid=(b, s // tm),
        in_specs=[tok_major(d), _const_spec((1, d)), _const_spec((d, ntok)), _const_spec((ntr, d)),
                  _const_spec((1, GATE_PAD)),
                  pl.BlockSpec((ROT_HALF, tm), lambda bi, j: (0, j)),
                  pl.BlockSpec((ROT_HALF, tm), lambda bi, j: (0, j)),
                  pl.BlockSpec((tm, LANES), lambda bi, j: (j, 0)),
                  pl.BlockSpec((tm, LANES), lambda bi, j: (j, 0)),
                  pl.BlockSpec((tm, LANES), lambda bi, j: (j, 0))],
        out_specs=[tr_major(Q_WIDTH), tr_major(Q_WIDTH), tok_major(KPAD_WIDTH), tok_major(KPAD_WIDTH),
                   tr_major(KV_WIDTH), tr_major(KV_WIDTH), cmp_rows, cmp_rows,
                   tok_major(GATE_PAD)],
        out_shape=[jax.ShapeDtypeStruct((b, Q_WIDTH, s), BF16), jax.ShapeDtypeStruct((b, Q_WIDTH, s), BF16),
                   jax.ShapeDtypeStruct((b, s, KPAD_WIDTH), BF16), jax.ShapeDtypeStruct((b, s, KPAD_WIDTH), BF16),
                   jax.ShapeDtypeStruct((b, KV_WIDTH, s), BF16), jax.ShapeDtypeStruct((b, KV_WIDTH, s), BF16),
                   jax.ShapeDtypeStruct(cmp_shape, F32), jax.ShapeDtypeStruct(cmp_shape, F32),
                   jax.ShapeDtypeStruct((b, s, GATE_PAD), F32)],
        scratch_shapes=[pltpu.VMEM((2 * KV_WIDTH // LANES, tm, LANES), F32)],
        compiler_params=_params(("arbitrary", "arbitrary")),
        name="nsa_project",
    )(x, g_pre.reshape(1, d), w_tok, w_tr, bg, cos.T, sin.T, ck, s1, s2)


def _compress_body(rk_ref, rv_ref, pos_ref, w1_ref, w2k_ref, w2vT_ref, kc_ref, vcT_ref):
    nrow = rk_ref.shape[2]
    half = rk_ref.shape[3]

    def hidden(r, t):
        top = _dot((r + pos_ref[t, 0:1, :]).astype(BF16), w1_ref[t, 0:half, :])
        bot = _dot((r + pos_ref[t, 1:2, :]).astype(BF16), w1_ref[t, half:2 * half, :])
        hid = top + pltpu.roll(bot, nrow - 1, 0)
        return (hid * _sigmoid(hid)).astype(BF16)

    kc_ref[0, 0] = _dot(hidden(rk_ref[0, 0], 0), w2k_ref[...]).astype(BF16)
    vcT_ref[0, 0] = _dot_nt(w2vT_ref[...], hidden(rv_ref[0, 0], 1)).astype(BF16)


def _nsa_compress(kc_raw, vc_raw, cmp_pos, cmp_w1, cmp_w2):
    b, _, nrow, half = kc_raw.shape
    hid = cmp_w1.shape[-1]
    pos = cmp_pos.reshape(2, 2, half)
    w2k = jnp.pad(cmp_w2[0], ((0, 0), (0, LANES - HEAD_DIM))).astype(BF16)
    w2vT = cmp_w2[1].T.astype(BF16)
    blk = pl.BlockSpec((1, 1, nrow, half), lambda bi, g: (bi, g, 0, 0))
    return pl.pallas_call(
        _compress_body,
        grid=(b, NSA_KV_GROUPS),
        in_specs=[blk, blk, _const_spec((2, 2, half)), _const_spec((2, 2 * half, hid)),
                  _const_spec((hid, LANES)), _const_spec((HEAD_DIM, hid))],
        out_specs=[pl.BlockSpec((1, 1, nrow, LANES), lambda bi, g: (bi, g, 0, 0)),
                   pl.BlockSpec((1, 1, HEAD_DIM, nrow), lambda bi, g: (bi, g, 0, 0))],
        out_shape=[jax.ShapeDtypeStruct((b, NSA_KV_GROUPS, nrow, LANES), BF16),
                   jax.ShapeDtypeStruct((b, NSA_KV_GROUPS, HEAD_DIM, nrow), BF16)],
        compiler_params=_params(("arbitrary", "arbitrary")),
        name="nsa_compress",
    )(kc_raw, vc_raw, pos, cmp_w1.astype(BF16), w2k, w2vT)


SEL_CHUNK = 512
CHUNK_BLOCKS = SEL_CHUNK // SEL_BLOCK
WIN_KEYS = WINDOW + Q_BLOCK
QLANES = NSA_REP * Q_BLOCK
STEP_BLOCKS = 4
STEP_LANES = STEP_BLOCKS * QLANES
N_FORCED = 3
ONES_ROWS = 16
CMP_ROWS_STEP = 256
MASKED_MAX_FLOOR = -1e29


def _with_ones(vT):
    return jnp.concatenate([vT, jnp.ones((ONES_ROWS, vT.shape[1]), BF16)], axis=0)


def _attn_body(qT_ref, qrT_ref, kc_ref, vcT_ref, ksel_ref, vselT_ref, kwin_ref, vwinT_ref, gate_ref,
               ovT_ref, oh_ref, o_ref, qaug_ref, s0_ref, s1_ref, m_ref, acc_ref, part_ref, swin_ref,
               top0_ref, top1_ref):
    i = pl.program_id(2)
    t_step = i * (STEP_BLOCKS * Q_BLOCK)
    nblk = ovT_ref.shape[0]
    ncmp = kc_ref.shape[2]
    n_top = min(N_SEL, nblk) - N_FORCED

    def lanes_of_heads(ref, e):
        cols = slice(e * Q_BLOCK, (e + 1) * Q_BLOCK)
        parts = [ref[0, r * HEAD_DIM:(r + 1) * HEAD_DIM, cols] for r in range(NSA_REP)]
        return jnp.concatenate(
            [jnp.concatenate(parts, axis=1), jnp.zeros((LANES - HEAD_DIM, QLANES), BF16)], axis=0)

    def positions(e):
        return t_step + e * Q_BLOCK + lax.broadcasted_iota(jnp.int32, (1, Q_BLOCK), 1)

    pos_step = jnp.concatenate([positions(e) for e in range(STEP_BLOCKS) for _ in range(NSA_REP)], axis=1)
    d = HEAD_DIM

    def chunk_start(j):
        return pl.multiple_of(j * SEL_CHUNK, SEL_CHUNK)

    def scores_into(s_ref, top_ref, j):
        k0 = chunk_start(j)
        ka = jnp.concatenate([ksel_ref[0, pl.ds(k0, SEL_CHUNK), :], oh_ref[pl.ds(k0, SEL_CHUNK), :]], axis=1)
        s = _dot(ka, qaug_ref[...])
        s_ref[...] = s
        top_ref[...] = jnp.max(s, axis=0, keepdims=True)

    def block_select_and_window(e, cmp_rows, steady):
        t0 = t_step + e * Q_BLOCK
        lanes = slice(e * QLANES, (e + 1) * QLANES)
        qT = lanes_of_heads(qT_ref, e)
        qrT = lanes_of_heads(qrT_ref, e)
        pos1 = positions(e)
        pos = jnp.concatenate([pos1] * NSA_REP, axis=1)
        qaug_ref[0:LANES, lanes] = qrT
        s = _dot(kc_ref[0, 0, 0:cmp_rows, :], qT)
        w0 = pl.multiple_of(t0 - WINDOW if steady else jnp.maximum(t0 - WINDOW, 0), Q_BLOCK)
        swin_ref[e] = _dot(kwin_ref[0, pl.ds(w0, WIN_KEYS), :], qrT)
        s0_ref[:, lanes] = _dot(ksel_ref[0, 0:SEL_CHUNK, :], qrT)
        cend = lax.broadcasted_iota(jnp.int32, (cmp_rows, QLANES), 0) * CMP_STRIDE + (CMP_LEN - 1)
        s = jnp.where(cend <= pos, s, NEG_INF)
        m = jnp.maximum(jnp.max(s, axis=0, keepdims=True), MASKED_MAX_FLOOR)
        p = jnp.exp2(s - m)
        inv_l = 1.0 / jnp.maximum(jnp.sum(p, axis=0, keepdims=True), 1e-30)
        o_cmp = _dot(vcT_ref[0, 0, :, 0:cmp_rows], p.astype(BF16)) * inv_l

        pn = p * inv_l
        ph = pn[:, 0:Q_BLOCK]
        for r in range(1, NSA_REP):
            ph = ph + pn[:, r * Q_BLOCK:(r + 1) * Q_BLOCK]
        hi = ph.astype(BF16)
        lo = (ph - hi.astype(F32)).astype(BF16)
        ovT = ovT_ref[:, 0:cmp_rows]
        imp = _dot(ovT, hi) + _dot(ovT, lo)

        blk = lax.broadcasted_iota(jnp.int32, (nblk, Q_BLOCK), 0)
        cur = pos1 // SEL_BLOCK
        forced = (blk == 0) | (blk == cur) | (blk == cur - 1)
        v0 = jnp.where(forced, -1.0, jnp.where(blk * SEL_BLOCK <= pos1, imp, -1.0))
        n_piece = WIN_KEYS // Q_BLOCK
        kk = lax.broadcasted_iota(jnp.int32, (Q_BLOCK, QLANES), 0)
        qq = lax.broadcasted_iota(jnp.int32, (Q_BLOCK, QLANES), 1) % Q_BLOCK
        win = {"m": None, "acc": None}

        def piece_max(w):
            rows = slice(w * Q_BLOCK, (w + 1) * Q_BLOCK)
            sp = swin_ref[e, rows, :]
            if not steady:
                kpos = w0 + w * Q_BLOCK + kk
                sp = jnp.where(kpos <= pos, jnp.where(kpos > pos - WINDOW, sp, NEG_INF), NEG_INF)
            elif w == 0:
                sp = jnp.where(kk > qq, sp, NEG_INF)
            elif w == n_piece - 1:
                sp = jnp.where(kk <= qq, sp, NEG_INF)
            if not steady or w in (0, n_piece - 1):
                swin_ref[e, rows, :] = sp
            top = jnp.max(sp, axis=0, keepdims=True)
            win["m"] = top if win["m"] is None else jnp.maximum(win["m"], top)

        def piece_absorb(w):
            rows = slice(w * Q_BLOCK, (w + 1) * Q_BLOCK)
            p = jnp.exp2(swin_ref[e, rows, :] - win["m"]).astype(BF16)
            k0 = pl.multiple_of(w0 + w * Q_BLOCK, Q_BLOCK)
            pv = _dot(_with_ones(vwinT_ref[0, :, pl.ds(k0, Q_BLOCK)]), p)
            win["acc"] = pv if win["acc"] is None else win["acc"] + pv

        for w in range(n_piece):
            piece_max(w)
        left = v0
        pending = list(range(n_piece))
        for r in range(n_top):
            left = jnp.where(left == jnp.max(left, axis=0, keepdims=True), -2.0, left)
            if pending and r % 2 == 0:
                piece_absorb(pending.pop(0))
        for w in pending:
            piece_absorb(w)
        taken = left != v0
        n_taken = jnp.sum(jnp.where(taken, jnp.where(v0 >= 0.0, 1.0, 0.0), 0.0), axis=0, keepdims=True)

        def store_bias(left):
            bias = jnp.where(forced, 0.0, jnp.where(left != v0, 0.0, NEG_INF)).astype(BF16)
            bias = jnp.concatenate([bias] * NSA_REP, axis=1)
            if nblk < LANES:
                bias = jnp.concatenate([bias, jnp.zeros((LANES - nblk, QLANES), BF16)], axis=0)
            qaug_ref[LANES:2 * LANES, lanes] = bias

        store_bias(left)
        bias0 = jnp.where(forced[0:CHUNK_BLOCKS], 0.0,
                          jnp.where(left[0:CHUNK_BLOCKS] != v0[0:CHUNK_BLOCKS], 0.0, NEG_INF))
        bias0 = jnp.concatenate([bias0] * NSA_REP, axis=1)
        top = None
        for n in range(CHUNK_BLOCKS):
            rows = slice(n * SEL_BLOCK, (n + 1) * SEL_BLOCK)
            sb = s0_ref[rows, lanes] + bias0[n:n + 1, :]
            s0_ref[rows, lanes] = sb
            bt = jnp.max(sb, axis=0, keepdims=True)
            top = bt if top is None else jnp.maximum(top, bt)
        top0_ref[:, lanes] = top

        a_w = win["acc"]
        gate = gate_ref[0, 0, e]
        part_ref[:, lanes] = gate[0:1] * o_cmp + gate[2:3] * (a_w[0:d] / a_w[d:d + 1])

        def select_exactly():
            def drop_first_max(_, val):
                top = jnp.max(val, axis=0, keepdims=True)
                first = jnp.min(jnp.where(val == top, blk, nblk), axis=0, keepdims=True)
                return jnp.where(blk == first, -2.0, val)

            store_bias(lax.fori_loop(0, n_top, drop_first_max, v0))

        return jnp.max(n_taken) > n_top, select_exactly

    def select_and_window(cmp_rows, steady):
        redo = [block_select_and_window(e, cmp_rows, steady) for e in range(STEP_BLOCKS)]
        any_overflowed = False
        for overflowed, select_exactly in redo:
            pl.when(overflowed)(select_exactly)
            any_overflowed = jnp.logical_or(any_overflowed, overflowed)
        pl.when(any_overflowed)(functools.partial(scores_into, s0_ref, top0_ref, 0))

    step_tokens = STEP_BLOCKS * Q_BLOCK

    def visible_cmp(step):
        return (step * step_tokens + step_tokens - CMP_LEN) // CMP_STRIDE + 1

    n_steps = ksel_ref.shape[1] // step_tokens
    steady_from = -(-WINDOW // step_tokens)
    first_rows = min(CMP_ROWS_STEP, ncmp)
    assert visible_cmp(steady_from - 1) <= first_rows
    pl.when(i < steady_from)(functools.partial(select_and_window, first_rows, False))
    lo = steady_from
    for rows in range(first_rows, ncmp + 1, CMP_ROWS_STEP):
        hi = next((step for step in range(lo, n_steps) if visible_cmp(step) > rows), n_steps)
        if hi > lo:
            pl.when((i >= lo) & (i < hi))(functools.partial(select_and_window, rows, True))
            lo = hi
    assert lo == n_steps

    def absorb(s_ref, top_ref, j, diagonal):
        k0 = chunk_start(j)
        s = s_ref[...]
        if diagonal:
            kpos = k0 + lax.broadcasted_iota(jnp.int32, (SEL_CHUNK, STEP_LANES), 0)
            s = jnp.where(kpos <= pos_step, s, NEG_INF)
            top = jnp.max(s, axis=0, keepdims=True)
        else:
            top = top_ref[...]
        m = m_ref[...]
        m_new = jnp.maximum(m, top)
        p = jnp.exp2(s - m_new).astype(BF16)
        acc_ref[...] = jnp.exp2(m - m_new) * acc_ref[...] + _dot(
            _with_ones(vselT_ref[0, :, pl.ds(k0, SEL_CHUNK)]), p)
        m_ref[...] = m_new

    def absorb_own_positions(s_ref, j):
        k0 = chunk_start(j)
        kk = lax.broadcasted_iota(jnp.int32, (Q_BLOCK, QLANES), 0)
        qq = lax.broadcasted_iota(jnp.int32, (Q_BLOCK, QLANES), 1) % Q_BLOCK
        for e in range(STEP_BLOCKS):
            lanes = slice(e * QLANES, (e + 1) * QLANES)
            rows = (e + 1) * Q_BLOCK
            s = jnp.where(kk <= qq, s_ref[e * Q_BLOCK:rows, lanes], NEG_INF)
            if e > 0:
                s = jnp.concatenate([s_ref[0:e * Q_BLOCK, lanes], s], axis=0)
            m = m_ref[:, lanes]
            m_new = jnp.maximum(m, jnp.max(s, axis=0, keepdims=True))
            p = jnp.exp2(s - m_new).astype(BF16)
            acc_ref[:, lanes] = jnp.exp2(m - m_new) * acc_ref[:, lanes] + _dot(
                _with_ones(vselT_ref[0, :, pl.ds(k0, rows)]), p)
            m_ref[:, lanes] = m_new

    own_chunk = STEP_BLOCKS * Q_BLOCK == SEL_CHUNK

    def absorb_last(s_ref, top_ref, j):
        if own_chunk:
            absorb_own_positions(s_ref, j)
        else:
            absorb(s_ref, top_ref, j, True)

    def scores_last_into(s_ref, top_ref, j):
        if not own_chunk:
            scores_into(s_ref, top_ref, j)
            return
        k0 = chunk_start(j)
        for e in range(STEP_BLOCKS):
            lanes = slice(e * QLANES, (e + 1) * QLANES)
            rows = (e + 1) * Q_BLOCK
            ka = jnp.concatenate([ksel_ref[0, pl.ds(k0, rows), :], oh_ref[pl.ds(k0, rows), :]], axis=1)
            s_ref[0:rows, lanes] = _dot(ka, qaug_ref[:, lanes])

    last = t_step // SEL_CHUNK
    m_ref[...] = jnp.full(m_ref.shape, NEG_INF, F32)
    acc_ref[...] = jnp.zeros(acc_ref.shape, F32)

    def pair(t):
        scores_into(s1_ref, top1_ref, 2 * t + 1)
        absorb(s0_ref, top0_ref, 2 * t, False)
        scores_into(s0_ref, top0_ref, 2 * t + 2)
        absorb(s1_ref, top1_ref, 2 * t + 1, False)

    def pair_step(t, carry):
        pair(t)
        return carry

    lax.fori_loop(0, last // 2, pair_step, 0)

    @pl.when(last % 2 == 1)
    def _():
        scores_last_into(s1_ref, top1_ref, last)
        absorb(s0_ref, top0_ref, last - 1, False)
        absorb_last(s1_ref, top1_ref, last)

    @pl.when(last % 2 == 0)
    def _():
        absorb_last(s0_ref, top0_ref, last)

    a_s = acc_ref[...]
    gate_sel = jnp.concatenate([gate_ref[0, 0, e][1:2] for e in range(STEP_BLOCKS)], axis=1)
    out = part_ref[...] + gate_sel * (a_s[0:d] / a_s[d:d + 1])
    for e in range(STEP_BLOCKS):
        for r in range(NSA_REP):
            lane0 = e * QLANES + r * Q_BLOCK
            o_ref[0, r * HEAD_DIM:(r + 1) * HEAD_DIM, e * Q_BLOCK:(e + 1) * Q_BLOCK] = (
                out[:, lane0:lane0 + Q_BLOCK].astype(BF16))


def _nsa_attention(qT, qrT, kcmp, vcmpT, ksel, vselT, kwin, vwinT, gates):
    b, _, s = qT.shape
    nq = s // Q_BLOCK
    nblk = s // SEL_BLOCK
    ncmp = kcmp.shape[2]
    rows = NSA_REP * HEAD_DIM
    assert nblk <= LANES and s % SEL_CHUNK == 0 and s >= WIN_KEYS

    gt = gates[:, :, :NSA_HEADS * N_GATES].reshape(b, nq, Q_BLOCK, NSA_KV_GROUPS, NSA_REP, N_GATES)
    gt = gt.transpose(0, 3, 1, 5, 4, 2).reshape(b, NSA_KV_GROUPS, nq, N_GATES, QLANES)

    cstart = np.arange(ncmp) * CMP_STRIDE
    sstart = np.arange(nblk) * SEL_BLOCK
    ovT = ((cstart[None, :] < sstart[:, None] + SEL_BLOCK) & (cstart[None, :] + CMP_LEN > sstart[:, None]))
    ovT = jnp.asarray(ovT, BF16)
    onehot = jnp.asarray(np.arange(s)[:, None] // SEL_BLOCK == np.arange(LANES)[None, :], BF16)

    assert nq % STEP_BLOCKS == 0
    q_spec = pl.BlockSpec((1, rows, STEP_BLOCKS * Q_BLOCK), lambda bi, g, i: (bi, g, i))
    k_spec = pl.BlockSpec((1, s, LANES), lambda bi, g, i: (bi, 0, g))
    vT_spec = pl.BlockSpec((1, HEAD_DIM, s), lambda bi, g, i: (bi, g, 0))
    return pl.pallas_call(
        _attn_body,
        grid=(b, NSA_KV_GROUPS, nq // STEP_BLOCKS),
        in_specs=[q_spec, q_spec,
                  pl.BlockSpec((1, 1, ncmp, LANES), lambda bi, g, i: (bi, g, 0, 0)),
                  pl.BlockSpec((1, 1, HEAD_DIM, ncmp), lambda bi, g, i: (bi, g, 0, 0)),
                  k_spec, vT_spec, k_spec, vT_spec,
                  pl.BlockSpec((1, 1, STEP_BLOCKS, N_GATES, QLANES), lambda bi, g, i: (bi, g, i, 0, 0)),
                  _const_spec((nblk, ncmp)), _const_spec((s, LANES))],
        out_specs=q_spec,
        out_shape=jax.ShapeDtypeStruct((b, NSA_HEADS * HEAD_DIM, s), BF16),
        scratch_shapes=[pltpu.VMEM((2 * LANES, STEP_LANES), BF16),
                        pltpu.VMEM((SEL_CHUNK, STEP_LANES), F32), pltpu.VMEM((SEL_CHUNK, STEP_LANES), F32),
                        pltpu.VMEM((1, STEP_LANES), F32), pltpu.VMEM((HEAD_DIM + ONES_ROWS, STEP_LANES), F32),
                        pltpu.VMEM((HEAD_DIM, STEP_LANES), F32), pltpu.VMEM((STEP_BLOCKS, WIN_KEYS, QLANES), F32),
                        pltpu.VMEM((1, STEP_LANES), F32), pltpu.VMEM((1, STEP_LANES), F32)],
        compiler_params=_params(("arbitrary", "arbitrary", "arbitrary")),
        name="nsa_attention",
    )(qT, qrT, kcmp, vcmpT, ksel, vselT, kwin, vwinT, gt, ovT, onehot)


def _nsa_mixer(x, g_pre, w_in, b_gate, cmp_pos, cmp_w1, cmp_w2):
    qT, qrT, ksel, kwin, vselT, vwinT, kc_raw, vc_raw, gates = _nsa_project(x, g_pre, w_in, b_gate)
    kcmp, vcmpT = _nsa_compress(kc_raw, vc_raw, cmp_pos, cmp_w1, cmp_w2)
    return _nsa_attention(qT, qrT, kcmp, vcmpT, ksel, vselT, kwin, vwinT, gates)


def kernel(x, mix_norm_pre, mix_norm_post, ffn_norm_pre, ffn_norm_post, ffn_w_gate, ffn_w_up, ffn_w_down,
           conv_w_pw1, conv_b_pw1, conv_w_dw, conv_b_dw, conv_ln_g, conv_ln_b, conv_w_pw2, conv_b_pw2,
           nsa_w_in, nsa_b_gate, nsa_cmp_pos, nsa_cmp_w1, nsa_cmp_w2, nsa_w_out):
    b, s, d = x.shape
    depth = mix_norm_pre.shape[0]
    n_mixers = 2

    w_gate, w_up, w_down = ffn_w_gate.astype(BF16), ffn_w_up.astype(BF16), ffn_w_down.astype(BF16)

    def ffn(x, i, half, attn_out=None):
        y = _ffn_half(x.reshape(b * s, d), ffn_norm_pre[i, half], ffn_norm_post[i, half],
                      w_gate, w_up, w_down, (i, half), attn_out)
        return y.reshape(b, s, d)

    for i in range(depth):
        x = ffn(x, i, 0)
        j = i // n_mixers
        if i % n_mixers == 0:
            x = _conv_mixer(x, mix_norm_pre[i], mix_norm_post[i], conv_w_pw1[j], conv_b_pw1[j], conv_w_dw[j],
                            conv_b_dw[j], conv_ln_g[j], conv_ln_b[j], conv_w_pw2[j], conv_b_pw2[j])
            x = ffn(x, i, 1)
        else:
            attnT = _nsa_mixer(x, mix_norm_pre[i], nsa_w_in[j], nsa_b_gate[j], nsa_cmp_pos[j],
                               nsa_cmp_w1[j], nsa_cmp_w2[j])
            x = ffn(x, i, 1, (attnT, nsa_w_out[j], mix_norm_post[i]))
    return x
```

```python
import functools
import math

import jax
import jax.numpy as jnp
import numpy as np
from jax import lax
from jax.experimental import pallas as pl
from jax.experimental.pallas import tpu as pltpu

RMS_EPS = 1e-6
LN_EPS = 1e-5
FFN_RESIDUAL_WEIGHT = 0.5
CONV_WIDTH = 31
NSA_HEADS = 16
NSA_KV_GROUPS = 4
NSA_REP = NSA_HEADS // NSA_KV_GROUPS
HEAD_DIM = 64
ROT_HALF = HEAD_DIM // 8
ROPE_THETA = 500000.0
CMP_LEN = 32
CMP_STRIDE = 16
SEL_BLOCK = 64
N_SEL = 16
WINDOW = 512
Q_BLOCK = 128
N_GATES = 3
NEG_INF = -1e30

LANES = 128
SUBLANES = 8
V7X_VMEM_LIMIT = 56 * 1024 * 1024

BF16 = jnp.bfloat16
F32 = jnp.float32


def _params(sem):
    return pltpu.CompilerParams(dimension_semantics=sem, vmem_limit_bytes=V7X_VMEM_LIMIT)


def _const_spec(shape):
    nd = len(shape)
    return pl.BlockSpec(shape, lambda *_: (0,) * nd, pipeline_mode=pl.Buffered(1))


def _rms(x, gain):
    return x * lax.rsqrt(jnp.mean(x * x, axis=-1, keepdims=True) + RMS_EPS) * gain


def _sigmoid(x):
    return 1.0 / (1.0 + jnp.exp(-x))


def _dot(a, b):
    return jnp.dot(a, b, preferred_element_type=F32)


def _dot_nt(a, b):
    return lax.dot_general(a, b, (((1,), (1,)), ((), ())), preferred_element_type=F32)


def _dot_tn(a, b):
    return lax.dot_general(a, b, (((0,), (0,)), ((), ())), preferred_element_type=F32)


FFN_ROWS = 512
FFN_CHUNK = 512


def _swiglu_half_step(x, gpre_ref, gpost_ref, wg_ref, wu_ref, wd_ref, o_ref):
    h = _rms(x, gpre_ref[...]).astype(BF16)
    d_ff = wd_ref.shape[0]
    acc = jnp.zeros(x.shape, F32)
    for c0 in range(0, d_ff, FFN_CHUNK):
        c1 = min(c0 + FFN_CHUNK, d_ff)
        g = _dot(h, wg_ref[:, c0:c1].astype(BF16))
        u = _dot(h, wu_ref[:, c0:c1].astype(BF16))
        a = (g * _sigmoid(g) * u).astype(BF16)
        acc = acc + _dot(a, wd_ref[c0:c1, :].astype(BF16))
    o_ref[...] = x + FFN_RESIDUAL_WEIGHT * _rms(acc, gpost_ref[...])


def _ffn_body(x_ref, *refs):
    _swiglu_half_step(x_ref[...], *refs)


def _attn_out_ffn_body(x_ref, aT_ref, wo_ref, gmix_ref, *refs):
    x = x_ref[...] + _rms(_dot_tn(aT_ref[0], wo_ref[...]), gmix_ref[...])
    _swiglu_half_step(x, *refs)


def _ffn_half(x2, g_pre, g_post, w_gate, w_up, w_down, layer, attn_out=None):
    t, d = x2.shape
    f = w_gate.shape[-1]
    tm = min(FFN_ROWS, t)
    row = pl.BlockSpec((tm, d), lambda i: (i, 0))

    def stacked(rows, cols):
        return pl.BlockSpec((None, None, rows, cols), lambda *_: (*layer, 0, 0), pipeline_mode=pl.Buffered(1))

    ffn_specs = [_const_spec((1, d)), _const_spec((1, d)), stacked(d, f), stacked(d, f), stacked(f, d)]
    ffn_args = (g_pre.reshape(1, d), g_post.reshape(1, d), w_gate, w_up, w_down)
    if attn_out is None:
        body, specs, args = _ffn_body, [row], (x2,)
    else:
        attnT, w_out, g_mix = attn_out
        tiles = attnT.shape[2] // tm
        body = _attn_out_ffn_body
        specs = [row, pl.BlockSpec((1, attnT.shape[1], tm), lambda i: (i // tiles, 0, i % tiles)),
                 _const_spec(w_out.shape), _const_spec((1, d))]
        args = (x2, attnT, w_out.astype(BF16), g_mix.reshape(1, d))
    return pl.pallas_call(
        body,
        grid=(t // tm,),
        in_specs=specs + ffn_specs,
        out_specs=row,
        out_shape=jax.ShapeDtypeStruct((t, d), F32),
        compiler_params=_params(("arbitrary",)),
        name="ffn_half",
    )(*args, *ffn_args)


CONV_ROWS = 512
CONV_HALO = 32
CONV_STRIP = 32


def _conv_body(x_ref, gpre_ref, w1_ref, b1_ref, wdw_ref, bdw_ref, lng_ref, lnb_ref,
               w2_ref, b2_ref, gpost_ref, o_ref, buf_ref, dw_ref, shift_ref):
    ts, d = x_ref.shape[1], x_ref.shape[2]

    @pl.when(pl.program_id(1) == 0)
    def _():
        buf_ref[0:CONV_HALO, :] = jnp.zeros((CONV_HALO, d), F32)

    x = x_ref[0]
    h = _rms(x, gpre_ref[...]).astype(BF16)
    p = _dot(h, w1_ref[...]) + b1_ref[...]
    buf_ref[CONV_HALO:CONV_HALO + ts, :] = p[:, :d] * _sigmoid(p[:, d:])

    base = CONV_HALO - (CONV_WIDTH - 1)
    span = shift_ref.shape[1]
    for b in range(1, SUBLANES):
        shift_ref[b - 1] = buf_ref[b:b + span, :]

    for r0 in range(0, ts, CONV_STRIP):
        acc = jnp.zeros((CONV_STRIP, d), F32)
        for k in range(CONV_WIDTH):
            a, b = divmod(base + k, SUBLANES)
            lo = r0 + a * SUBLANES
            win = buf_ref[lo:lo + CONV_STRIP, :] if b == 0 else shift_ref[b - 1, lo:lo + CONV_STRIP, :]
            acc = acc + wdw_ref[k:k + 1, :] * win
        dw_ref[r0:r0 + CONV_STRIP, :] = acc
    buf_ref[0:CONV_HALO, :] = buf_ref[ts:ts + CONV_HALO, :]

    c = dw_ref[...] + bdw_ref[...]
    mu = jnp.mean(c, axis=-1, keepdims=True)
    cc = c - mu
    var = jnp.mean(cc * cc, axis=-1, keepdims=True)
    y = cc * lax.rsqrt(var + LN_EPS) * lng_ref[...] + lnb_ref[...]
    y = (y * _sigmoid(y)).astype(BF16)
    out = _dot(y, w2_ref[...]) + b2_ref[...]
    o_ref[0] = x + _rms(out, gpost_ref[...])


def _conv_mixer(x, g_pre, g_post, w_pw1, b_pw1, w_dw, b_dw, ln_g, ln_b, w_pw2, b_pw2):
    b, s, d = x.shape
    ts = min(CONV_ROWS, s)
    tile = pl.BlockSpec((1, ts, d), lambda bi, j: (bi, j, 0))
    vec = lambda n: _const_spec((1, n))
    return pl.pallas_call(
        _conv_body,
        grid=(b, s // ts),
        in_specs=[tile, vec(d), _const_spec((d, 2 * d)), vec(2 * d), _const_spec((CONV_WIDTH, d)),
                  vec(d), vec(d), vec(d), _const_spec((d, d)), vec(d), vec(d)],
        out_specs=tile,
        out_shape=jax.ShapeDtypeStruct((b, s, d), F32),
        scratch_shapes=[pltpu.VMEM((ts + CONV_HALO, d), F32), pltpu.VMEM((ts, d), F32),
                        pltpu.VMEM((SUBLANES - 1, ts + (CONV_HALO - 1) // SUBLANES * SUBLANES, d), F32)],
        compiler_params=_params(("arbitrary", "arbitrary")),
        name="conv_mixer",
    )(x, g_pre.reshape(1, d), w_pw1.astype(BF16), b_pw1.reshape(1, 2 * d), w_dw,
      b_dw.reshape(1, d), ln_g.reshape(1, d), ln_b.reshape(1, d), w_pw2.astype(BF16),
      b_pw2.reshape(1, d), g_post.reshape(1, d))


PROJ_ROWS = 512
Q_WIDTH = NSA_HEADS * HEAD_DIM
KV_WIDTH = NSA_KV_GROUPS * HEAD_DIM
KPAD_WIDTH = NSA_KV_GROUPS * LANES
GATE_PAD = LANES
Q_SCALE = HEAD_DIM ** -0.5 * math.log2(math.e)


def _rope_tables(s):
    pos = jnp.arange(s, dtype=F32)
    inv_freq = ROPE_THETA ** (-jnp.arange(0, 2 * ROT_HALF, 2, dtype=F32) / (2 * ROT_HALF))
    ang = pos[:, None] * inv_freq[None, :]
    return jnp.cos(ang), jnp.sin(ang)


def _proj_body(x_ref, gpre_ref, wtok_ref, wtr_ref, bg_ref, cosT_ref, sinT_ref, ck_ref, s1_ref, s2_ref,
               qT_ref, qrT_ref, ksel_ref, kwin_ref, vselT_ref, vwinT_ref, kc_ref, vc_ref, gate_ref, raw_ref):
    tm = x_ref.shape[1]
    h = _rms(x_ref[0], gpre_ref[...]).astype(BF16)
    tok = _dot(h, wtok_ref[...])
    tr = _dot_nt(wtr_ref[...], h)

    q = tr[0:Q_WIDTH] * Q_SCALE
    qT_ref[0] = q.astype(BF16)
    q3 = q.reshape(NSA_HEADS, HEAD_DIM, tm)
    cos, sin = cosT_ref[...], sinT_ref[...]
    x1, x2 = q3[:, 0:ROT_HALF], q3[:, ROT_HALF:2 * ROT_HALF]
    qr = jnp.concatenate([x1 * cos - x2 * sin, x2 * cos + x1 * sin, q3[:, 2 * ROT_HALF:]], axis=1)
    qrT_ref[0] = qr.reshape(Q_WIDTH, tm).astype(BF16)
    vselT_ref[0] = tr[Q_WIDTH:Q_WIDTH + KV_WIDTH].astype(BF16)
    vwinT_ref[0] = tr[Q_WIDTH + KV_WIDTH:Q_WIDTH + 2 * KV_WIDTH].astype(BF16)

    ck, s1, s2 = ck_ref[...], s1_ref[...], s2_ref[...]
    for out_ref, base in ((ksel_ref, 0), (kwin_ref, KPAD_WIDTH)):
        for g in range(NSA_KV_GROUPS):
            k = tok[:, base + g * LANES:base + (g + 1) * LANES]
            kr = k * ck + pltpu.roll(k, ROT_HALF, 1) * s1 + pltpu.roll(k, LANES - ROT_HALF, 1) * s2
            out_ref[0, :, g * LANES:(g + 1) * LANES] = kr.astype(BF16)
    c0 = 2 * KPAD_WIDTH
    gate_ref[0] = _sigmoid(tok[:, c0 + 2 * KV_WIDTH:] + bg_ref[...])

    tiles = KV_WIDTH // LANES
    for t in range(2 * tiles):
        raw_ref[t] = tok[:, c0 + t * LANES:c0 + (t + 1) * LANES]
    nrow = tm // CMP_STRIDE
    low_half = lax.broadcasted_iota(jnp.int32, (nrow, LANES), 1) < HEAD_DIM
    for out_ref, base in ((kc_ref, 0), (vc_ref, tiles)):
        for h in range(tiles):
            for i in range(CMP_STRIDE // 2):
                a0 = raw_ref[base + h, pl.ds(2 * i, nrow, stride=CMP_STRIDE), :]
                a1 = raw_ref[base + h, pl.ds(2 * i + 1, nrow, stride=CMP_STRIDE), :]
                cols = slice(i * LANES, (i + 1) * LANES)
                out_ref[0, 2 * h, :, cols] = jnp.where(low_half, a0, pltpu.roll(a1, HEAD_DIM, 1))
                out_ref[0, 2 * h + 1, :, cols] = jnp.where(low_half, pltpu.roll(a0, HEAD_DIM, 1), a1)


def _nsa_project(x, g_pre, w_in, b_gate):
    b, s, d = x.shape
    tm = min(PROJ_ROWS, s)
    n_gate = NSA_HEADS * N_GATES
    w_q = w_in[:, :Q_WIDTH]
    w_kv = w_in[:, Q_WIDTH:Q_WIDTH + 6 * KV_WIDTH].reshape(d, 6, NSA_KV_GROUPS, HEAD_DIM)
    w_gate = w_in[:, Q_WIDTH + 6 * KV_WIDTH:]

    def padded(w):
        return jnp.pad(w, ((0, 0), (0, 0), (0, LANES - HEAD_DIM))).reshape(d, KPAD_WIDTH)

    w_tok = jnp.concatenate(
        [padded(w_kv[:, 2]), padded(w_kv[:, 4]), w_kv[:, 0].reshape(d, KV_WIDTH),
         w_kv[:, 1].reshape(d, KV_WIDTH), jnp.pad(w_gate, ((0, 0), (0, GATE_PAD - n_gate)))],
        axis=1).astype(BF16)
    w_tr = jnp.concatenate(
        [w_q, w_kv[:, 3].reshape(d, KV_WIDTH), w_kv[:, 5].reshape(d, KV_WIDTH)], axis=1).T.astype(BF16)
    bg = jnp.pad(b_gate, (0, GATE_PAD - n_gate)).reshape(1, GATE_PAD)

    cos, sin = _rope_tables(s)
    zeros = jnp.zeros_like(sin)
    pad_to = lambda parts, fill: jnp.concatenate(
        parts + [jnp.full((s, LANES - 2 * ROT_HALF), fill, F32)], axis=1)
    ck = pad_to([cos, cos], 1.0)
    s1 = pad_to([zeros, sin], 0.0)
    s2 = pad_to([-sin, zeros], 0.0)

    ntok, ntr = w_tok.shape[1], w_tr.shape[0]
    tok_major = lambda w: pl.BlockSpec((1, tm, w), lambda bi, j: (bi, j, 0))
    tr_major = lambda r: pl.BlockSpec((1, r, tm), lambda bi, j: (bi, 0, j))
    cmp_shape = (b, NSA_KV_GROUPS, s // CMP_STRIDE, CMP_STRIDE * HEAD_DIM)
    cmp_rows = pl.BlockSpec((1, NSA_KV_GROUPS, tm // CMP_STRIDE, CMP_STRIDE * HEAD_DIM),
                            lambda bi, j: (bi, 0, j, 0))
    return pl.pallas_call(
        _proj_body,
        grid=(b, s // tm),
        in_specs=[tok_major(d), _const_spec((1, d)), _const_spec((d, ntok)), _const_spec((ntr, d)),
                  _const_spec((1, GATE_PAD)),
                  pl.BlockSpec((ROT_HALF, tm), lambda bi, j: (0, j)),
                  pl.BlockSpec((ROT_HALF, tm), lambda bi, j: (0, j)),
                  pl.BlockSpec((tm, LANES), lambda bi, j: (j, 0)),
                  pl.BlockSpec((tm, LANES), lambda bi, j: (j, 0)),
                  pl.BlockSpec((tm, LANES), lambda bi, j: (j, 0))],
        out_specs=[tr_major(Q_WIDTH), tr_major(Q_WIDTH), tok_major(KPAD_WIDTH), tok_major(KPAD_WIDTH),
                   tr_major(KV_WIDTH), tr_major(KV_WIDTH), cmp_rows, cmp_rows,
                   tok_major(GATE_PAD)],
        out_shape=[jax.ShapeDtypeStruct((b, Q_WIDTH, s), BF16), jax.ShapeDtypeStruct((b, Q_WIDTH, s), BF16),
                   jax.ShapeDtypeStruct((b, s, KPAD_WIDTH), BF16), jax.ShapeDtypeStruct((b, s, KPAD_WIDTH), BF16),
                   jax.ShapeDtypeStruct((b, KV_WIDTH, s), BF16), jax.ShapeDtypeStruct((b, KV_WIDTH, s), BF16),
                   jax.ShapeDtypeStruct(cmp_shape, F32), jax.ShapeDtypeStruct(cmp_shape, F32),
                   jax.ShapeDtypeStruct((b, s, GATE_PAD), F32)],
        scratch_shapes=[pltpu.VMEM((2 * KV_WIDTH // LANES, tm, LANES), F32)],
        compiler_params=_params(("arbitrary", "arbitrary")),
        name="nsa_project",
    )(x, g_pre.reshape(1, d), w_tok, w_tr, bg, cos.T, sin.T, ck, s1, s2)


def _compress_body(rk_ref, rv_ref, pos_ref, w1_ref, w2k_ref, w2vT_ref, kc_ref, vcT_ref):
    nrow = rk_ref.shape[2]
    half = rk_ref.shape[3]

    def hidden(r, t):
        top = _dot((r + pos_ref[t, 0:1, :]).astype(BF16), w1_ref[t, 0:half, :])
        bot = _dot((r + pos_ref[t, 1:2, :]).astype(BF16), w1_ref[t, half:2 * half, :])
        hid = top + pltpu.roll(bot, nrow - 1, 0)
        return (hid * _sigmoid(hid)).astype(BF16)

    kc_ref[0, 0] = _dot(hidden(rk_ref[0, 0], 0), w2k_ref[...]).astype(BF16)
    vcT_ref[0, 0] = _dot_nt(w2vT_ref[...], hidden(rv_ref[0, 0], 1)).astype(BF16)


def _nsa_compress(kc_raw, vc_raw, cmp_pos, cmp_w1, cmp_w2):
    b, _, nrow, half = kc_raw.shape
    hid = cmp_w1.shape[-1]
    pos = cmp_pos.reshape(2, 2, half)
    w2k = jnp.pad(cmp_w2[0], ((0, 0), (0, LANES - HEAD_DIM))).astype(BF16)
    w2vT = cmp_w2[1].T.astype(BF16)
    blk = pl.BlockSpec((1, 1, nrow, half), lambda bi, g: (bi, g, 0, 0))
    return pl.pallas_call(
        _compress_body,
        grid=(b, NSA_KV_GROUPS),
        in_specs=[blk, blk, _const_spec((2, 2, half)), _const_spec((2, 2 * half, hid)),
                  _const_spec((hid, LANES)), _const_spec((HEAD_DIM, hid))],
        out_specs=[pl.BlockSpec((1, 1, nrow, LANES), lambda bi, g: (bi, g, 0, 0)),
                   pl.BlockSpec((1, 1, HEAD_DIM, nrow), lambda bi, g: (bi, g, 0, 0))],
        out_shape=[jax.ShapeDtypeStruct((b, NSA_KV_GROUPS, nrow, LANES), BF16),
                   jax.ShapeDtypeStruct((b, NSA_KV_GROUPS, HEAD_DIM, nrow), BF16)],
        compiler_params=_params(("arbitrary", "arbitrary")),
        name="nsa_compress",
    )(kc_raw, vc_raw, pos, cmp_w1.astype(BF16), w2k, w2vT)


SEL_CHUNK = 512
CHUNK_BLOCKS = SEL_CHUNK // SEL_BLOCK
WIN_KEYS = WINDOW + Q_BLOCK
QLANES = NSA_REP * Q_BLOCK
STEP_BLOCKS = 4
STEP_LANES = STEP_BLOCKS * QLANES
LANE_PARTS = tuple(slice(e * QLANES, (e + 1) * QLANES) for e in range(STEP_BLOCKS))
N_FORCED = 3
ONES_ROWS = 16
CMP_ROWS_STEP = 256
MASKED_MAX_FLOOR = -1e29


def _with_ones(vT):
    return jnp.concatenate([vT, jnp.ones((ONES_ROWS, vT.shape[1]), BF16)], axis=0)


def _attn_body(qT_ref, qrT_ref, kc_ref, vcT_ref, ksel_ref, vselT_ref, kwin_ref, vwinT_ref, gate_ref,
               ovT_ref, oh_ref, o_ref, qaug_ref, s0_ref, s1_ref, m_ref, acc_ref, part_ref, swin_ref,
               top0_ref, top1_ref):
    i = pl.program_id(2)
    t_step = i * (STEP_BLOCKS * Q_BLOCK)
    nblk = ovT_ref.shape[0]
    ncmp = kc_ref.shape[2]
    n_top = min(N_SEL, nblk) - N_FORCED

    def lanes_of_heads(ref, e):
        cols = slice(e * Q_BLOCK, (e + 1) * Q_BLOCK)
        parts = [ref[0, r * HEAD_DIM:(r + 1) * HEAD_DIM, cols] for r in range(NSA_REP)]
        return jnp.concatenate(
            [jnp.concatenate(parts, axis=1), jnp.zeros((LANES - HEAD_DIM, QLANES), BF16)], axis=0)

    def positions(e):
        return t_step + e * Q_BLOCK + lax.broadcasted_iota(jnp.int32, (1, Q_BLOCK), 1)

    pos_step = jnp.concatenate([positions(e) for e in range(STEP_BLOCKS) for _ in range(NSA_REP)], axis=1)
    d = HEAD_DIM

    def chunk_start(j):
        return pl.multiple_of(j * SEL_CHUNK, SEL_CHUNK)

    def scores_into(s_ref, top_ref, j, lanes=slice(None)):
        k0 = chunk_start(j)
        ka = jnp.concatenate([ksel_ref[0, pl.ds(k0, SEL_CHUNK), :], oh_ref[pl.ds(k0, SEL_CHUNK), :]], axis=1)
        s = _dot(ka, qaug_ref[:, lanes])
        s_ref[:, lanes] = s
        top_ref[:, lanes] = jnp.max(s, axis=0, keepdims=True)

    def block_select_and_window(e, cmp_rows, steady):
        t0 = t_step + e * Q_BLOCK
        lanes = slice(e * QLANES, (e + 1) * QLANES)
        qT = lanes_of_heads(qT_ref, e)
        qrT = lanes_of_heads(qrT_ref, e)
        pos1 = positions(e)
        pos = jnp.concatenate([pos1] * NSA_REP, axis=1)
        qaug_ref[0:LANES, lanes] = qrT
        s = _dot(kc_ref[0, 0, 0:cmp_rows, :], qT)
        w0 = pl.multiple_of(t0 - WINDOW if steady else jnp.maximum(t0 - WINDOW, 0), Q_BLOCK)
        swin_ref[e] = _dot(kwin_ref[0, pl.ds(w0, WIN_KEYS), :], qrT)
        s0_ref[:, lanes] = _dot(ksel_ref[0, 0:SEL_CHUNK, :], qrT)
        cend = lax.broadcasted_iota(jnp.int32, (cmp_rows, QLANES), 0) * CMP_STRIDE + (CMP_LEN - 1)
        s = jnp.where(cend <= pos, s, NEG_INF)
        m = jnp.maximum(jnp.max(s, axis=0, keepdims=True), MASKED_MAX_FLOOR)
        p = jnp.exp2(s - m)
        inv_l = 1.0 / jnp.maximum(jnp.sum(p, axis=0, keepdims=True), 1e-30)
        o_cmp = _dot(vcT_ref[0, 0, :, 0:cmp_rows], p.astype(BF16)) * inv_l

        pn = p * inv_l
        ph = pn[:, 0:Q_BLOCK]
        for r in range(1, NSA_REP):
            ph = ph + pn[:, r * Q_BLOCK:(r + 1) * Q_BLOCK]
        hi = ph.astype(BF16)
        lo = (ph - hi.astype(F32)).astype(BF16)
        ovT = ovT_ref[:, 0:cmp_rows]
        imp = _dot(ovT, hi) + _dot(ovT, lo)

        blk = lax.broadcasted_iota(jnp.int32, (nblk, Q_BLOCK), 0)
        cur = pos1 // SEL_BLOCK
        forced = (blk == 0) | (blk == cur) | (blk == cur - 1)
        v0 = jnp.where(forced, -1.0, jnp.where(blk * SEL_BLOCK <= pos1, imp, -1.0))
        n_piece = WIN_KEYS // Q_BLOCK
        kk = lax.broadcasted_iota(jnp.int32, (Q_BLOCK, QLANES), 0)
        qq = lax.broadcasted_iota(jnp.int32, (Q_BLOCK, QLANES), 1) % Q_BLOCK
        win = {"m": None, "acc": None}

        def piece_max(w):
            rows = slice(w * Q_BLOCK, (w + 1) * Q_BLOCK)
            sp = swin_ref[e, rows, :]
            if not steady:
                kpos = w0 + w * Q_BLOCK + kk
                sp = jnp.where(kpos <= pos, jnp.where(kpos > pos - WINDOW, sp, NEG_INF), NEG_INF)
            elif w == 0:
                sp = jnp.where(kk > qq, sp, NEG_INF)
            elif w == n_piece - 1:
                sp = jnp.where(kk <= qq, sp, NEG_INF)
            if not steady or w in (0, n_piece - 1):
                swin_ref[e, rows, :] = sp
            top = jnp.max(sp, axis=0, keepdims=True)
            win["m"] = top if win["m"] is None else jnp.maximum(win["m"], top)

        def piece_absorb(w):
            rows = slice(w * Q_BLOCK, (w + 1) * Q_BLOCK)
            p = jnp.exp2(swin_ref[e, rows, :] - win["m"]).astype(BF16)
            k0 = pl.multiple_of(w0 + w * Q_BLOCK, Q_BLOCK)
            pv = _dot(_with_ones(vwinT_ref[0, :, pl.ds(k0, Q_BLOCK)]), p)
            win["acc"] = pv if win["acc"] is None else win["acc"] + pv

        for w in range(n_piece):
            piece_max(w)
        left = v0
        pending = list(range(n_piece))
        for r in range(n_top):
            left = jnp.where(left == jnp.max(left, axis=0, keepdims=True), -2.0, left)
            if pending and r % 2 == 0:
                piece_absorb(pending.pop(0))
        for w in pending:
            piece_absorb(w)
        taken = left != v0
        n_taken = jnp.sum(jnp.where(taken, jnp.where(v0 >= 0.0, 1.0, 0.0), 0.0), axis=0, keepdims=True)

        def store_bias(left):
            bias = jnp.where(forced, 0.0, jnp.where(left != v0, 0.0, NEG_INF)).astype(BF16)
            bias = jnp.concatenate([bias] * NSA_REP, axis=1)
            if nblk < LANES:
                bias = jnp.concatenate([bias, jnp.zeros((LANES - nblk, QLANES), BF16)], axis=0)
            qaug_ref[LANES:2 * LANES, lanes] = bias

        store_bias(left)
        bias0 = jnp.where(forced[0:CHUNK_BLOCKS], 0.0,
                          jnp.where(left[0:CHUNK_BLOCKS] != v0[0:CHUNK_BLOCKS], 0.0, NEG_INF))
        bias0 = jnp.concatenate([bias0] * NSA_REP, axis=1)
        top = None
        for n in range(CHUNK_BLOCKS):
            rows = slice(n * SEL_BLOCK, (n + 1) * SEL_BLOCK)
            sb = s0_ref[rows, lanes] + bias0[n:n + 1, :]
            s0_ref[rows, lanes] = sb
            bt = jnp.max(sb, axis=0, keepdims=True)
            top = bt if top is None else jnp.maximum(top, bt)
        top0_ref[:, lanes] = top

        a_w = win["acc"]
        gate = gate_ref[0, 0, e]
        part_ref[:, lanes] = gate[0:1] * o_cmp + gate[2:3] * (a_w[0:d] / a_w[d:d + 1])

        def select_exactly():
            def drop_first_max(_, val):
                top = jnp.max(val, axis=0, keepdims=True)
                first = jnp.min(jnp.where(val == top, blk, nblk), axis=0, keepdims=True)
                return jnp.where(blk == first, -2.0, val)

            store_bias(lax.fori_loop(0, n_top, drop_first_max, v0))

        return jnp.max(n_taken) > n_top, select_exactly

    def select_and_window(cmp_rows, steady):
        redo = [block_select_and_window(e, cmp_rows, steady) for e in range(STEP_BLOCKS)]
        any_overflowed = False
        for overflowed, select_exactly in redo:
            pl.when(overflowed)(select_exactly)
            any_overflowed = jnp.logical_or(any_overflowed, overflowed)
        pl.when(any_overflowed)(functools.partial(scores_into, s0_ref, top0_ref, 0))

    step_tokens = STEP_BLOCKS * Q_BLOCK

    def visible_cmp(step):
        return (step * step_tokens + step_tokens - CMP_LEN) // CMP_STRIDE + 1

    n_steps = ksel_ref.shape[1] // step_tokens
    steady_from = -(-WINDOW // step_tokens)
    first_rows = min(CMP_ROWS_STEP, ncmp)
    assert visible_cmp(steady_from - 1) <= first_rows
    pl.when(i < steady_from)(functools.partial(select_and_window, first_rows, False))
    lo = steady_from
    for rows in range(first_rows, ncmp + 1, CMP_ROWS_STEP):
        hi = next((step for step in range(lo, n_steps) if visible_cmp(step) > rows), n_steps)
        if hi > lo:
            pl.when((i >= lo) & (i < hi))(functools.partial(select_and_window, rows, True))
            lo = hi
    assert lo == n_steps

    def absorb(s_ref, top_ref, j, diagonal, lanes=slice(None)):
        k0 = chunk_start(j)
        s = s_ref[:, lanes]
        if diagonal:
            kpos = k0 + lax.broadcasted_iota(jnp.int32, s.shape, 0)
            s = jnp.where(kpos <= pos_step[:, lanes], s, NEG_INF)
            top = jnp.max(s, axis=0, keepdims=True)
        else:
            top = top_ref[:, lanes]
        m = m_ref[:, lanes]
        m_new = jnp.maximum(m, top)
        p = jnp.exp2(s - m_new).astype(BF16)
        acc_ref[:, lanes] = jnp.exp2(m - m_new) * acc_ref[:, lanes] + _dot(
            _with_ones(vselT_ref[0, :, pl.ds(k0, SEL_CHUNK)]), p)
        m_ref[:, lanes] = m_new

    def absorb_own_positions(s_ref, j):
        k0 = chunk_start(j)
        kk = lax.broadcasted_iota(jnp.int32, (Q_BLOCK, QLANES), 0)
        qq = lax.broadcasted_iota(jnp.int32, (Q_BLOCK, QLANES), 1) % Q_BLOCK
        for e in range(STEP_BLOCKS):
            lanes = slice(e * QLANES, (e + 1) * QLANES)
            rows = (e + 1) * Q_BLOCK
            s = jnp.where(kk <= qq, s_ref[e * Q_BLOCK:rows, lanes], NEG_INF)
            if e > 0:
                s = jnp.concatenate([s_ref[0:e * Q_BLOCK, lanes], s], axis=0)
            m = m_ref[:, lanes]
            m_new = jnp.maximum(m, jnp.max(s, axis=0, keepdims=True))
            p = jnp.exp2(s - m_new).astype(BF16)
            acc_ref[:, lanes] = jnp.exp2(m - m_new) * acc_ref[:, lanes] + _dot(
                _with_ones(vselT_ref[0, :, pl.ds(k0, rows)]), p)
            m_ref[:, lanes] = m_new

    own_chunk = STEP_BLOCKS * Q_BLOCK == SEL_CHUNK

    def absorb_last(s_ref, top_ref, j):
        if own_chunk:
            absorb_own_positions(s_ref, j)
        else:
            absorb(s_ref, top_ref, j, True)

    def scores_last_into(s_ref, top_ref, j):
        if not own_chunk:
            scores_into(s_ref, top_ref, j)
            return
        k0 = chunk_start(j)
        for e in range(STEP_BLOCKS):
            lanes = slice(e * QLANES, (e + 1) * QLANES)
            rows = (e + 1) * Q_BLOCK
            ka = jnp.concatenate([ksel_ref[0, pl.ds(k0, rows), :], oh_ref[pl.ds(k0, rows), :]], axis=1)
            s_ref[0:rows, lanes] = _dot(ka, qaug_ref[:, lanes])

    last = t_step // SEL_CHUNK
    m_ref[...] = jnp.full(m_ref.shape, NEG_INF, F32)
    acc_ref[...] = jnp.zeros(acc_ref.shape, F32)

    def pair(t):
        for part in LANE_PARTS:
            scores_into(s1_ref, top1_ref, 2 * t + 1, part)
            absorb(s0_ref, top0_ref, 2 * t, False, part)
        for part in LANE_PARTS:
            scores_into(s0_ref, top0_ref, 2 * t + 2, part)
            absorb(s1_ref, top1_ref, 2 * t + 1, False, part)

    def pair_step(t, carry):
        pair(t)
        return carry

    lax.fori_loop(0, last // 2, pair_step, 0)

    @pl.when(last % 2 == 1)
    def _():
        scores_last_into(s1_ref, top1_ref, last)
        absorb(s0_ref, top0_ref, last - 1, False)
        absorb_last(s1_ref, top1_ref, last)

    @pl.when(last % 2 == 0)
    def _():
        absorb_last(s0_ref, top0_ref, last)

    a_s = acc_ref[...]
    gate_sel = jnp.concatenate([gate_ref[0, 0, e][1:2] for e in range(STEP_BLOCKS)], axis=1)
    out = part_ref[...] + gate_sel * (a_s[0:d] / a_s[d:d + 1])
    for e in range(STEP_BLOCKS):
        for r in range(NSA_REP):
            lane0 = e * QLANES + r * Q_BLOCK
            o_ref[0, r * HEAD_DIM:(r + 1) * HEAD_DIM, e * Q_BLOCK:(e + 1) * Q_BLOCK] = (
                out[:, lane0:lane0 + Q_BLOCK].astype(BF16))


def _nsa_attention(qT, qrT, kcmp, vcmpT, ksel, vselT, kwin, vwinT, gates):
    b, _, s = qT.shape
    nq = s // Q_BLOCK
    nblk = s // SEL_BLOCK
    ncmp = kcmp.shape[2]
    rows = NSA_REP * HEAD_DIM
    assert nblk <= LANES and s % SEL_CHUNK == 0 and s >= WIN_KEYS

    gt = gates[:, :, :NSA_HEADS * N_GATES].reshape(b, nq, Q_BLOCK, NSA_KV_GROUPS, NSA_REP, N_GATES)
    gt = gt.transpose(0, 3, 1, 5, 4, 2).reshape(b, NSA_KV_GROUPS, nq, N_GATES, QLANES)

    cstart = np.arange(ncmp) * CMP_STRIDE
    sstart = np.arange(nblk) * SEL_BLOCK
    ovT = ((cstart[None, :] < sstart[:, None] + SEL_BLOCK) & (cstart[None, :] + CMP_LEN > sstart[:, None]))
    ovT = jnp.asarray(ovT, BF16)
    onehot = jnp.asarray(np.arange(s)[:, None] // SEL_BLOCK == np.arange(LANES)[None, :], BF16)

    assert nq % STEP_BLOCKS == 0
    q_spec = pl.BlockSpec((1, rows, STEP_BLOCKS * Q_BLOCK), lambda bi, g, i: (bi, g, i))
    k_spec = pl.BlockSpec((1, s, LANES), lambda bi, g, i: (bi, 0, g))
    vT_spec = pl.BlockSpec((1, HEAD_DIM, s), lambda bi, g, i: (bi, g, 0))
    return pl.pallas_call(
        _attn_body,
        grid=(b, NSA_KV_GROUPS, nq // STEP_BLOCKS),
        in_specs=[q_spec, q_spec,
                  pl.BlockSpec((1, 1, ncmp, LANES), lambda bi, g, i: (bi, g, 0, 0)),
                  pl.BlockSpec((1, 1, HEAD_DIM, ncmp), lambda bi, g, i: (bi, g, 0, 0)),
                  k_spec, vT_spec, k_spec, vT_spec,
                  pl.BlockSpec((1, 1, STEP_BLOCKS, N_GATES, QLANES), lambda bi, g, i: (bi, g, i, 0, 0)),
                  _const_spec((nblk, ncmp)), _const_spec((s, LANES))],
        out_specs=q_spec,
        out_shape=jax.ShapeDtypeStruct((b, NSA_HEADS * HEAD_DIM, s), BF16),
        scratch_shapes=[pltpu.VMEM((2 * LANES, STEP_LANES), BF16),
                        pltpu.VMEM((SEL_CHUNK, STEP_LANES), F32), pltpu.VMEM((SEL_CHUNK, STEP_LANES), F32),
                        pltpu.VMEM((1, STEP_LANES), F32), pltpu.VMEM((HEAD_DIM + ONES_ROWS, STEP_LANES), F32),
                        pltpu.VMEM((HEAD_DIM, STEP_LANES), F32), pltpu.VMEM((STEP_BLOCKS, WIN_KEYS, QLANES), F32),
                        pltpu.VMEM((1, STEP_LANES), F32), pltpu.VMEM((1, STEP_LANES), F32)],
        compiler_params=_params(("arbitrary", "arbitrary", "arbitrary")),
        name="nsa_attention",
    )(qT, qrT, kcmp, vcmpT, ksel, vselT, kwin, vwinT, gt, ovT, onehot)


def _nsa_mixer(x, g_pre, w_in, b_gate, cmp_pos, cmp_w1, cmp_w2):
    qT, qrT, ksel, kwin, vselT, vwinT, kc_raw, vc_raw, gates = _nsa_project(x, g_pre, w_in, b_gate)
    kcmp, vcmpT = _nsa_compress(kc_raw, vc_raw, cmp_pos, cmp_w1, cmp_w2)
    return _nsa_attention(qT, qrT, kcmp, vcmpT, ksel, vselT, kwin, vwinT, gates)


def kernel(x, mix_norm_pre, mix_norm_post, ffn_norm_pre, ffn_norm_post, ffn_w_gate, ffn_w_up, ffn_w_down,
           conv_w_pw1, conv_b_pw1, conv_w_dw, conv_b_dw, conv_ln_g, conv_ln_b, conv_w_pw2, conv_b_pw2,
           nsa_w_in, nsa_b_gate, nsa_cmp_pos, nsa_cmp_w1, nsa_cmp_w2, nsa_w_out):
    b, s, d = x.shape
    depth = mix_norm_pre.shape[0]
    n_mixers = 2

    w_gate, w_up, w_down = ffn_w_gate, ffn_w_up, ffn_w_down

    def ffn(x, i, half, attn_out=None):
        y = _ffn_half(x.reshape(b * s, d), ffn_norm_pre[i, half], ffn_norm_post[i, half],
                      w_gate, w_up, w_down, (i, half), attn_out)
        return y.reshape(b, s, d)

    for i in range(depth):
        x = ffn(x, i, 0)
        j = i // n_mixers
        if i % n_mixers == 0:
            x = _conv_mixer(x, mix_norm_pre[i], mix_norm_post[i], conv_w_pw1[j], conv_b_pw1[j], conv_w_dw[j],
                            conv_b_dw[j], conv_ln_g[j], conv_ln_b[j], conv_w_pw2[j], conv_b_pw2[j])
            x = ffn(x, i, 1)
        else:
            attnT = _nsa_mixer(x, mix_norm_pre[i], nsa_w_in[j], nsa_b_gate[j], nsa_cmp_pos[j],
                               nsa_cmp_w1[j], nsa_cmp_w2[j])
            x = ffn(x, i, 1, (attnT, nsa_w_out[j], mix_norm_post[i]))
    return x
```

```python
import functools
import math

import jax
import jax.numpy as jnp
import numpy as np
from jax import lax
from jax.experimental import pallas as pl
from jax.experimental.pallas import tpu as pltpu

RMS_EPS = 1e-6
LN_EPS = 1e-5
FFN_RESIDUAL_WEIGHT = 0.5
CONV_WIDTH = 31
NSA_HEADS = 16
NSA_KV_GROUPS = 4
NSA_REP = NSA_HEADS // NSA_KV_GROUPS
HEAD_DIM = 64
ROT_HALF = HEAD_DIM // 8
ROPE_THETA = 500000.0
CMP_LEN = 32
CMP_STRIDE = 16
SEL_BLOCK = 64
N_SEL = 16
WINDOW = 512
Q_BLOCK = 128
N_GATES = 3
NEG_INF = -1e30

LANES = 128
SUBLANES = 8
V7X_VMEM_LIMIT = 56 * 1024 * 1024

BF16 = jnp.bfloat16
F32 = jnp.float32


def _params(sem):
    return pltpu.CompilerParams(dimension_semantics=sem, vmem_limit_bytes=V7X_VMEM_LIMIT)


def _const_spec(shape):
    nd = len(shape)
    return pl.BlockSpec(shape, lambda *_: (0,) * nd, pipeline_mode=pl.Buffered(1))


def _rms(x, gain):
    return x * lax.rsqrt(jnp.mean(x * x, axis=-1, keepdims=True) + RMS_EPS) * gain


def _sigmoid(x):
    return 1.0 / (1.0 + jnp.exp(-x))


def _dot(a, b):
    return jnp.dot(a, b, preferred_element_type=F32)


def _dot_nt(a, b):
    return lax.dot_general(a, b, (((1,), (1,)), ((), ())), preferred_element_type=F32)


def _dot_tn(a, b):
    return lax.dot_general(a, b, (((0,), (0,)), ((), ())), preferred_element_type=F32)


FFN_ROWS = 512
FFN_CHUNK = 512


def _swiglu_half_step(x, gpre_ref, gpost_ref, wg_ref, wu_ref, wd_ref, o_ref):
    h = _rms(x, gpre_ref[...]).astype(BF16)
    d_ff = wd_ref.shape[0]
    acc = jnp.zeros(x.shape, F32)
    for c0 in range(0, d_ff, FFN_CHUNK):
        c1 = min(c0 + FFN_CHUNK, d_ff)
        g = _dot(h, wg_ref[:, c0:c1].astype(BF16))
        u = _dot(h, wu_ref[:, c0:c1].astype(BF16))
        a = (g * _sigmoid(g) * u).astype(BF16)
        acc = acc + _dot(a, wd_ref[c0:c1, :].astype(BF16))
    o_ref[...] = x + FFN_RESIDUAL_WEIGHT * _rms(acc, gpost_ref[...])


def _ffn_body(x_ref, *refs):
    _swiglu_half_step(x_ref[...], *refs)


def _attn_out_ffn_body(x_ref, aT_ref, wo_ref, gmix_ref, *refs):
    x = x_ref[...] + _rms(_dot_tn(aT_ref[0], wo_ref[...]), gmix_ref[...])
    _swiglu_half_step(x, *refs)


def _ffn_half(x2, g_pre, g_post, w_gate, w_up, w_down, layer, attn_out=None):
    t, d = x2.shape
    f = w_gate.shape[-1]
    tm = min(FFN_ROWS, t)
    row = pl.BlockSpec((tm, d), lambda i: (i, 0))

    def stacked(rows, cols):
        return pl.BlockSpec((None, None, rows, cols), lambda *_: (*layer, 0, 0), pipeline_mode=pl.Buffered(1))

    ffn_specs = [_const_spec((1, d)), _const_spec((1, d)), stacked(d, f), stacked(d, f), stacked(f, d)]
    ffn_args = (g_pre.reshape(1, d), g_post.reshape(1, d), w_gate, w_up, w_down)
    if attn_out is None:
        body, specs, args = _ffn_body, [row], (x2,)
    else:
        attnT, w_out, g_mix = attn_out
        tiles = attnT.shape[2] // tm
        body = _attn_out_ffn_body
        specs = [row, pl.BlockSpec((1, attnT.shape[1], tm), lambda i: (i // tiles, 0, i % tiles)),
                 _const_spec(w_out.shape), _const_spec((1, d))]
        args = (x2, attnT, w_out.astype(BF16), g_mix.reshape(1, d))
    return pl.pallas_call(
        body,
        grid=(t // tm,),
        in_specs=specs + ffn_specs,
        out_specs=row,
        out_shape=jax.ShapeDtypeStruct((t, d), F32),
        compiler_params=_params(("arbitrary",)),
        name="ffn_half",
    )(*args, *ffn_args)


CONV_ROWS = 512
CONV_HALO = 32
CONV_STRIP = 32


def _conv_body(x_ref, gpre_ref, w1_ref, b1_ref, wdw_ref, bdw_ref, lng_ref, lnb_ref,
               w2_ref, b2_ref, gpost_ref, o_ref, buf_ref, dw_ref, shift_ref):
    ts, d = x_ref.shape[1], x_ref.shape[2]

    @pl.when(pl.program_id(1) == 0)
    def _():
        buf_ref[0:CONV_HALO, :] = jnp.zeros((CONV_HALO, d), F32)

    x = x_ref[0]
    h = _rms(x, gpre_ref[...]).astype(BF16)
    p = _dot(h, w1_ref[...]) + b1_ref[...]
    buf_ref[CONV_HALO:CONV_HALO + ts, :] = p[:, :d] * _sigmoid(p[:, d:])

    base = CONV_HALO - (CONV_WIDTH - 1)
    span = shift_ref.shape[1]
    for b in range(1, SUBLANES):
        shift_ref[b - 1] = buf_ref[b:b + span, :]

    for r0 in range(0, ts, CONV_STRIP):
        acc = jnp.zeros((CONV_STRIP, d), F32)
        for k in range(CONV_WIDTH):
            a, b = divmod(base + k, SUBLANES)
            lo = r0 + a * SUBLANES
            win = buf_ref[lo:lo + CONV_STRIP, :] if b == 0 else shift_ref[b - 1, lo:lo + CONV_STRIP, :]
            acc = acc + wdw_ref[k:k + 1, :] * win
        dw_ref[r0:r0 + CONV_STRIP, :] = acc
    buf_ref[0:CONV_HALO, :] = buf_ref[ts:ts + CONV_HALO, :]

    c = dw_ref[...] + bdw_ref[...]
    mu = jnp.mean(c, axis=-1, keepdims=True)
    cc = c - mu
    var = jnp.mean(cc * cc, axis=-1, keepdims=True)
    y = cc * lax.rsqrt(var + LN_EPS) * lng_ref[...] + lnb_ref[...]
    y = (y * _sigmoid(y)).astype(BF16)
    out = _dot(y, w2_ref[...]) + b2_ref[...]
    o_ref[0] = x + _rms(out, gpost_ref[...])


def _conv_mixer(x, g_pre, g_post, w_pw1, b_pw1, w_dw, b_dw, ln_g, ln_b, w_pw2, b_pw2):
    b, s, d = x.shape
    ts = min(CONV_ROWS, s)
    tile = pl.BlockSpec((1, ts, d), lambda bi, j: (bi, j, 0))
    vec = lambda n: _const_spec((1, n))
    return pl.pallas_call(
        _conv_body,
        grid=(b, s // ts),
        in_specs=[tile, vec(d), _const_spec((d, 2 * d)), vec(2 * d), _const_spec((CONV_WIDTH, d)),
                  vec(d), vec(d), vec(d), _const_spec((d, d)), vec(d), vec(d)],
        out_specs=tile,
        out_shape=jax.ShapeDtypeStruct((b, s, d), F32),
        scratch_shapes=[pltpu.VMEM((ts + CONV_HALO, d), F32), pltpu.VMEM((ts, d), F32),
                        pltpu.VMEM((SUBLANES - 1, ts + (CONV_HALO - 1) // SUBLANES * SUBLANES, d), F32)],
        compiler_params=_params(("arbitrary", "arbitrary")),
        name="conv_mixer",
    )(x, g_pre.reshape(1, d), w_pw1.astype(BF16), b_pw1.reshape(1, 2 * d), w_dw,
      b_dw.reshape(1, d), ln_g.reshape(1, d), ln_b.reshape(1, d), w_pw2.astype(BF16),
      b_pw2.reshape(1, d), g_post.reshape(1, d))


PROJ_ROWS = 1024
Q_WIDTH = NSA_HEADS * HEAD_DIM
KV_WIDTH = NSA_KV_GROUPS * HEAD_DIM
KPAD_WIDTH = NSA_KV_GROUPS * LANES
GATE_PAD = LANES
Q_SCALE = HEAD_DIM ** -0.5 * math.log2(math.e)


def _rope_tables(s):
    pos = jnp.arange(s, dtype=F32)
    inv_freq = ROPE_THETA ** (-jnp.arange(0, 2 * ROT_HALF, 2, dtype=F32) / (2 * ROT_HALF))
    ang = pos[:, None] * inv_freq[None, :]
    return jnp.cos(ang), jnp.sin(ang)


def _proj_body(x_ref, gpre_ref, wtok_ref, wtr_ref, bg_ref, cosT_ref, sinT_ref, ck_ref, s1_ref, s2_ref,
               qT_ref, qrT_ref, ksel_ref, kwin_ref, vselT_ref, vwinT_ref, kc_ref, vc_ref, gate_ref, raw_ref):
    tm = x_ref.shape[1]
    h = _rms(x_ref[0], gpre_ref[...]).astype(BF16)
    tok = _dot(h, wtok_ref[...])
    tr = _dot_nt(wtr_ref[...], h)

    q = tr[0:Q_WIDTH] * Q_SCALE
    qT_ref[0] = q.astype(BF16)
    q3 = q.reshape(NSA_HEADS, HEAD_DIM, tm)
    cos, sin = cosT_ref[...], sinT_ref[...]
    x1, x2 = q3[:, 0:ROT_HALF], q3[:, ROT_HALF:2 * ROT_HALF]
    qr = jnp.concatenate([x1 * cos - x2 * sin, x2 * cos + x1 * sin, q3[:, 2 * ROT_HALF:]], axis=1)
    qrT_ref[0] = qr.reshape(Q_WIDTH, tm).astype(BF16)
    vselT_ref[0] = tr[Q_WIDTH:Q_WIDTH + KV_WIDTH].astype(BF16)
    vwinT_ref[0] = tr[Q_WIDTH + KV_WIDTH:Q_WIDTH + 2 * KV_WIDTH].astype(BF16)

    ck, s1, s2 = ck_ref[...], s1_ref[...], s2_ref[...]
    for out_ref, base in ((ksel_ref, 0), (kwin_ref, KPAD_WIDTH)):
        for g in range(NSA_KV_GROUPS):
            k = tok[:, base + g * LANES:base + (g + 1) * LANES]
            kr = k * ck + pltpu.roll(k, ROT_HALF, 1) * s1 + pltpu.roll(k, LANES - ROT_HALF, 1) * s2
            out_ref[0, :, g * LANES:(g + 1) * LANES] = kr.astype(BF16)
    c0 = 2 * KPAD_WIDTH
    gate_ref[0] = _sigmoid(tok[:, c0 + 2 * KV_WIDTH:] + bg_ref[...])

    tiles = KV_WIDTH // LANES
    for t in range(2 * tiles):
        raw_ref[t] = tok[:, c0 + t * LANES:c0 + (t + 1) * LANES]
    nrow = tm // CMP_STRIDE
    low_half = lax.broadcasted_iota(jnp.int32, (nrow, LANES), 1) < HEAD_DIM
    for out_ref, base in ((kc_ref, 0), (vc_ref, tiles)):
        for h in range(tiles):
            for i in range(CMP_STRIDE // 2):
                a0 = raw_ref[base + h, pl.ds(2 * i, nrow, stride=CMP_STRIDE), :]
                a1 = raw_ref[base + h, pl.ds(2 * i + 1, nrow, stride=CMP_STRIDE), :]
                cols = slice(i * LANES, (i + 1) * LANES)
                out_ref[0, 2 * h, :, cols] = jnp.where(low_half, a0, pltpu.roll(a1, HEAD_DIM, 1))
                out_ref[0, 2 * h + 1, :, cols] = jnp.where(low_half, pltpu.roll(a0, HEAD_DIM, 1), a1)


def _nsa_project(x, g_pre, w_in, b_gate):
    b, s, d = x.shape
    tm = min(PROJ_ROWS, s)
    n_gate = NSA_HEADS * N_GATES
    w_q = w_in[:, :Q_WIDTH]
    w_kv = w_in[:, Q_WIDTH:Q_WIDTH + 6 * KV_WIDTH].reshape(d, 6, NSA_KV_GROUPS, HEAD_DIM)
    w_gate = w_in[:, Q_WIDTH + 6 * KV_WIDTH:]

    def padded(w):
        return jnp.pad(w, ((0, 0), (0, 0), (0, LANES - HEAD_DIM))).reshape(d, KPAD_WIDTH)

    w_tok = jnp.concatenate(
        [padded(w_kv[:, 2]), padded(w_kv[:, 4]), w_kv[:, 0].reshape(d, KV_WIDTH),
         w_kv[:, 1].reshape(d, KV_WIDTH), jnp.pad(w_gate, ((0, 0), (0, GATE_PAD - n_gate)))],
        axis=1).astype(BF16)
    w_tr = jnp.concatenate(
        [w_q, w_kv[:, 3].reshape(d, KV_WIDTH), w_kv[:, 5].reshape(d, KV_WIDTH)], axis=1).T.astype(BF16)
    bg = jnp.pad(b_gate, (0, GATE_PAD - n_gate)).reshape(1, GATE_PAD)

    cos, sin = _rope_tables(s)
    zeros = jnp.zeros_like(sin)
    pad_to = lambda parts, fill: jnp.concatenate(
        parts + [jnp.full((s, LANES - 2 * ROT_HALF), fill, F32)], axis=1)
    ck = pad_to([cos, cos], 1.0)
    s1 = pad_to([zeros, sin], 0.0)
    s2 = pad_to([-sin, zeros], 0.0)

    ntok, ntr = w_tok.shape[1], w_tr.shape[0]
    tok_major = lambda w: pl.BlockSpec((1, tm, w), lambda bi, j: (bi, j, 0))
    tr_major = lambda r: pl.BlockSpec((1, r, tm), lambda bi, j: (bi, 0, j))
    cmp_shape = (b, NSA_KV_GROUPS, s // CMP_STRIDE, CMP_STRIDE * HEAD_DIM)
    cmp_rows = pl.BlockSpec((1, NSA_KV_GROUPS, tm // CMP_STRIDE, CMP_STRIDE * HEAD_DIM),
                            lambda bi, j: (bi, 0, j, 0))
    return pl.pallas_call(
        _proj_body,
        grid=(b, s // tm),
        in_specs=[tok_major(d), _const_spec((1, d)), _const_spec((d, ntok)), _const_spec((ntr, d)),
                  _const_spec((1, GATE_PAD)),
                  pl.BlockSpec((ROT_HALF, tm), lambda bi, j: (0, j)),
                  pl.BlockSpec((ROT_HALF, tm), lambda bi, j: (0, j)),
                  pl.BlockSpec((tm, LANES), lambda bi, j: (j, 0)),
                  pl.BlockSpec((tm, LANES), lambda bi, j: (j, 0)),
                  pl.BlockSpec((tm, LANES), lambda bi, j: (j, 0))],
        out_specs=[tr_major(Q_WIDTH), tr_major(Q_WIDTH), tok_major(KPAD_WIDTH), tok_major(KPAD_WIDTH),
                   tr_major(KV_WIDTH), tr_major(KV_WIDTH), cmp_rows, cmp_rows,
                   tok_major(GATE_PAD)],
        out_shape=[jax.ShapeDtypeStruct((b, Q_WIDTH, s), BF16), jax.ShapeDtypeStruct((b, Q_WIDTH, s), BF16),
                   jax.ShapeDtypeStruct((b, s, KPAD_WIDTH), BF16), jax.ShapeDtypeStruct((b, s, KPAD_WIDTH), BF16),
                   jax.ShapeDtypeStruct((b, KV_WIDTH, s), BF16), jax.ShapeDtypeStruct((b, KV_WIDTH, s), BF16),
                   jax.ShapeDtypeStruct(cmp_shape, F32), jax.ShapeDtypeStruct(cmp_shape, F32),
                   jax.ShapeDtypeStruct((b, s, GATE_PAD), F32)],
        scratch_shapes=[pltpu.VMEM((2 * KV_WIDTH // LANES, tm, LANES), F32)],
        compiler_params=_params(("arbitrary", "arbitrary")),
        name="nsa_project",
    )(x, g_pre.reshape(1, d), w_tok, w_tr, bg, cos.T, sin.T, ck, s1, s2)


def _compress_body(rk_ref, rv_ref, pos_ref, w1_ref, w2k_ref, w2vT_ref, kc_ref, vcT_ref):
    nrow = rk_ref.shape[2]
    half = rk_ref.shape[3]

    def hidden(r, t):
        top = _dot((r + pos_ref[t, 0:1, :]).astype(BF16), w1_ref[t, 0:half, :])
        bot = _dot((r + pos_ref[t, 1:2, :]).astype(BF16), w1_ref[t, half:2 * half, :])
        hid = top + pltpu.roll(bot, nrow - 1, 0)
        return (hid * _sigmoid(hid)).astype(BF16)

    kc_ref[0, 0] = _dot(hidden(rk_ref[0, 0], 0), w2k_ref[...]).astype(BF16)
    vcT_ref[0, 0] = _dot_nt(w2vT_ref[...], hidden(rv_ref[0, 0], 1)).astype(BF16)


def _nsa_compress(kc_raw, vc_raw, cmp_pos, cmp_w1, cmp_w2):
    b, _, nrow, half = kc_raw.shape
    hid = cmp_w1.shape[-1]
    pos = cmp_pos.reshape(2, 2, half)
    w2k = jnp.pad(cmp_w2[0], ((0, 0), (0, LANES - HEAD_DIM))).astype(BF16)
    w2vT = cmp_w2[1].T.astype(BF16)
    blk = pl.BlockSpec((1, 1, nrow, half), lambda bi, g: (bi, g, 0, 0))
    return pl.pallas_call(
        _compress_body,
        grid=(b, NSA_KV_GROUPS),
        in_specs=[blk, blk, _const_spec((2, 2, half)), _const_spec((2, 2 * half, hid)),
                  _const_spec((hid, LANES)), _const_spec((HEAD_DIM, hid))],
        out_specs=[pl.BlockSpec((1, 1, nrow, LANES), lambda bi, g: (bi, g, 0, 0)),
                   pl.BlockSpec((1, 1, HEAD_DIM, nrow), lambda bi, g: (bi, g, 0, 0))],
        out_shape=[jax.ShapeDtypeStruct((b, NSA_KV_GROUPS, nrow, LANES), BF16),
                   jax.ShapeDtypeStruct((b, NSA_KV_GROUPS, HEAD_DIM, nrow), BF16)],
        compiler_params=_params(("arbitrary", "arbitrary")),
        name="nsa_compress",
    )(kc_raw, vc_raw, pos, cmp_w1.astype(BF16), w2k, w2vT)


SEL_CHUNK = 512
CHUNK_BLOCKS = SEL_CHUNK // SEL_BLOCK
WIN_KEYS = WINDOW + Q_BLOCK
QLANES = NSA_REP * Q_BLOCK
STEP_BLOCKS = 4
STEP_LANES = STEP_BLOCKS * QLANES
LANE_PARTS = tuple(slice(e * QLANES, (e + 1) * QLANES) for e in range(STEP_BLOCKS))
N_FORCED = 3
ONES_ROWS = 16
CMP_ROWS_STEP = 256
MASKED_MAX_FLOOR = -1e29


def _with_ones(vT):
    return jnp.concatenate([vT, jnp.ones((ONES_ROWS, vT.shape[1]), BF16)], axis=0)


def _attn_body(qT_ref, qrT_ref, kc_ref, vcT_ref, ksel_ref, vselT_ref, kwin_ref, vwinT_ref, gate_ref,
               ovT_ref, oh_ref, o_ref, qaug_ref, s0_ref, s1_ref, m_ref, acc_ref, part_ref, swin_ref,
               top0_ref, top1_ref):
    i = pl.program_id(2)
    t_step = i * (STEP_BLOCKS * Q_BLOCK)
    nblk = ovT_ref.shape[0]
    ncmp = kc_ref.shape[2]
    n_top = min(N_SEL, nblk) - N_FORCED

    def lanes_of_heads(ref, e):
        cols = slice(e * Q_BLOCK, (e + 1) * Q_BLOCK)
        parts = [ref[0, r * HEAD_DIM:(r + 1) * HEAD_DIM, cols] for r in range(NSA_REP)]
        return jnp.concatenate(
            [jnp.concatenate(parts, axis=1), jnp.zeros((LANES - HEAD_DIM, QLANES), BF16)], axis=0)

    def positions(e):
        return t_step + e * Q_BLOCK + lax.broadcasted_iota(jnp.int32, (1, Q_BLOCK), 1)

    pos_step = jnp.concatenate([positions(e) for e in range(STEP_BLOCKS) for _ in range(NSA_REP)], axis=1)
    d = HEAD_DIM

    def chunk_start(j):
        return pl.multiple_of(j * SEL_CHUNK, SEL_CHUNK)

    def scores_into(s_ref, top_ref, j, lanes=slice(None)):
        k0 = chunk_start(j)
        ka = jnp.concatenate([ksel_ref[0, pl.ds(k0, SEL_CHUNK), :], oh_ref[pl.ds(k0, SEL_CHUNK), :]], axis=1)
        s = _dot(ka, qaug_ref[:, lanes])
        s_ref[:, lanes] = s
        top_ref[:, lanes] = jnp.max(s, axis=0, keepdims=True)

    def block_select_and_window(e, cmp_rows, steady):
        t0 = t_step + e * Q_BLOCK
        lanes = slice(e * QLANES, (e + 1) * QLANES)
        qT = lanes_of_heads(qT_ref, e)
        qrT = lanes_of_heads(qrT_ref, e)
        pos1 = positions(e)
        pos = jnp.concatenate([pos1] * NSA_REP, axis=1)
        qaug_ref[0:LANES, lanes] = qrT
        s = _dot(kc_ref[0, 0, 0:cmp_rows, :], qT)
        w0 = pl.multiple_of(t0 - WINDOW if steady else jnp.maximum(t0 - WINDOW, 0), Q_BLOCK)
        swin_ref[e] = _dot(kwin_ref[0, pl.ds(w0, WIN_KEYS), :], qrT)
        s0_ref[:, lanes] = _dot(ksel_ref[0, 0:SEL_CHUNK, :], qrT)
        cend = lax.broadcasted_iota(jnp.int32, (cmp_rows, QLANES), 0) * CMP_STRIDE + (CMP_LEN - 1)
        s = jnp.where(cend <= pos, s, NEG_INF)
        m = jnp.maximum(jnp.max(s, axis=0, keepdims=True), MASKED_MAX_FLOOR)
        p = jnp.exp2(s - m)
        inv_l = 1.0 / jnp.maximum(jnp.sum(p, axis=0, keepdims=True), 1e-30)
        o_cmp = _dot(vcT_ref[0, 0, :, 0:cmp_rows], p.astype(BF16)) * inv_l

        pn = p * inv_l
        ph = pn[:, 0:Q_BLOCK]
        for r in range(1, NSA_REP):
            ph = ph + pn[:, r * Q_BLOCK:(r + 1) * Q_BLOCK]
        hi = ph.astype(BF16)
        lo = (ph - hi.astype(F32)).astype(BF16)
        ovT = ovT_ref[:, 0:cmp_rows]
        imp = _dot(ovT, hi) + _dot(ovT, lo)

        blk = lax.broadcasted_iota(jnp.int32, (nblk, Q_BLOCK), 0)
        cur = pos1 // SEL_BLOCK
        forced = (blk == 0) | (blk == cur) | (blk == cur - 1)
        v0 = jnp.where(forced, -1.0, jnp.where(blk * SEL_BLOCK <= pos1, imp, -1.0))
        n_piece = WIN_KEYS // Q_BLOCK
        kk = lax.broadcasted_iota(jnp.int32, (Q_BLOCK, QLANES), 0)
        qq = lax.broadcasted_iota(jnp.int32, (Q_BLOCK, QLANES), 1) % Q_BLOCK
        win = {"m": None, "acc": None}

        def piece_max(w):
            rows = slice(w * Q_BLOCK, (w + 1) * Q_BLOCK)
            sp = swin_ref[e, rows, :]
            if not steady:
                kpos = w0 + w * Q_BLOCK + kk
                sp = jnp.where(kpos <= pos, jnp.where(kpos > pos - WINDOW, sp, NEG_INF), NEG_INF)
            elif w == 0:
                sp = jnp.where(kk > qq, sp, NEG_INF)
            elif w == n_piece - 1:
                sp = jnp.where(kk <= qq, sp, NEG_INF)
            if not steady or w in (0, n_piece - 1):
                swin_ref[e, rows, :] = sp
            top = jnp.max(sp, axis=0, keepdims=True)
            win["m"] = top if win["m"] is None else jnp.maximum(win["m"], top)

        def piece_absorb(w):
            rows = slice(w * Q_BLOCK, (w + 1) * Q_BLOCK)
            p = jnp.exp2(swin_ref[e, rows, :] - win["m"]).astype(BF16)
            k0 = pl.multiple_of(w0 + w * Q_BLOCK, Q_BLOCK)
            pv = _dot(_with_ones(vwinT_ref[0, :, pl.ds(k0, Q_BLOCK)]), p)
            win["acc"] = pv if win["acc"] is None else win["acc"] + pv

        for w in range(n_piece):
            piece_max(w)
        left = v0
        pending = list(range(n_piece))
        for r in range(n_top):
            left = jnp.where(left == jnp.max(left, axis=0, keepdims=True), -2.0, left)
            if pending and r % 2 == 0:
                piece_absorb(pending.pop(0))
        for w in pending:
            piece_absorb(w)
        taken = left != v0
        n_taken = jnp.sum(jnp.where(taken, jnp.where(v0 >= 0.0, 1.0, 0.0), 0.0), axis=0, keepdims=True)

        def store_bias(left):
            bias = jnp.where(forced, 0.0, jnp.where(left != v0, 0.0, NEG_INF)).astype(BF16)
            bias = jnp.concatenate([bias] * NSA_REP, axis=1)
            if nblk < LANES:
                bias = jnp.concatenate([bias, jnp.zeros((LANES - nblk, QLANES), BF16)], axis=0)
            qaug_ref[LANES:2 * LANES, lanes] = bias

        store_bias(left)
        bias0 = jnp.where(forced[0:CHUNK_BLOCKS], 0.0,
                          jnp.where(left[0:CHUNK_BLOCKS] != v0[0:CHUNK_BLOCKS], 0.0, NEG_INF))
        bias0 = jnp.concatenate([bias0] * NSA_REP, axis=1)
        top = None
        for n in range(CHUNK_BLOCKS):
            rows = slice(n * SEL_BLOCK, (n + 1) * SEL_BLOCK)
            sb = s0_ref[rows, lanes] + bias0[n:n + 1, :]
            s0_ref[rows, lanes] = sb
            bt = jnp.max(sb, axis=0, keepdims=True)
            top = bt if top is None else jnp.maximum(top, bt)
        top0_ref[:, lanes] = top

        a_w = win["acc"]
        gate = gate_ref[0, 0, e]
        part_ref[:, lanes] = gate[0:1] * o_cmp + gate[2:3] * (a_w[0:d] / a_w[d:d + 1])

        def select_exactly():
            def drop_first_max(_, val):
                top = jnp.max(val, axis=0, keepdims=True)
                first = jnp.min(jnp.where(val == top, blk, nblk), axis=0, keepdims=True)
                return jnp.where(blk == first, -2.0, val)

            store_bias(lax.fori_loop(0, n_top, drop_first_max, v0))

        return jnp.max(n_taken) > n_top, select_exactly

    def select_and_window(cmp_rows, steady):
        redo = [block_select_and_window(e, cmp_rows, steady) for e in range(STEP_BLOCKS)]
        any_overflowed = False
        for overflowed, select_exactly in redo:
            pl.when(overflowed)(select_exactly)
            any_overflowed = jnp.logical_or(any_overflowed, overflowed)
        pl.when(any_overflowed)(functools.partial(scores_into, s0_ref, top0_ref, 0))

    step_tokens = STEP_BLOCKS * Q_BLOCK

    def visible_cmp(step):
        return (step * step_tokens + step_tokens - CMP_LEN) // CMP_STRIDE + 1

    n_steps = ksel_ref.shape[1] // step_tokens
    steady_from = -(-WINDOW // step_tokens)
    first_rows = min(CMP_ROWS_STEP, ncmp)
    assert visible_cmp(steady_from - 1) <= first_rows
    pl.when(i < steady_from)(functools.partial(select_and_window, first_rows, False))
    lo = steady_from
    for rows in range(first_rows, ncmp + 1, CMP_ROWS_STEP):
        hi = next((step for step in range(lo, n_steps) if visible_cmp(step) > rows), n_steps)
        if hi > lo:
            pl.when((i >= lo) & (i < hi))(functools.partial(select_and_window, rows, True))
            lo = hi
    assert lo == n_steps

    def absorb(s_ref, top_ref, j, diagonal, lanes=slice(None)):
        k0 = chunk_start(j)
        s = s_ref[:, lanes]
        if diagonal:
            kpos = k0 + lax.broadcasted_iota(jnp.int32, s.shape, 0)
            s = jnp.where(kpos <= pos_step[:, lanes], s, NEG_INF)
            top = jnp.max(s, axis=0, keepdims=True)
        else:
            top = top_ref[:, lanes]
        m = m_ref[:, lanes]
        m_new = jnp.maximum(m, top)
        p = jnp.exp2(s - m_new).astype(BF16)
        acc_ref[:, lanes] = jnp.exp2(m - m_new) * acc_ref[:, lanes] + _dot(
            _with_ones(vselT_ref[0, :, pl.ds(k0, SEL_CHUNK)]), p)
        m_ref[:, lanes] = m_new

    def absorb_own_positions(s_ref, j):
        k0 = chunk_start(j)
        kk = lax.broadcasted_iota(jnp.int32, (Q_BLOCK, QLANES), 0)
        qq = lax.broadcasted_iota(jnp.int32, (Q_BLOCK, QLANES), 1) % Q_BLOCK
        for e in range(STEP_BLOCKS):
            lanes = slice(e * QLANES, (e + 1) * QLANES)
            rows = (e + 1) * Q_BLOCK
            s = jnp.where(kk <= qq, s_ref[e * Q_BLOCK:rows, lanes], NEG_INF)
            if e > 0:
                s = jnp.concatenate([s_ref[0:e * Q_BLOCK, lanes], s], axis=0)
            m = m_ref[:, lanes]
            m_new = jnp.maximum(m, jnp.max(s, axis=0, keepdims=True))
            p = jnp.exp2(s - m_new).astype(BF16)
            acc_ref[:, lanes] = jnp.exp2(m - m_new) * acc_ref[:, lanes] + _dot(
                _with_ones(vselT_ref[0, :, pl.ds(k0, rows)]), p)
            m_ref[:, lanes] = m_new

    own_chunk = STEP_BLOCKS * Q_BLOCK == SEL_CHUNK

    def absorb_last(s_ref, top_ref, j):
        if own_chunk:
            absorb_own_positions(s_ref, j)
        else:
            absorb(s_ref, top_ref, j, True)

    def scores_last_into(s_ref, top_ref, j):
        if not own_chunk:
            scores_into(s_ref, top_ref, j)
            return
        k0 = chunk_start(j)
        for e in range(STEP_BLOCKS):
            lanes = slice(e * QLANES, (e + 1) * QLANES)
            rows = (e + 1) * Q_BLOCK
            ka = jnp.concatenate([ksel_ref[0, pl.ds(k0, rows), :], oh_ref[pl.ds(k0, rows), :]], axis=1)
            s_ref[0:rows, lanes] = _dot(ka, qaug_ref[:, lanes])

    last = t_step // SEL_CHUNK
    m_ref[...] = jnp.full(m_ref.shape, NEG_INF, F32)
    acc_ref[...] = jnp.zeros(acc_ref.shape, F32)

    def pair(t):
        for part in LANE_PARTS:
            scores_into(s1_ref, top1_ref, 2 * t + 1, part)
            absorb(s0_ref, top0_ref, 2 * t, False, part)
        for part in LANE_PARTS:
            scores_into(s0_ref, top0_ref, 2 * t + 2, part)
            absorb(s1_ref, top1_ref, 2 * t + 1, False, part)

    def pair_step(t, carry):
        pair(t)
        return carry

    lax.fori_loop(0, last // 2, pair_step, 0)

    @pl.when(last % 2 == 1)
    def _():
        scores_last_into(s1_ref, top1_ref, last)
        absorb(s0_ref, top0_ref, last - 1, False)
        absorb_last(s1_ref, top1_ref, last)

    @pl.when(last % 2 == 0)
    def _():
        absorb_last(s0_ref, top0_ref, last)

    a_s = acc_ref[...]
    gate_sel = jnp.concatenate([gate_ref[0, 0, e][1:2] for e in range(STEP_BLOCKS)], axis=1)
    out = part_ref[...] + gate_sel * (a_s[0:d] / a_s[d:d + 1])
    for e in range(STEP_BLOCKS):
        for r in range(NSA_REP):
            lane0 = e * QLANES + r * Q_BLOCK
            o_ref[0, r * HEAD_DIM:(r + 1) * HEAD_DIM, e * Q_BLOCK:(e + 1) * Q_BLOCK] = (
                out[:, lane0:lane0 + Q_BLOCK].astype(BF16))


def _nsa_attention(qT, qrT, kcmp, vcmpT, ksel, vselT, kwin, vwinT, gates):
    b, _, s = qT.shape
    nq = s // Q_BLOCK
    nblk = s // SEL_BLOCK
    ncmp = kcmp.shape[2]
    rows = NSA_REP * HEAD_DIM
    assert nblk <= LANES and s % SEL_CHUNK == 0 and s >= WIN_KEYS

    gt = gates[:, :, :NSA_HEADS * N_GATES].reshape(b, nq, Q_BLOCK, NSA_KV_GROUPS, NSA_REP, N_GATES)
    gt = gt.transpose(0, 3, 1, 5, 4, 2).reshape(b, NSA_KV_GROUPS, nq, N_GATES, QLANES)

    cstart = np.arange(ncmp) * CMP_STRIDE
    sstart = np.arange(nblk) * SEL_BLOCK
    ovT = ((cstart[None, :] < sstart[:, None] + SEL_BLOCK) & (cstart[None, :] + CMP_LEN > sstart[:, None]))
    ovT = jnp.asarray(ovT, BF16)
    onehot = jnp.asarray(np.arange(s)[:, None] // SEL_BLOCK == np.arange(LANES)[None, :], BF16)

    assert nq % STEP_BLOCKS == 0
    q_spec = pl.BlockSpec((1, rows, STEP_BLOCKS * Q_BLOCK), lambda bi, g, i: (bi, g, i))
    k_spec = pl.BlockSpec((1, s, LANES), lambda bi, g, i: (bi, 0, g))
    vT_spec = pl.BlockSpec((1, HEAD_DIM, s), lambda bi, g, i: (bi, g, 0))
    return pl.pallas_call(
        _attn_body,
        grid=(b, NSA_KV_GROUPS, nq // STEP_BLOCKS),
        in_specs=[q_spec, q_spec,
                  pl.BlockSpec((1, 1, ncmp, LANES), lambda bi, g, i: (bi, g, 0, 0)),
                  pl.BlockSpec((1, 1, HEAD_DIM, ncmp), lambda bi, g, i: (bi, g, 0, 0)),
                  k_spec, vT_spec, k_spec, vT_spec,
                  pl.BlockSpec((1, 1, STEP_BLOCKS, N_GATES, QLANES), lambda bi, g, i: (bi, g, i, 0, 0)),
                  _const_spec((nblk, ncmp)), _const_spec((s, LANES))],
        out_specs=q_spec,
        out_shape=jax.ShapeDtypeStruct((b, NSA_HEADS * HEAD_DIM, s), BF16),
        scratch_shapes=[pltpu.VMEM((2 * LANES, STEP_LANES), BF16),
                        pltpu.VMEM((SEL_CHUNK, STEP_LANES), F32), pltpu.VMEM((SEL_CHUNK, STEP_LANES), F32),
                        pltpu.VMEM((1, STEP_LANES), F32), pltpu.VMEM((HEAD_DIM + ONES_ROWS, STEP_LANES), F32),
                        pltpu.VMEM((HEAD_DIM, STEP_LANES), F32), pltpu.VMEM((STEP_BLOCKS, WIN_KEYS, QLANES), F32),
                        pltpu.VMEM((1, STEP_LANES), F32), pltpu.VMEM((1, STEP_LANES), F32)],
        compiler_params=_params(("arbitrary", "arbitrary", "arbitrary")),
        name="nsa_attention",
    )(qT, qrT, kcmp, vcmpT, ksel, vselT, kwin, vwinT, gt, ovT, onehot)


def _nsa_mixer(x, g_pre, w_in, b_gate, cmp_pos, cmp_w1, cmp_w2):
    qT, qrT, ksel, kwin, vselT, vwinT, kc_raw, vc_raw, gates = _nsa_project(x, g_pre, w_in, b_gate)
    kcmp, vcmpT = _nsa_compress(kc_raw, vc_raw, cmp_pos, cmp_w1, cmp_w2)
    return _nsa_attention(qT, qrT, kcmp, vcmpT, ksel, vselT, kwin, vwinT, gates)


def kernel(x, mix_norm_pre, mix_norm_post, ffn_norm_pre, ffn_norm_post, ffn_w_gate, ffn_w_up, ffn_w_down,
           conv_w_pw1, conv_b_pw1, conv_w_dw, conv_b_dw, conv_ln_g, conv_ln_b, conv_w_pw2, conv_b_pw2,
           nsa_w_in, nsa_b_gate, nsa_cmp_pos, nsa_cmp_w1, nsa_cmp_w2, nsa_w_out):
    b, s, d = x.shape
    depth = mix_norm_pre.shape[0]
    n_mixers = 2

    w_gate, w_up, w_down = ffn_w_gate, ffn_w_up, ffn_w_down

    def ffn(x, i, half, attn_out=None):
        y = _ffn_half(x.reshape(b * s, d), ffn_norm_pre[i, half], ffn_norm_post[i, half],
                      w_gate, w_up, w_down, (i, half), attn_out)
        return y.reshape(b, s, d)

    for i in range(depth):
        x = ffn(x, i, 0)
        j = i // n_mixers
        if i % n_mixers == 0:
            x = _conv_mixer(x, mix_norm_pre[i], mix_norm_post[i], conv_w_pw1[j], conv_b_pw1[j], conv_w_dw[j],
                            conv_b_dw[j], conv_ln_g[j], conv_ln_b[j], conv_w_pw2[j], conv_b_pw2[j])
            x = ffn(x, i, 1)
        else:
            attnT = _nsa_mixer(x, mix_norm_pre[i], nsa_w_in[j], nsa_b_gate[j], nsa_cmp_pos[j],
                               nsa_cmp_w1[j], nsa_cmp_w2[j])
            x = ffn(x, i, 1, (attnT, nsa_w_out[j], mix_norm_post[i]))
    return x
```
